```python
import math
import jax
import jax.numpy as jnp
from jax import lax
import numpy as np

D_MODEL = 1024
BATCH = 8
SEQ = 4096
DEPTH = 2

CHUNK = 64
Q_BLOCK = 128
ROPE_THETA = 10000.0
NORM_EPS = 1e-6

MLA_HEADS = 8
MLA_Q_LORA = 256
MLA_KV_LORA = 256
MLA_NOPE = 64
MLA_ROPE = 32
MLA_V = 64
MLA_QK = MLA_NOPE + MLA_ROPE

DIFF_HEADS = 4
DIFF_HD = 64
DIFF_V = 2 * DIFF_HD
DIFF_QK_W = DIFF_HEADS * 2 * DIFF_HD

EV_IN = MLA_Q_LORA + MLA_KV_LORA + MLA_ROPE + 3 * DIFF_QK_W
EV_MIX = MLA_HEADS * MLA_V + DIFF_HEADS * DIFF_V

RWKV_HEADS = 8
RWKV_HD = 64
RWKV_W = RWKV_HEADS * RWKV_HD
RWKV_DECAY_LORA = 64
RWKV_A_LORA = 64
RWKV_GATE_LORA = 128
RWKV_IN = 3 * RWKV_W + RWKV_DECAY_LORA + RWKV_A_LORA + RWKV_GATE_LORA
RWKV_GN_EPS = 64e-5

HGRN_HEADS = 4
HGRN_DK = 128
HGRN_DV = 128
HGRN_IN = 2 * HGRN_HEADS * HGRN_DK + 2 * HGRN_HEADS * HGRN_DV

OD_IN = RWKV_IN + HGRN_IN
OD_MIX = RWKV_W + HGRN_HEADS * HGRN_DV

MOE_GROUPS = 4
MOE_EXPERTS_PER_GROUP = 8
MOE_EXPERTS = MOE_GROUPS * MOE_EXPERTS_PER_GROUP
MOE_TOP_K = 2
MOE_FF = 512
MOE_BLOCK = 128

F32 = jnp.float32

kernel_name = 'hybrid_mla_diff_rwkv7_hgrn2_hmoe'


def rmsnorm(x, g, eps=NORM_EPS):
    xf = x.astype(F32)
    y = xf * lax.rsqrt(jnp.mean(xf * xf, axis=-1, keepdims=True) + eps)
    return (y * g.astype(F32)).astype(x.dtype)


def rope(x, pos):
    d = x.shape[-1]
    half = d // 2
    inv = jnp.power(ROPE_THETA, -jnp.arange(half, dtype=F32) / half)
    ang = pos.astype(F32)[:, None] * inv[None, :]
    ang = ang.reshape((ang.shape[0],) + (1,) * (x.ndim - 3) + (half,))
    cos, sin = jnp.cos(ang), jnp.sin(ang)
    xf = x.astype(F32)
    x1, x2 = xf[..., :half], xf[..., half:]
    return jnp.concatenate([x1 * cos - x2 * sin, x2 * cos + x1 * sin], axis=-1).astype(x.dtype)


def chunk_causal_attention(q, k, v):
    B, H, S, DK = q.shape
    DV = v.shape[-1]
    nb = S // Q_BLOCK
    scale = DK ** -0.5
    qb = q.reshape(B, H, nb, Q_BLOCK, DK).transpose(2, 0, 1, 3, 4)
    key_chunk = jnp.arange(S) // CHUNK

    def block(args):
        qi, bi = args
        s = jnp.einsum('bhqd,bhkd->bhqk', qi, k).astype(F32) * scale
        q_chunk = (bi * Q_BLOCK + jnp.arange(Q_BLOCK)) // CHUNK
        s = jnp.where(key_chunk[None, :] <= q_chunk[:, None], s, -jnp.inf)
        p = jax.nn.softmax(s, axis=-1).astype(v.dtype)
        return jnp.einsum('bhqk,bhkd->bhqd', p, v)

    out = lax.map(block, (qb, jnp.arange(nb)))
    return out.transpose(1, 2, 0, 3, 4).reshape(B, H, S, DV)


def _bhsd(a):
    return a.transpose(0, 2, 1, 3)


def mla_diff_mixer(u, layer_idx, q_norm, w_q_up, kv_norm, w_kv_up, lam, subln):
    B, S, _ = u.shape
    pos = jnp.arange(S)
    s1 = MLA_Q_LORA
    s2 = s1 + MLA_KV_LORA
    s3 = s2 + MLA_ROPE
    s4 = s3 + DIFF_QK_W
    s5 = s4 + DIFF_QK_W
    c_q, c_kv, k_r, dq, dk, dv = jnp.split(u, [s1, s2, s3, s4, s5], axis=-1)

    q = (rmsnorm(c_q, q_norm) @ w_q_up).reshape(B, S, MLA_HEADS, MLA_QK)
    q = jnp.concatenate([q[..., :MLA_NOPE], rope(q[..., MLA_NOPE:], pos)], axis=-1)
    kv = (rmsnorm(c_kv, kv_norm) @ w_kv_up).reshape(B, S, MLA_HEADS, MLA_NOPE + MLA_V)
    k_rope = jnp.broadcast_to(rope(k_r, pos)[:, :, None, :], (B, S, MLA_HEADS, MLA_ROPE))
    k = jnp.concatenate([kv[..., :MLA_NOPE], k_rope], axis=-1)
    v = kv[..., MLA_NOPE:]
    o_mla = _bhsd(chunk_causal_attention(_bhsd(q), _bhsd(k), _bhsd(v)))
    o_mla = o_mla.reshape(B, S, MLA_HEADS * MLA_V)

    dq = rope(dq.reshape(B, S, DIFF_HEADS, 2, DIFF_HD), pos)
    dk = rope(dk.reshape(B, S, DIFF_HEADS, 2, DIFF_HD), pos)
    dv = _bhsd(dv.reshape(B, S, DIFF_HEADS, DIFF_V))
    a1 = chunk_causal_attention(_bhsd(dq[..., 0, :]), _bhsd(dk[..., 0, :]), dv)
    a2 = chunk_causal_attention(_bhsd(dq[..., 1, :]), _bhsd(dk[..., 1, :]), dv)
    lam = lam.astype(F32)
    lam_init = 0.8 - 0.6 * math.exp(-0.3 * layer_idx)
    lam_full = jnp.exp(jnp.sum(lam[0] * lam[1])) - jnp.exp(jnp.sum(lam[2] * lam[3])) + lam_init
    o = a1.astype(F32) - lam_full * a2.astype(F32)
    o = rmsnorm(_bhsd(o), subln) * (1.0 - lam_init)
    o_diff = o.reshape(B, S, DIFF_HEADS * DIFF_V).astype(u.dtype)
    return jnp.concatenate([o_mla, o_diff.astype(o_mla.dtype)], axis=-1)


def token_shift(u, mu):
    u_prev = jnp.pad(u, ((0, 0), (1, 0), (0, 0)))[:, :-1]
    return u + (u_prev - u) * mu


def rwkv7_recurrence(r, w, k, v, a, b):
    B, S, H, N = r.shape

    def step(state, inp):
        r_t, w_t, k_t, v_t, a_t, b_t = inp
        sa = jnp.einsum('bhvk,bhk->bhv', state, a_t)
        state = (state * w_t[:, :, None, :] + sa[..., None] * b_t[:, :, None, :]
                 + v_t[..., None] * k_t[:, :, None, :])
        y = jnp.einsum('bhvk,bhk->bhv', state, r_t)
        return state, y

    xs = tuple(t.astype(F32).transpose(1, 0, 2, 3) for t in (r, w, k, v, a, b))
    _, ys = lax.scan(step, jnp.zeros((B, H, N, N), F32), xs)
    return ys.transpose(1, 0, 2, 3)


def rwkv7_mixer(u, mu, w0, w2, a0, a2, g2, k_k, k_a, r_k, ln_w, ln_b):
    B, S, _ = u.shape
    u = token_shift(u, mu)
    p1 = RWKV_W
    p2 = 2 * RWKV_W
    p3 = 3 * RWKV_W
    p4 = p3 + RWKV_DECAY_LORA
    p5 = p4 + RWKV_A_LORA
    r, k, v, xw, xa, xg = jnp.split(u, [p1, p2, p3, p4, p5], axis=-1)
    w_raw = -jax.nn.softplus(-(w0 + jnp.tanh(xw) @ w2)) - 0.5
    decay = jnp.exp(-jnp.exp(w_raw.astype(F32)))
    a = jax.nn.sigmoid((a0 + xa @ a2).astype(F32))
    g = jax.nn.sigmoid(xg) @ g2

    def heads(t):
        return t.reshape(B, S, RWKV_HEADS, RWKV_HD)

    kk = heads((k * k_k).astype(F32))
    kk = kk * lax.rsqrt(jnp.maximum(jnp.sum(kk * kk, axis=-1, keepdims=True), 1e-24))
    k = k.astype(F32) * (1.0 + (a - 1.0) * k_a.astype(F32))
    r4, k4, v4, a4 = heads(r.astype(F32)), heads(k), heads(v.astype(F32)), heads(a)
    y = rwkv7_recurrence(r4, heads(decay), k4, v4, -kk, kk * a4)
    mean = jnp.mean(y, axis=-1, keepdims=True)
    var = jnp.mean(jnp.square(y - mean), axis=-1, keepdims=True)
    y = ((y - mean) * lax.rsqrt(var + RWKV_GN_EPS)).reshape(B, S, RWKV_W)
    y = y * ln_w.astype(F32) + ln_b.astype(F32)
    bonus = jnp.sum(r4 * k4 * r_k.astype(F32), axis=-1, keepdims=True) * v4
    y = y + bonus.reshape(B, S, RWKV_W)
    return (y * g.astype(F32)).astype(u.dtype)


def hgrn2_chunkwise(q, k, v, log_f):
    B, S, H, DK = q.shape
    DV = v.shape[-1]
    nc = S // CHUNK

    def chunks(t):
        return t.astype(F32).reshape(B, nc, CHUNK, H, t.shape[-1]).transpose(1, 0, 3, 2, 4)

    causal = jnp.tril(jnp.ones((CHUNK, CHUNK), dtype=bool))[:, :, None]

    def step(state, inp):
        qc, kc, vc, gc = inp
        b = jnp.cumsum(gc, axis=2)
        o_inter = jnp.einsum('bhtd,bhde->bhte', qc * jnp.exp(b), state)
        rel = jnp.where(causal, b[:, :, :, None, :] - b[:, :, None, :, :], -jnp.inf)
        scores = jnp.einsum('bhtd,bhtsd,bhsd->bhts', qc, jnp.exp(rel), kc)
        o_intra = jnp.einsum('bhts,bhse->bhte', scores, vc)
        b_last = b[:, :, -1:, :]
        state = (jnp.exp(b_last[:, :, 0, :])[..., None] * state
                 + jnp.einsum('bhsd,bhse->bhde', kc * jnp.exp(b_last - b), vc))
        return state, o_inter + o_intra

    init = jnp.zeros((B, H, DK, DV), F32)
    _, o = lax.scan(step, init, (chunks(q), chunks(k), chunks(v), chunks(log_f)))
    return o.transpose(1, 0, 3, 2, 4).reshape(B, S, H, DV)


def hgrn2_mixer(u, lower_bound, o_norm):
    B, S, _ = u.shape
    c1 = HGRN_HEADS * HGRN_DK
    c2 = 2 * c1
    c3 = c2 + HGRN_HEADS * HGRN_DV
    q, f, i, g = jnp.split(u, [c1, c2, c3], axis=-1)
    q = jax.nn.silu(q)
    z = f.astype(F32)
    lb = lower_bound.astype(F32)
    log_f = jnp.logaddexp(jnp.log(lb), jnp.log1p(-lb) + jax.nn.log_sigmoid(z))
    key_in = (1.0 - lb) * jax.nn.sigmoid(-z)

    def heads(t, d):
        return t.reshape(B, S, HGRN_HEADS, d)

    o = hgrn2_chunkwise(heads(q, HGRN_DK), heads(key_in, HGRN_DK),
                        heads(i, HGRN_DV), heads(log_f, HGRN_DK))
    o = rmsnorm(o, o_norm.reshape(HGRN_HEADS, HGRN_DV)).reshape(B, S, HGRN_HEADS * HGRN_DV)
    return (o * jax.nn.silu(g.astype(F32))).astype(u.dtype)


def hier_moe(h, w_group, b_group, w_expert, b_expert, w_gate, w_up, w_down):
    B, S, D = h.shape
    zg = (h @ w_group + b_group).astype(F32)
    pg = jax.nn.softmax(zg, axis=-1)
    g_idx = jnp.argmax(zg, axis=-1)
    pg_top = jnp.take_along_axis(pg, g_idx[..., None], axis=-1)
    ze = (h @ w_expert + b_expert).astype(F32).reshape(B, S, MOE_GROUPS, MOE_EXPERTS_PER_GROUP)
    ze = jnp.take_along_axis(ze, g_idx[..., None, None], axis=2)[..., 0, :]
    pe = jax.nn.softmax(ze, axis=-1)
    top_v, top_i = lax.top_k(pe, MOE_TOP_K)
    top_v = top_v / jnp.sum(top_v, axis=-1, keepdims=True)
    within = jnp.sum(jax.nn.one_hot(top_i, MOE_EXPERTS_PER_GROUP, dtype=F32) * top_v[..., None], axis=-2)
    gate = jax.nn.one_hot(g_idx, MOE_GROUPS, dtype=F32)[..., :, None] * (pg_top * within)[..., None, :]
    gate = gate.reshape(B, S, MOE_EXPERTS).astype(h.dtype)
    nb = S // MOE_BLOCK
    hb = h.reshape(B, nb, MOE_BLOCK, D).transpose(1, 0, 2, 3)
    gb = gate.reshape(B, nb, MOE_BLOCK, MOE_EXPERTS).transpose(1, 0, 2, 3)

    def block(args):
        hx, gx = args
        a = jnp.einsum('bmd,edf->bmef', hx, w_gate)
        up = jnp.einsum('bmd,edf->bmef', hx, w_up)
        act = jax.nn.silu(a) * up * gx[..., None]
        return jnp.einsum('bmef,efd->bmd', act, w_down)

    y = lax.map(block, (hb, gb))
    return y.transpose(1, 0, 2, 3).reshape(B, S, D)


def setup_inputs(seed: int = 0) -> dict:
    key = jax.random.key(seed)
    keys = jax.random.split(key, 48)
    ctr = [0]

    def nxt():
        kk = keys[ctr[0]]
        ctr[0] += 1
        return kk

    def nrm(shape, scale):
        return jax.random.normal(nxt(), shape, F32) * scale

    def gain(shape):
        return 1.0 + nrm(shape, 0.02)

    n_even = (DEPTH + 1) // 2
    n_odd = DEPTH // 2
    D = D_MODEL
    return {
        'x': nrm((BATCH, SEQ, D), 1.0),
        'norm_mix': gain((DEPTH, D)),
        'norm_ffn': gain((DEPTH, D)),
        'norm_final': gain((D,)),
        'ev_w_in': nrm((n_even, D, EV_IN), D ** -0.5),
        'ev_w_out': nrm((n_even, EV_MIX, D), EV_MIX ** -0.5),
        'mla_q_norm': gain((n_even, MLA_Q_LORA)),
        'mla_w_q_up': nrm((n_even, MLA_Q_LORA, MLA_HEADS * MLA_QK), MLA_Q_LORA ** -0.5),
        'mla_kv_norm': gain((n_even, MLA_KV_LORA)),
        'mla_w_kv_up': nrm((n_even, MLA_KV_LORA, MLA_HEADS * (MLA_NOPE + MLA_V)), MLA_KV_LORA ** -0.5),
        'diff_lambda': nrm((n_even, 4, DIFF_HD), 0.1),
        'diff_subln': gain((n_even, DIFF_V)),
        'od_w_in': nrm((n_odd, D, OD_IN), D ** -0.5),
        'od_w_out': nrm((n_odd, OD_MIX, D), OD_MIX ** -0.5),
        'rwkv_mu': jax.random.uniform(nxt(), (n_odd, RWKV_IN), F32),
        'rwkv_w0': nrm((n_odd, RWKV_W), 0.5),
        'rwkv_w2': nrm((n_odd, RWKV_DECAY_LORA, RWKV_W), 0.5 * RWKV_DECAY_LORA ** -0.5),
        'rwkv_a0': nrm((n_odd, RWKV_W), 0.1),
        'rwkv_a2': nrm((n_odd, RWKV_A_LORA, RWKV_W), 0.5 * RWKV_A_LORA ** -0.5),
        'rwkv_g2': nrm((n_odd, RWKV_GATE_LORA, RWKV_W), RWKV_GATE_LORA ** -0.5),
        'rwkv_k_k': 0.85 + nrm((n_odd, RWKV_W), 0.02),
        'rwkv_k_a': 1.0 + nrm((n_odd, RWKV_W), 0.02),
        'rwkv_r_k': nrm((n_odd, RWKV_HEADS, RWKV_HD), 0.1),
        'rwkv_ln_w': gain((n_odd, RWKV_W)),
        'rwkv_ln_b': nrm((n_odd, RWKV_W), 0.02),
        'hgrn_lb': 1.0 + nrm((DEPTH, HGRN_HEADS * HGRN_DK), 0.1),
        'hgrn_o_norm': gain((n_odd, HGRN_HEADS * HGRN_DV)),
        'moe_w_group': nrm((DEPTH, D, MOE_GROUPS), D ** -0.5),
        'moe_b_group': nrm((DEPTH, MOE_GROUPS), 0.01),
        'moe_w_expert': nrm((DEPTH, D, MOE_EXPERTS), D ** -0.5),
        'moe_b_expert': nrm((DEPTH, MOE_EXPERTS), 0.01),
        'moe_w_gate': nrm((DEPTH, MOE_EXPERTS, D, MOE_FF), D ** -0.5),
        'moe_w_up': nrm((DEPTH, MOE_EXPERTS, D, MOE_FF), D ** -0.5),
        'moe_w_down': nrm((DEPTH, MOE_EXPERTS, MOE_FF, D), MOE_FF ** -0.5),
    }


def reference(x, norm_mix, norm_ffn, norm_final, ev_w_in, ev_w_out, mla_q_norm, mla_w_q_up,
              mla_kv_norm, mla_w_kv_up, diff_lambda, diff_subln, od_w_in, od_w_out, rwkv_mu,
              rwkv_w0, rwkv_w2, rwkv_a0, rwkv_a2, rwkv_g2, rwkv_k_k, rwkv_k_a, rwkv_r_k,
              rwkv_ln_w, rwkv_ln_b, hgrn_lb, hgrn_o_norm, moe_w_group, moe_b_group,
              moe_w_expert, moe_b_expert, moe_w_gate, moe_w_up, moe_w_down):
    lb_p = jax.nn.softmax(hgrn_lb.astype(F32), axis=0)
    lower_bounds = jnp.cumsum(lb_p, axis=0) - lb_p[0]
    h = x
    for l in range(DEPTH):
        i = l // 2
        hn = rmsnorm(h, norm_mix[l])
        if l % 2 == 0:
            u = hn @ ev_w_in[i]
            mixed = mla_diff_mixer(u, l, mla_q_norm[i], mla_w_q_up[i], mla_kv_norm[i],
                                   mla_w_kv_up[i], diff_lambda[i], diff_subln[i])
            h = h + (mixed.astype(h.dtype) @ ev_w_out[i]).astype(h.dtype)
        else:
            u = hn @ od_w_in[i]
            o_c = rwkv7_mixer(u[..., :RWKV_IN], rwkv_mu[i], rwkv_w0[i], rwkv_w2[i], rwkv_a0[i],
                              rwkv_a2[i], rwkv_g2[i], rwkv_k_k[i], rwkv_k_a[i], rwkv_r_k[i],
                              rwkv_ln_w[i], rwkv_ln_b[i])
            o_d = hgrn2_mixer(u[..., RWKV_IN:], lower_bounds[l], hgrn_o_norm[i])
            mixed = jnp.concatenate([o_c, o_d], axis=-1).astype(h.dtype)
            h = h + (mixed @ od_w_out[i]).astype(h.dtype)
        hf = rmsnorm(h, norm_ffn[l])
        h = h + hier_moe(hf, moe_w_group[l], moe_b_group[l], moe_w_expert[l], moe_b_expert[l],
                         moe_w_gate[l], moe_w_up[l], moe_w_down[l]).astype(h.dtype)
    return rmsnorm(h, norm_final)
```

```python
import functools
import math

import jax
import jax.numpy as jnp
from jax import lax
from jax.experimental import pallas as pl
from jax.experimental.pallas import tpu as pltpu

F32 = jnp.float32
BF16 = jnp.bfloat16

CHUNK = 64
ROPE_THETA = 10000.0
NORM_EPS = 1e-6
MLA_HEADS = 8
MLA_LORA = 256
MLA_NOPE = 64
MLA_ROPE = 32
MLA_V = 64
MLA_QK = MLA_NOPE + MLA_ROPE
DIFF_HEADS = 4
DIFF_HD = 64
DIFF_V = 2 * DIFF_HD
DIFF_W = DIFF_HEADS * 2 * DIFF_HD
RWKV_HEADS = 8
RWKV_HD = 64
RWKV_W = RWKV_HEADS * RWKV_HD
RWKV_DECAY_LORA = 64
RWKV_A_LORA = 64
RWKV_GATE_LORA = 128
RWKV_IN = 3 * RWKV_W + RWKV_DECAY_LORA + RWKV_A_LORA + RWKV_GATE_LORA
RWKV_GN_EPS = 64e-5
HGRN_HEADS = 4
HGRN_D = 128
HGRN_W = HGRN_HEADS * HGRN_D
HGRN_IN = 4 * HGRN_W
MOE_GROUPS = 4
MOE_EPG = 8
MOE_EXPERTS = MOE_GROUPS * MOE_EPG
MOE_FF = 512

LANES = 128
ROW_TILE = 256
ATTN_BLOCK = 256
EXPERT_TILE = 256
SUB = 16
VMEM_LIMIT = 48 * 1024 * 1024
NEG = -1e30


def _cparams(sem):
    return pltpu.CompilerParams(dimension_semantics=sem, vmem_limit_bytes=VMEM_LIMIT)


def _dot(a, b):
    return jnp.dot(a, b, preferred_element_type=F32)


def _dot_nt(a, b):
    return lax.dot_general(a, b, (((1,), (1,)), ((), ())), preferred_element_type=F32)


def _dot_tn(a, b):
    return lax.dot_general(a, b, (((0,), (0,)), ((), ())), preferred_element_type=F32)


def _rms(x, g):
    return x * lax.rsqrt(jnp.mean(x * x, axis=-1, keepdims=True) + NORM_EPS) * g


def _sigmoid(x):
    return 1.0 / (1.0 + jnp.exp(-x))


def _split_dot(x, w_bf16, left=False):
    hi = x.astype(BF16)
    lo = (x - hi.astype(F32)).astype(BF16)
    if left:
        return _dot(w_bf16, hi) + _dot(w_bf16, lo)
    return _dot(hi, w_bf16) + _dot(lo, w_bf16)


def _tile_lanes(t, n):
    return jnp.concatenate([t] * n, axis=-1) if n > 1 else t


def _rope_rot(x, half):
    w = x.shape[-1]
    lane = lax.broadcasted_iota(jnp.int32, x.shape, x.ndim - 1)
    first = (lane & (2 * half - 1)) < half
    return jnp.where(first, pltpu.roll(x, w - half, x.ndim - 1), pltpu.roll(x, half, x.ndim - 1))


def _ev_in_kernel(x_ref, g_ref, w0_ref, qg_ref, kvg_ref, wq_ref, wk_ref, wv_ref,
                  cd_ref, sd_ref, cm_ref, sm_ref,
                  qf_ref, kf_ref, vm_ref, dq_ref, dk_ref, dv_ref):
    hn = _rms(x_ref[...], g_ref[...]).astype(BF16)
    u = _dot(hn, w0_ref[...])
    c_q = u[:, 0:256]
    c_kv = u[:, 256:512]
    kr = u[:, 512:640]
    dq = u[:, 640:1152]
    dk = u[:, 1152:1664]
    dv = u[:, 1664:2176]
    cqn = _rms(c_q, qg_ref[...]).astype(BF16)
    ckn = _rms(c_kv, kvg_ref[...]).astype(BF16)
    q = _dot(cqn, wq_ref[...])
    k = _dot(ckn, wk_ref[...])
    v = _dot(ckn, wv_ref[...])
    cm = cm_ref[...]
    sm = sm_ref[...]
    m_half = MLA_ROPE // 2
    lane = lax.broadcasted_iota(jnp.int32, q.shape, 1) & (LANES - 1)
    first = (lane >= MLA_NOPE) & (lane < MLA_NOPE + m_half)
    rot_q = jnp.where(first, pltpu.roll(q, q.shape[1] - m_half, 1), pltpu.roll(q, m_half, 1))
    q = (q * _tile_lanes(cm, MLA_HEADS) + rot_q * _tile_lanes(sm, MLA_HEADS)) * (MLA_QK ** -0.5)
    lane1 = lax.broadcasted_iota(jnp.int32, kr.shape, 1)
    first1 = (lane1 >= MLA_NOPE) & (lane1 < MLA_NOPE + m_half)
    rot_k = jnp.where(first1, pltpu.roll(kr, LANES - m_half, 1), pltpu.roll(kr, m_half, 1))
    kr = kr * cm + rot_k * sm
    k = k + _tile_lanes(kr, MLA_HEADS)
    qf_ref[...] = q.astype(BF16)
    kf_ref[...] = k.astype(BF16)
    vm_ref[...] = v.astype(BF16)
    cd = _tile_lanes(cd_ref[...], DIFF_W // LANES)
    sd = _tile_lanes(sd_ref[...], DIFF_W // LANES)
    dq = (dq * cd + _rope_rot(dq, DIFF_HD // 2) * sd) * (DIFF_HD ** -0.5)
    dk = dk * cd + _rope_rot(dk, DIFF_HD // 2) * sd
    dq_ref[...] = dq.astype(BF16)
    dk_ref[...] = dk.astype(BF16)
    dv_ref[...] = dv.astype(BF16)


def _ev_in(x2, g, w0, qg, kvg, wq, wk, wv, cd, sd, cm, sm, seq):
    n, d = x2.shape
    tm = ROW_TILE
    nt = seq // tm
    row = lambda i: (i, 0)
    full = lambda i: (0, 0)
    tab = lambda i: (i % nt, 0)
    wspec = lambda a: pl.BlockSpec(a.shape, full)
    outs = [(MLA_HEADS * LANES,), (MLA_HEADS * LANES,), (MLA_HEADS * MLA_V,), (DIFF_W,), (DIFF_W,), (DIFF_W,)]
    return pl.pallas_call(
        _ev_in_kernel,
        out_shape=[jax.ShapeDtypeStruct((n, o[0]), BF16) for o in outs],
        grid=(n // tm,),
        in_specs=[pl.BlockSpec((tm, d), row), wspec(g), wspec(w0), wspec(qg), wspec(kvg), wspec(wq), wspec(wk),
                  wspec(wv)] + [pl.BlockSpec((tm, LANES), tab)] * 4,
        out_specs=[pl.BlockSpec((tm, o[0]), row) for o in outs],
        compiler_params=_cparams(("parallel",)),
        name="ev_in",
    )(x2, g, w0, qg, kvg, wq, wk, wv, cd, sd, cm, sm)


def _online_update(s, vb, m_ref, l_ref, acc_ref, idx):
    m_prev = m_ref[idx]
    m_new = jnp.maximum(m_prev, jnp.max(s, axis=-1, keepdims=True))
    alpha = jnp.exp(m_prev - m_new)
    p = jnp.exp(s - m_new)
    l_ref[idx] = alpha * l_ref[idx] + jnp.sum(p, axis=-1, keepdims=True)
    acc_ref[idx] = alpha * acc_ref[idx] + _dot(p.astype(BF16), vb)
    m_ref[idx] = m_new


def _chunk_mask(shape):
    sh = int(math.log2(CHUNK))
    r = jnp.right_shift(lax.broadcasted_iota(jnp.int32, shape, 0), sh)
    c = jnp.right_shift(lax.broadcasted_iota(jnp.int32, shape, 1), sh)
    return c <= r


def _attn_loop(i, q_a, q_b, k_ref, v_ref, m_ref, l_ref, acc_ref, split_k):
    bk = ATTN_BLOCK
    m_ref[...] = jnp.full(m_ref.shape, NEG, F32)
    l_ref[...] = jnp.zeros(l_ref.shape, F32)
    acc_ref[...] = jnp.zeros(acc_ref.shape, F32)

    def scores(off):
        kb = k_ref[0, pl.ds(off, bk), :]
        vb = v_ref[0, pl.ds(off, bk), :]
        if split_k:
            return _dot_nt(q_a, kb[:, :LANES]), _dot_nt(q_b, kb[:, LANES:]), vb
        return _dot_nt(q_a, kb), _dot_nt(q_b, kb), vb

    def body(j, carry):
        sa, sb, vb = scores(pl.multiple_of(j * bk, bk))
        _online_update(sa, vb, m_ref, l_ref, acc_ref, 0)
        _online_update(sb, vb, m_ref, l_ref, acc_ref, 1)
        return carry

    lax.fori_loop(0, i, body, 0)
    sa, sb, vb = scores(pl.multiple_of(i * bk, bk))
    mask = _chunk_mask(sa.shape)
    _online_update(jnp.where(mask, sa, NEG), vb, m_ref, l_ref, acc_ref, 0)
    _online_update(jnp.where(mask, sb, NEG), vb, m_ref, l_ref, acc_ref, 1)
    return acc_ref[0] / l_ref[0], acc_ref[1] / l_ref[1]


def _mla_attn_kernel(q_ref, k_ref, v_ref, o_ref, m_ref, l_ref, acc_ref):
    q = q_ref[0]
    o_a, o_b = _attn_loop(pl.program_id(2), q[:, :LANES], q[:, LANES:], k_ref, v_ref,
                          m_ref, l_ref, acc_ref, True)
    lane = lax.broadcasted_iota(jnp.int32, o_a.shape, 1)
    o_ref[0] = jnp.where(lane < MLA_V, o_a, o_b).astype(BF16)


def _diff_attn_kernel(q_ref, k_ref, v_ref, lam_ref, sub_ref, o_ref, m_ref, l_ref, acc_ref, *, lam_init):
    q = q_ref[0]
    lane = lax.broadcasted_iota(jnp.int32, q.shape, 1)
    zero = jnp.zeros_like(q)
    a1, a2 = _attn_loop(pl.program_id(2), jnp.where(lane < DIFF_HD, q, zero), jnp.where(lane >= DIFF_HD, q, zero),
                        k_ref, v_ref, m_ref, l_ref, acc_ref, False)
    lam = lam_ref[...]
    s1 = jnp.sum(lam[0:1] * lam[1:2], axis=-1, keepdims=True)
    s2 = jnp.sum(lam[2:3] * lam[3:4], axis=-1, keepdims=True)
    lam_full = jnp.exp(s1) - jnp.exp(s2) + lam_init
    o = a1 - lam_full * a2
    o_ref[0] = (_rms(o, sub_ref[...]) * (1.0 - lam_init)).astype(BF16)


def _attn_scratch(bq):
    return [pltpu.VMEM((2, bq, 1), F32), pltpu.VMEM((2, bq, 1), F32), pltpu.VMEM((2, bq, LANES), F32)]


def _mla_attn(qf, kf, vm):
    b, s, _ = qf.shape
    bq = ATTN_BLOCK
    return pl.pallas_call(
        _mla_attn_kernel,
        out_shape=jax.ShapeDtypeStruct((b, s, MLA_HEADS * MLA_V), BF16),
        grid=(b, MLA_HEADS // 2, s // bq),
        in_specs=[pl.BlockSpec((1, bq, 2 * LANES), lambda bi, h, i: (bi, i, h)),
                  pl.BlockSpec((1, s, 2 * LANES), lambda bi, h, i: (bi, 0, h)),
                  pl.BlockSpec((1, s, LANES), lambda bi, h, i: (bi, 0, h))],
        out_specs=pl.BlockSpec((1, bq, LANES), lambda bi, h, i: (bi, i, h)),
        scratch_shapes=_attn_scratch(bq),
        compiler_params=_cparams(("parallel", "parallel", "arbitrary")),
        name="mla_attn",
    )(qf, kf, vm)


def _diff_attn(dq, dk, dv, lam, subln, lam_init):
    b, s, _ = dq.shape
    bq = ATTN_BLOCK
    blk = lambda rows, im: pl.BlockSpec((1, rows, LANES), im)
    return pl.pallas_call(
        functools.partial(_diff_attn_kernel, lam_init=lam_init),
        out_shape=jax.ShapeDtypeStruct((b, s, DIFF_HEADS * DIFF_V), BF16),
        grid=(b, DIFF_HEADS, s // bq),
        in_specs=[blk(bq, lambda bi, h, i: (bi, i, h)), blk(s, lambda bi, h, i: (bi, 0, h)),
                  blk(s, lambda bi, h, i: (bi, 0, h)),
                  pl.BlockSpec(lam.shape, lambda bi, h, i: (0, 0)),
                  pl.BlockSpec(subln.shape, lambda bi, h, i: (0, 0))],
        out_specs=blk(bq, lambda bi, h, i: (bi, i, h)),
        scratch_shapes=_attn_scratch(bq),
        compiler_params=_cparams(("parallel", "parallel", "arbitrary")),
        name="diff_attn",
    )(dq, dk, dv, lam, subln)


def _mix_out_kernel(h_ref, a_ref, b_ref, w_ref, g_ref, rwh_ref, rwl_ref, rb_ref,
                    hout_ref, hf_ref, route_ref):
    mixed = jnp.concatenate([a_ref[...], b_ref[...]], axis=-1)
    h = h_ref[...] + _dot(mixed, w_ref[...])
    hout_ref[...] = h
    hf = _rms(h, g_ref[...])
    hi = hf.astype(BF16)
    hf_ref[...] = hi
    lo = (hf - hi.astype(F32)).astype(BF16)
    z = _dot(hi, rwh_ref[...]) + _dot(lo, rwh_ref[...]) + _dot(hi, rwl_ref[...]) + rb_ref[...]
    lane_i = lax.broadcasted_iota(jnp.int32, z.shape, 1)
    lane = lane_i.astype(F32)
    big = float(LANES)
    is_g = lane_i < MOE_GROUPS
    zg = jnp.where(is_g, z, NEG)
    mg = jnp.max(zg, axis=-1, keepdims=True)
    g_idx = jnp.min(jnp.where(zg == mg, lane, big), axis=-1, keepdims=True)
    pg_top = 1.0 / jnp.sum(jnp.where(is_g, jnp.exp(zg - mg), 0.0), axis=-1, keepdims=True)
    grp_of_lane = jnp.right_shift(lane_i - MOE_GROUPS, int(math.log2(MOE_EPG))).astype(F32)
    in_grp = (lane_i >= MOE_GROUPS) & (lane_i < MOE_GROUPS + MOE_EXPERTS) & (grp_of_lane == g_idx)
    ze = jnp.where(in_grp, z, NEG)
    m1 = jnp.max(ze, axis=-1, keepdims=True)
    i1 = jnp.min(jnp.where(ze == m1, lane, big), axis=-1, keepdims=True)
    ze2 = jnp.where(lane == i1, NEG, ze)
    m2 = jnp.max(ze2, axis=-1, keepdims=True)
    i2 = jnp.min(jnp.where(ze2 == m2, lane, big), axis=-1, keepdims=True)
    r = jnp.exp(m2 - m1)
    w1 = 1.0 / (1.0 + r)
    w2 = r / (1.0 + r)
    e1 = i1 - float(MOE_GROUPS)
    e2 = i2 - float(MOE_GROUPS)
    out = jnp.where(lane_i == 0, e1, jnp.where(lane_i == 1, e2, jnp.where(lane_i == 2, pg_top * w1,
                    jnp.where(lane_i == 3, pg_top * w2, 0.0))))
    route_ref[...] = out


def _mix_out(h, a, b, w, g, rwh, rwl, rb):
    n, d = h.shape
    tm = ROW_TILE
    row = lambda i: (i, 0)
    full = lambda i: (0, 0)
    wspec = lambda t: pl.BlockSpec(t.shape, full)
    return pl.pallas_call(
        _mix_out_kernel,
        out_shape=[jax.ShapeDtypeStruct((n, d), F32), jax.ShapeDtypeStruct((n, d), BF16),
                   jax.ShapeDtypeStruct((n, LANES), F32)],
        grid=(n // tm,),
        in_specs=[pl.BlockSpec((tm, d), row), pl.BlockSpec((tm, a.shape[1]), row), pl.BlockSpec((tm, b.shape[1]), row),
                  wspec(w), wspec(g), wspec(rwh), wspec(rwl), wspec(rb)],
        out_specs=[pl.BlockSpec((tm, d), row), pl.BlockSpec((tm, d), row), pl.BlockSpec((tm, LANES), row)],
        compiler_params=_cparams(("parallel",)),
        name="mix_out",
    )(h, a, b, w, g, rwh, rwl, rb)


def _expert_kernel(te_ref, nv_ref, x_ref, wg_ref, wu_ref, wd_ref, y_ref):
    t = pl.program_id(0)

    @pl.when(t < nv_ref[0])
    def _():
        x = x_ref[...]
        a = _dot(x, wg_ref[0])
        up = _dot(x, wu_ref[0])
        act = (a * _sigmoid(a) * up).astype(BF16)
        y_ref[...] = _dot(act, wd_ref[0]).astype(BF16)

    @pl.when(t >= nv_ref[0])
    def _():
        y_ref[...] = jnp.zeros(y_ref.shape, BF16)


def _experts(xs, tile_expert, n_valid, wg, wu, wd):
    p, d = xs.shape
    tm = EXPERT_TILE
    ff = wg.shape[2]
    grid_spec = pltpu.PrefetchScalarGridSpec(
        num_scalar_prefetch=2,
        grid=(p // tm,),
        in_specs=[pl.BlockSpec((tm, d), lambda t, te, nv: (t, 0)),
                  pl.BlockSpec((1, d, ff), lambda t, te, nv: (te[t], 0, 0)),
                  pl.BlockSpec((1, d, ff), lambda t, te, nv: (te[t], 0, 0)),
                  pl.BlockSpec((1, ff, d), lambda t, te, nv: (te[t], 0, 0))],
        out_specs=pl.BlockSpec((tm, d), lambda t, te, nv: (t, 0)),
    )
    return pl.pallas_call(
        _expert_kernel,
        out_shape=jax.ShapeDtypeStruct((p, d), BF16),
        grid_spec=grid_spec,
        compiler_params=_cparams(("arbitrary",)),
        name="experts",
    )(tile_expert, n_valid, xs, wg, wu, wd)


def _combine_kernel(h_ref, y_ref, route_ref, g_ref, o_ref, *, final):
    d = h_ref.shape[1]
    route = route_ref[...]
    y = y_ref[...]
    out = h_ref[...] + route[:, 2:3] * y[:, :d].astype(F32) + route[:, 3:4] * y[:, d:].astype(F32)
    if final:
        out = _rms(out, g_ref[...])
    o_ref[...] = out


def _combine(h, ypair, route, g, final):
    n, d = h.shape
    tm = ROW_TILE
    row = lambda i: (i, 0)
    return pl.pallas_call(
        functools.partial(_combine_kernel, final=final),
        out_shape=jax.ShapeDtypeStruct((n, d), F32),
        grid=(n // tm,),
        in_specs=[pl.BlockSpec((tm, d), row), pl.BlockSpec((tm, 2 * d), row), pl.BlockSpec((tm, LANES), row),
                  pl.BlockSpec(g.shape, lambda i: (0, 0))],
        out_specs=pl.BlockSpec((tm, d), row),
        compiler_params=_cparams(("parallel",)),
        name="moe_combine",
    )(h, ypair, route, g)


def _moe(h, hf, route, wg, wu, wd, g_final, final):
    n, d = h.shape
    tm = EXPERT_TILE
    e_flat = route[:, 0:2].astype(jnp.int32).reshape(-1)
    onehot = (e_flat[:, None] == jnp.arange(MOE_EXPERTS, dtype=jnp.int32)[None, :]).astype(jnp.int32)
    counts = jnp.sum(onehot, axis=0)
    padded = ((counts + tm - 1) // tm) * tm
    ends = jnp.cumsum(padded)
    starts = ends - padded
    order = jnp.argsort(e_flat, stable=True)
    sorted_e = e_flat[order]
    first_of_e = jnp.cumsum(counts) - counts
    rank = jnp.arange(2 * n, dtype=jnp.int32) - first_of_e[sorted_e]
    dest_sorted = starts[sorted_e] + rank
    n_rows = 2 * n + MOE_EXPERTS * tm
    tok_for_row = jnp.zeros((n_rows,), jnp.int32).at[dest_sorted].set((order // 2).astype(jnp.int32))
    dest = jnp.zeros((2 * n,), jnp.int32).at[order].set(dest_sorted.astype(jnp.int32))
    tile_start = jnp.arange(n_rows // tm, dtype=jnp.int32) * tm
    tile_expert = jnp.minimum(jnp.searchsorted(ends, tile_start, side="right"), MOE_EXPERTS - 1).astype(jnp.int32)
    n_valid = (ends[-1] // tm).astype(jnp.int32).reshape(1)
    xs = jnp.take(hf, tok_for_row, axis=0)
    ys = _experts(xs, tile_expert, n_valid, wg, wu, wd)
    ypair = jnp.take(ys, dest, axis=0).reshape(n, 2 * d)
    return _combine(h, ypair, route, g_final, final)


def _od_in_kernel(x_ref, g_ref, w_ref, ur_ref, uh_ref):
    hn = _rms(x_ref[...], g_ref[...]).astype(BF16)
    u = _dot(hn, w_ref[...])
    ur_ref[...] = u[:, :RWKV_IN]
    uh_ref[...] = u[:, RWKV_IN:]


def _od_in(x2, g, w):
    n, d = x2.shape
    tm = ROW_TILE
    row = lambda i: (i, 0)
    return pl.pallas_call(
        _od_in_kernel,
        out_shape=[jax.ShapeDtypeStruct((n, RWKV_IN), F32), jax.ShapeDtypeStruct((n, HGRN_IN), F32)],
        grid=(n // tm,),
        in_specs=[pl.BlockSpec((tm, d), row), pl.BlockSpec(g.shape, lambda i: (0, 0)),
                  pl.BlockSpec(w.shape, lambda i: (0, 0))],
        out_specs=[pl.BlockSpec((tm, RWKV_IN), row), pl.BlockSpec((tm, HGRN_IN), row)],
        compiler_params=_cparams(("parallel",)),
        name="od_in",
    )(x2, g, w)


def _rwkv_kernel(u_ref, mu_ref, w0_ref, w2_ref, a0_ref, a2_ref, g2_ref, kk_ref, ka_ref, rk_ref,
                 lnw_ref, lnb_ref, tri_ref, seg_ref, o_ref, prev_ref, ht_ref):
    c = pl.program_id(1)

    @pl.when(c == 0)
    def _():
        prev_ref[...] = jnp.zeros(prev_ref.shape, F32)
        ht_ref[...] = jnp.zeros(ht_ref.shape, F32)

    u = u_ref[0]
    rows = lax.broadcasted_iota(jnp.int32, u.shape, 0)
    u_prev = jnp.where(rows == 0, prev_ref[...], pltpu.roll(u, 1, 0))
    prev_ref[...] = u[CHUNK - 1:CHUNK, :]
    us = u + (u_prev - u) * mu_ref[...]
    w = RWKV_W
    r = us[:, 0:w]
    k = us[:, w:2 * w]
    v = us[:, 2 * w:3 * w]
    x12 = us[:, 3 * w:3 * w + LANES]
    xg = us[:, 3 * w + LANES:]
    seg = seg_ref[...]
    tri = tri_ref[...]

    wl = w0_ref[...] + _dot(jnp.tanh(x12).astype(BF16), w2_ref[...])
    nwl = -wl
    softplus = jnp.maximum(nwl, 0.0) + jnp.log1p(jnp.exp(-jnp.abs(nwl)))
    lw = -jnp.exp(-softplus - 0.5)
    a = _sigmoid(a0_ref[...] + _dot(x12.astype(BF16), a2_ref[...]))
    g = _dot(_sigmoid(xg).astype(BF16), g2_ref[...])
    kk = k * kk_ref[...]
    kk = kk * lax.rsqrt(jnp.maximum(_split_dot(kk * kk, seg), 1e-24))
    k2 = k * (1.0 + (a - 1.0) * ka_ref[...])
    a_in = -kk
    b_in = kk * a

    lg = _split_dot(lw, tri, left=True)
    lg_c = lg[CHUNK - 1:CHUNK, :]
    e_neg = jnp.exp(-lg)
    e_rel = jnp.exp(lg_c - lg)
    g_c = jnp.exp(lg_c)
    at = a_in * jnp.exp(lg - lw)
    rt = r * jnp.exp(lg)
    kt = k2 * e_neg
    bt = b_in * e_neg
    kh = k2 * e_rel
    bh = b_in * e_rel

    c2 = 2 * CHUNK
    ri = lax.broadcasted_iota(jnp.int32, (c2, c2), 0)
    ci = lax.broadcasted_iota(jnp.int32, (c2, c2), 1)
    sh = int(math.log2(CHUNK))
    same = jnp.right_shift(ri, sh) == jnp.right_shift(ci, sh)
    strict = same & (ri > ci)
    incl = same & (ri >= ci)
    eye = jnp.where(ri == ci, 1.0, 0.0).astype(F32)
    lane = lax.broadcasted_iota(jnp.int32, (CHUNK, LANES), 1)
    lo_half = lane < RWKV_HD

    def bd(x):
        return jnp.concatenate([jnp.where(lo_half, x, 0.0), jnp.where(lo_half, 0.0, x)], axis=0)

    ys = []
    for p in range(RWKV_HEADS // 2):
        sl = slice(p * LANES, (p + 1) * LANES)
        at_b, rt_b, kt_b, bt_b = bd(at[:, sl]), bd(rt[:, sl]), bd(kt[:, sl]), bd(bt[:, sl])
        kh_b, bh_b, v_b = bd(kh[:, sl]), bd(bh[:, sl]), bd(v[:, sl])
        sc = _dot_nt(jnp.concatenate([at_b, rt_b], axis=0).astype(BF16),
                     jnp.concatenate([kt_b, bt_b], axis=0).astype(BF16))
        a_ak = jnp.where(strict, sc[:c2, :c2], 0.0).astype(BF16)
        a_ab = jnp.where(strict, sc[:c2, c2:], 0.0)
        a_rk = jnp.where(incl, sc[c2:, :c2], 0.0).astype(BF16)
        a_rb = jnp.where(incl, sc[c2:, c2:], 0.0).astype(BF16)
        pw = a_ab
        t_inv = eye + a_ab
        for _ in range(int(math.log2(CHUNK)) - 1):
            pwb = pw.astype(BF16)
            pw = _dot(pwb, pwb)
            t_inv = t_inv + _dot(t_inv.astype(BF16), pw.astype(BF16))
        v_bb = v_b.astype(BF16)
        x2 = _dot(a_ak, v_bb)
        tx = _dot(t_inv.astype(BF16), jnp.concatenate([at_b, x2], axis=1).astype(BF16))
        txb = tx.astype(BF16)
        qe = _dot(a_rb, txb)
        q_b = rt_b + qe[:, :LANES]
        e_b = _dot(a_rk, v_bb) + qe[:, LANES:]
        bhb = bh_b.astype(BF16)
        gd = _dot_tn(txb, bhb)
        g2t = gd[:LANES]
        dmt = _dot_tn(v_bb, kh_b.astype(BF16)) + gd[LANES:]
        ht = ht_ref[p]
        htb = ht.astype(BF16)
        y_b = _dot_nt(q_b.astype(BF16), htb) + e_b
        ht_ref[p] = jnp.where(same, ht * g_c[:, sl] + _dot(htb, g2t.astype(BF16)) + dmt, 0.0)
        ys.append(y_b[:CHUNK] + y_b[CHUNK:])
    y = jnp.concatenate(ys, axis=1)

    inv_n = 1.0 / RWKV_HD
    mean = _split_dot(y, seg) * inv_n
    dlt = y - mean
    var = _split_dot(dlt * dlt, seg) * inv_n
    yn = dlt * lax.rsqrt(var + RWKV_GN_EPS) * lnw_ref[...] + lnb_ref[...]
    bonus = _split_dot(r * k2 * rk_ref[...], seg) * v
    o_ref[0] = ((yn + bonus) * g).astype(BF16)


def _rwkv(u, mu, w0, w2p, a0, a2p, g2, k_k, k_a, r_k, ln_w, ln_b, tri, seg):
    b, s, _ = u.shape
    full = lambda bi, ci: (0, 0)
    wspec = lambda t: pl.BlockSpec(t.shape, full)
    params = [mu, w0, w2p, a0, a2p, g2, k_k, k_a, r_k, ln_w, ln_b, tri, seg]
    return pl.pallas_call(
        _rwkv_kernel,
        out_shape=jax.ShapeDtypeStruct((b, s, RWKV_W), BF16),
        grid=(b, s // CHUNK),
        in_specs=[pl.BlockSpec((1, CHUNK, RWKV_IN), lambda bi, ci: (bi, ci, 0))] + [wspec(t) for t in params],
        out_specs=pl.BlockSpec((1, CHUNK, RWKV_W), lambda bi, ci: (bi, ci, 0)),
        scratch_shapes=[pltpu.VMEM((1, RWKV_IN), F32), pltpu.VMEM((RWKV_HEADS // 2, LANES, LANES), F32)],
        compiler_params=_cparams(("parallel", "arbitrary")),
        name="rwkv7",
    )(u, *params)


def _hgrn_kernel(u_ref, lb_ref, on_ref, tri_ref, ones_ref, o_ref, st_ref):
    c = pl.program_id(1)

    @pl.when(c == 0)
    def _():
        st_ref[...] = jnp.zeros(st_ref.shape, F32)

    u = u_ref[0]
    lbp = lb_ref[...]
    mx = jnp.maximum(lbp[0:1], lbp[1:2])
    e0 = jnp.exp(lbp[0:1] - mx)
    e1 = jnp.exp(lbp[1:2] - mx)
    p0 = e0 / (e0 + e1)
    p1 = e1 / (e0 + e1)
    lb = (p0 + p1) - p0
    tri = tri_ref[...]
    ones = ones_ref[...]
    d = HGRN_D
    srow = lax.broadcasted_iota(jnp.int32, (SUB, d), 0)
    for h in range(HGRN_HEADS):
        sl = slice(h * d, (h + 1) * d)
        q = u[:, sl]
        z = u[:, HGRN_W + h * d:HGRN_W + (h + 1) * d]
        iv = u[:, 2 * HGRN_W + h * d:2 * HGRN_W + (h + 1) * d]
        gt = u[:, 3 * HGRN_W + h * d:3 * HGRN_W + (h + 1) * d]
        lbh = lb[:, sl]
        qs = q * _sigmoid(q)
        log_sig = jnp.minimum(z, 0.0) - jnp.log1p(jnp.exp(-jnp.abs(z)))
        x1 = jnp.log(lbh)
        x2 = jnp.log1p(-lbh) + log_sig
        log_f = jnp.maximum(x1, x2) + jnp.log1p(jnp.exp(-jnp.abs(x1 - x2)))
        key = (1.0 - lbh) * _sigmoid(-z)
        bc = _split_dot(log_f, tri, left=True)
        st = st_ref[h]
        o = _dot_nt((qs * jnp.exp(bc)).astype(BF16), st.astype(BF16))
        ivb = iv.astype(BF16)
        blocks = []
        for blk in range(CHUNK // SUB):
            r0 = blk * SUB
            b_i = bc[r0:r0 + SUB]
            q_i = qs[r0:r0 + SUB]
            k_i = key[r0:r0 + SUB]
            v_i = iv[r0:r0 + SUB]
            p_rows = []
            for t in range(SUB):
                wgt = jnp.where(srow <= t, jnp.exp(b_i[t:t + 1] - b_i), 0.0)
                p_rows.append(q_i[t:t + 1] * k_i * wgt)
            rs = _dot(jnp.concatenate(p_rows, axis=0).astype(BF16), ones)
            o_rows = [jnp.sum(rs[t * SUB:(t + 1) * SUB] * v_i, axis=0, keepdims=True) for t in range(SUB)]
            o_i = jnp.concatenate(o_rows, axis=0)
            if blk > 0:
                b_m = bc[r0 - 1:r0]
                qp = (q_i * jnp.exp(b_i - b_m)).astype(BF16)
                kp = (key[:r0] * jnp.exp(b_m - bc[:r0])).astype(BF16)
                o_i = o_i + _dot(_dot_nt(qp, kp).astype(BF16), ivb[:r0])
            blocks.append(o_i)
        o = o + jnp.concatenate(blocks, axis=0)
        b_l = bc[CHUNK - 1:CHUNK]
        st_ref[h] = st * jnp.exp(b_l) + _dot_tn(ivb, (key * jnp.exp(b_l - bc)).astype(BF16))
        on = _rms(o, on_ref[:, sl])
        o_ref[0, :, sl] = (on * (gt * _sigmoid(gt))).astype(BF16)


def _hgrn(u, lbp, o_norm, tri, ones):
    b, s, _ = u.shape
    full = lambda bi, ci: (0, 0)
    wspec = lambda t: pl.BlockSpec(t.shape, full)
    return pl.pallas_call(
        _hgrn_kernel,
        out_shape=jax.ShapeDtypeStruct((b, s, HGRN_W), BF16),
        grid=(b, s // CHUNK),
        in_specs=[pl.BlockSpec((1, CHUNK, HGRN_IN), lambda bi, ci: (bi, ci, 0)), wspec(lbp), wspec(o_norm),
                  wspec(tri), wspec(ones)],
        out_specs=pl.BlockSpec((1, CHUNK, HGRN_W), lambda bi, ci: (bi, ci, 0)),
        scratch_shapes=[pltpu.VMEM((HGRN_HEADS, HGRN_D, HGRN_D), F32)],
        compiler_params=_cparams(("parallel", "arbitrary")),
        name="hgrn2",
    )(u, lbp, o_norm, tri, ones)


def _rope_tables(seq):
    pos = jnp.arange(seq, dtype=F32)[:, None]

    def cs(half):
        inv = jnp.power(ROPE_THETA, -jnp.arange(half, dtype=F32) / half)
        ang = pos * inv[None, :]
        return jnp.cos(ang), jnp.sin(ang)

    cd, sd = cs(DIFF_HD // 2)
    cos_d = jnp.tile(jnp.concatenate([cd, cd], axis=1), (1, LANES // DIFF_HD))
    sin_d = jnp.tile(jnp.concatenate([-sd, sd], axis=1), (1, LANES // DIFF_HD))
    cm, sm = cs(MLA_ROPE // 2)
    one = jnp.ones((seq, MLA_NOPE), F32)
    tail = LANES - MLA_QK
    cos_m = jnp.concatenate([one, cm, cm, jnp.ones((seq, tail), F32)], axis=1)
    sin_m = jnp.concatenate([0 * one, -sm, sm, jnp.zeros((seq, tail), F32)], axis=1)
    return cos_d, sin_d, cos_m, sin_m


def _router_weights(w_group, b_group, w_expert, b_expert):
    d = w_group.shape[0]
    pad = LANES - MOE_GROUPS - MOE_EXPERTS
    w = jnp.concatenate([w_group, w_expert, jnp.zeros((d, pad), F32)], axis=1)
    bias = jnp.concatenate([b_group, b_expert, jnp.zeros((pad,), F32)])[None, :]
    hi = w.astype(BF16)
    lo = (w - hi.astype(F32)).astype(BF16)
    return hi, lo, bias


def kernel(x, norm_mix, norm_ffn, norm_final, ev_w_in, ev_w_out, mla_q_norm, mla_w_q_up, mla_kv_norm, mla_w_kv_up, diff_lambda, diff_subln, od_w_in, od_w_out, rwkv_mu, rwkv_w0, rwkv_w2, rwkv_a0, rwkv_a2, rwkv_g2, rwkv_k_k, rwkv_k_a, rwkv_r_k, rwkv_ln_w, rwkv_ln_b, hgrn_lb, hgrn_o_norm, moe_w_group, moe_b_group, moe_w_expert, moe_b_expert, moe_w_gate, moe_w_up, moe_w_down):
    b, s, d = x.shape
    n = b * s
    assert norm_mix.shape[0] == 2 and hgrn_lb.shape[0] == 2
    assert s % ATTN_BLOCK == 0 and s % ROW_TILE == 0 and ATTN_BLOCK % CHUNK == 0
    row2 = lambda t: t.reshape(1, -1)
    h = x.reshape(n, d)

    w_in = ev_w_in[0]
    o1, o2, o3 = MLA_LORA, 2 * MLA_LORA, 2 * MLA_LORA + MLA_ROPE
    kr_pad = jnp.zeros((d, LANES), F32).at[:, MLA_NOPE:MLA_QK].set(w_in[:, o2:o3])
    w0 = jnp.concatenate([w_in[:, :o2], kr_pad, w_in[:, o3:]], axis=1).astype(BF16)
    wq = mla_w_q_up[0].reshape(MLA_LORA, MLA_HEADS, MLA_QK)
    wq = jnp.pad(wq, ((0, 0), (0, 0), (0, LANES - MLA_QK))).reshape(MLA_LORA, MLA_HEADS * LANES).astype(BF16)
    wkv = mla_w_kv_up[0].reshape(MLA_LORA, MLA_HEADS, MLA_NOPE + MLA_V)
    wk = jnp.pad(wkv[:, :, :MLA_NOPE], ((0, 0), (0, 0), (0, LANES - MLA_NOPE)))
    wk = wk.reshape(MLA_LORA, MLA_HEADS * LANES).astype(BF16)
    wv = wkv[:, :, MLA_NOPE:].reshape(MLA_LORA, MLA_HEADS * MLA_V).astype(BF16)
    cos_d, sin_d, cos_m, sin_m = _rope_tables(s)
    qf, kf, vm, dq, dk, dv = _ev_in(h, row2(norm_mix[0]), w0, row2(mla_q_norm[0]), row2(mla_kv_norm[0]),
                                    wq, wk, wv, cos_d, sin_d, cos_m, sin_m, s)
    r3 = lambda t: t.reshape(b, s, t.shape[-1])
    o_mla = _mla_attn(r3(qf), r3(kf), r3(vm))
    lam_init = 0.8 - 0.6 * math.exp(-0.3 * 0)
    o_diff = _diff_attn(r3(dq), r3(dk), r3(dv), diff_lambda[0], row2(diff_subln[0]), lam_init)
    rwh, rwl, rb = _router_weights(moe_w_group[0], moe_b_group[0], moe_w_expert[0], moe_b_expert[0])
    h, hf, route = _mix_out(h, o_mla.reshape(n, -1), o_diff.reshape(n, -1), ev_w_out[0].astype(BF16),
                            row2(norm_ffn[0]), rwh, rwl, rb)
    h = _moe(h, hf, route, moe_w_gate[0].astype(BF16), moe_w_up[0].astype(BF16), moe_w_down[0].astype(BF16),
             row2(norm_final), False)

    ur, uh = _od_in(h, row2(norm_mix[1]), od_w_in[0].astype(BF16))
    zpad = jnp.zeros((RWKV_DECAY_LORA, RWKV_W), F32)
    w2p = jnp.concatenate([rwkv_w2[0], zpad], axis=0).astype(BF16)
    a2p = jnp.concatenate([zpad, rwkv_a2[0]], axis=0).astype(BF16)
    ci = jnp.arange(CHUNK)
    tri = (ci[None, :] <= ci[:, None]).astype(BF16)
    li = jnp.arange(RWKV_W) // RWKV_HD
    seg = (li[:, None] == li[None, :]).astype(BF16)
    o_c = _rwkv(ur.reshape(b, s, RWKV_IN), row2(rwkv_mu[0]), row2(rwkv_w0[0]), w2p, row2(rwkv_a0[0]), a2p,
                rwkv_g2[0].astype(BF16), row2(rwkv_k_k[0]), row2(rwkv_k_a[0]), row2(rwkv_r_k[0]),
                row2(rwkv_ln_w[0]), row2(rwkv_ln_b[0]), tri, seg)
    o_d = _hgrn(uh.reshape(b, s, HGRN_IN), hgrn_lb, row2(hgrn_o_norm[0]), tri, jnp.ones((LANES, LANES), BF16))
    rwh, rwl, rb = _router_weights(moe_w_group[1], moe_b_group[1], moe_w_expert[1], moe_b_expert[1])
    h, hf, route = _mix_out(h, o_c.reshape(n, -1), o_d.reshape(n, -1), od_w_out[0].astype(BF16),
                            row2(norm_ffn[1]), rwh, rwl, rb)
    out = _moe(h, hf, route, moe_w_gate[1].astype(BF16), moe_w_up[1].astype(BF16), moe_w_down[1].astype(BF16),
               row2(norm_final), True)
    return out.reshape(b, s, d)
```

```python
import functools
import math

import jax
import jax.numpy as jnp
from jax import lax
from jax.experimental import pallas as pl
from jax.experimental.pallas import tpu as pltpu

F32 = jnp.float32
BF16 = jnp.bfloat16

CHUNK = 64
ROPE_THETA = 10000.0
NORM_EPS = 1e-6
MLA_HEADS = 8
MLA_LORA = 256
MLA_NOPE = 64
MLA_ROPE = 32
MLA_V = 64
MLA_QK = MLA_NOPE + MLA_ROPE
DIFF_HEADS = 4
DIFF_HD = 64
DIFF_V = 2 * DIFF_HD
DIFF_W = DIFF_HEADS * 2 * DIFF_HD
RWKV_HEADS = 8
RWKV_HD = 64
RWKV_W = RWKV_HEADS * RWKV_HD
RWKV_DECAY_LORA = 64
RWKV_A_LORA = 64
RWKV_GATE_LORA = 128
RWKV_IN = 3 * RWKV_W + RWKV_DECAY_LORA + RWKV_A_LORA + RWKV_GATE_LORA
RWKV_GN_EPS = 64e-5
HGRN_HEADS = 4
HGRN_D = 128
HGRN_W = HGRN_HEADS * HGRN_D
HGRN_IN = 4 * HGRN_W
MOE_GROUPS = 4
MOE_EPG = 8
MOE_EXPERTS = MOE_GROUPS * MOE_EPG
MOE_FF = 512

LANES = 128
ROW_TILE = 256
ATTN_BLOCK = 512
EXPERT_TILE = 256
SUB = 16
VMEM_LIMIT = 48 * 1024 * 1024
NEG = -1e30
LOG2E = math.log2(math.e)


def _cparams(sem):
    return pltpu.CompilerParams(dimension_semantics=sem, vmem_limit_bytes=VMEM_LIMIT)


def _dot(a, b):
    return jnp.dot(a, b, preferred_element_type=F32)


def _dot_nt(a, b):
    return lax.dot_general(a, b, (((1,), (1,)), ((), ())), preferred_element_type=F32)


def _dot_tn(a, b):
    return lax.dot_general(a, b, (((0,), (0,)), ((), ())), preferred_element_type=F32)


def _rms(x, g):
    return x * lax.rsqrt(jnp.mean(x * x, axis=-1, keepdims=True) + NORM_EPS) * g


def _sigmoid(x):
    return 1.0 / (1.0 + jnp.exp(-x))


def _split_dot(x, w_bf16, left=False):
    hi = x.astype(BF16)
    lo = (x - hi.astype(F32)).astype(BF16)
    if left:
        return _dot(w_bf16, hi) + _dot(w_bf16, lo)
    return _dot(hi, w_bf16) + _dot(lo, w_bf16)


def _tile_lanes(t, n):
    return jnp.concatenate([t] * n, axis=-1) if n > 1 else t


def _rope_rot(x, half):
    w = x.shape[-1]
    lane = lax.broadcasted_iota(jnp.int32, x.shape, x.ndim - 1)
    first = (lane & (2 * half - 1)) < half
    return jnp.where(first, pltpu.roll(x, w - half, x.ndim - 1), pltpu.roll(x, half, x.ndim - 1))


def _ev_in_kernel(x_ref, g_ref, w0_ref, qg_ref, kvg_ref, wq_ref, wk_ref, wv_ref,
                  cd_ref, sd_ref, cm_ref, sm_ref,
                  qf_ref, kf_ref, vm_ref, dq_ref, dk_ref, dv_ref):
    hn = _rms(x_ref[...], g_ref[...]).astype(BF16)
    u = _dot(hn, w0_ref[...])
    c_q = u[:, 0:256]
    c_kv = u[:, 256:512]
    kr = u[:, 512:640]
    dq = u[:, 640:1152]
    dk = u[:, 1152:1664]
    dv = u[:, 1664:2176]
    cqn = _rms(c_q, qg_ref[...]).astype(BF16)
    ckn = _rms(c_kv, kvg_ref[...]).astype(BF16)
    q = _dot(cqn, wq_ref[...])
    k = _dot(ckn, wk_ref[...])
    v = _dot(ckn, wv_ref[...])
    cm = cm_ref[...]
    sm = sm_ref[...]
    m_half = MLA_ROPE // 2
    lane = lax.broadcasted_iota(jnp.int32, q.shape, 1) & (LANES - 1)
    first = (lane >= MLA_NOPE) & (lane < MLA_NOPE + m_half)
    rot_q = jnp.where(first, pltpu.roll(q, q.shape[1] - m_half, 1), pltpu.roll(q, m_half, 1))
    q = (q * _tile_lanes(cm, MLA_HEADS) + rot_q * _tile_lanes(sm, MLA_HEADS)) * (MLA_QK ** -0.5 * LOG2E)
    lane1 = lax.broadcasted_iota(jnp.int32, kr.shape, 1)
    first1 = (lane1 >= MLA_NOPE) & (lane1 < MLA_NOPE + m_half)
    rot_k = jnp.where(first1, pltpu.roll(kr, LANES - m_half, 1), pltpu.roll(kr, m_half, 1))
    kr = kr * cm + rot_k * sm
    k = k + _tile_lanes(kr, MLA_HEADS)
    qf_ref[...] = q.astype(BF16)
    kf_ref[...] = k.astype(BF16)
    vm_ref[0, :, 0] = v.T.reshape(vm_ref.shape[1], LANES, v.shape[0]).astype(BF16)
    cd = _tile_lanes(cd_ref[...], DIFF_W // LANES)
    sd = _tile_lanes(sd_ref[...], DIFF_W // LANES)
    dq = (dq * cd + _rope_rot(dq, DIFF_HD // 2) * sd) * (DIFF_HD ** -0.5 * LOG2E)
    dk = dk * cd + _rope_rot(dk, DIFF_HD // 2) * sd
    dq_ref[...] = dq.astype(BF16)
    dk_ref[...] = dk.astype(BF16)
    dv_ref[0, :, 0] = dv.T.reshape(dv_ref.shape[1], LANES, dv.shape[0]).astype(BF16)


def _ev_in(x2, g, w0, qg, kvg, wq, wk, wv, cd, sd, cm, sm, seq):
    n, d = x2.shape
    tm = ATTN_BLOCK
    nt = seq // tm
    row = lambda i: (i, 0)
    full = lambda i: (0, 0)
    tab = lambda i: (i % nt, 0)
    wspec = lambda a: pl.BlockSpec(a.shape, full)
    vt_shape = lambda w: jax.ShapeDtypeStruct((n // seq, w // LANES, nt, LANES, tm), BF16)
    vt_spec = lambda w: pl.BlockSpec((1, w // LANES, 1, LANES, tm), lambda i: (i // nt, 0, i % nt, 0, 0))
    rows = lambda w: jax.ShapeDtypeStruct((n, w), BF16)
    rspec = lambda w: pl.BlockSpec((tm, w), row)
    wide = MLA_HEADS * LANES
    return pl.pallas_call(
        _ev_in_kernel,
        out_shape=[rows(wide), rows(wide), vt_shape(MLA_HEADS * MLA_V), rows(DIFF_W), rows(DIFF_W), vt_shape(DIFF_W)],
        grid=(n // tm,),
        in_specs=[pl.BlockSpec((tm, d), row), wspec(g), wspec(w0), wspec(qg), wspec(kvg), wspec(wq), wspec(wk),
                  wspec(wv)] + [pl.BlockSpec((tm, LANES), tab)] * 4,
        out_specs=[rspec(wide), rspec(wide), vt_spec(MLA_HEADS * MLA_V), rspec(DIFF_W), rspec(DIFF_W),
                   vt_spec(DIFF_W)],
        compiler_params=_cparams(("parallel",)),
        name="ev_in",
    )(x2, g, w0, qg, kvg, wq, wk, wv, cd, sd, cm, sm)


ONES_ROWS = 16


def _online_update(st, vt, m_ref, acc_ref, idx):
    m_prev = m_ref[idx]
    m_new = jnp.maximum(m_prev, jnp.max(st, axis=0, keepdims=True))
    alpha = jnp.exp2(m_prev - m_new)
    p = jnp.exp2(st - m_new).astype(BF16)
    acc_ref[idx] = alpha * acc_ref[idx] + _dot(vt, p)
    m_ref[idx] = m_new


def _chunk_mask_t(shape):
    sh = int(math.log2(CHUNK))
    key_chunk = jnp.right_shift(lax.broadcasted_iota(jnp.int32, shape, 0), sh)
    q_chunk = jnp.right_shift(lax.broadcasted_iota(jnp.int32, shape, 1), sh)
    return key_chunk <= q_chunk


def _attn_loop(i, q_a, q_b, k_ref, vt_ref, m_ref, acc_ref, split):
    bk = ATTN_BLOCK
    m_ref[...] = jnp.full(m_ref.shape, NEG, F32)
    acc_ref[...] = jnp.zeros(acc_ref.shape, F32)
    dv = acc_ref.shape[1] - ONES_ROWS
    ones = jnp.ones((ONES_ROWS, bk), BF16)

    def scores(j):
        kb = k_ref[0, pl.ds(pl.multiple_of(j * bk, bk), bk), :]
        vt = vt_ref[0, 0, j]
        if split:
            va = jnp.concatenate([vt[:dv], ones], axis=0)
            vb = jnp.concatenate([vt[dv:], ones], axis=0)
            return _dot_nt(kb[:, :LANES], q_a), _dot_nt(kb[:, LANES:], q_b), va, vb
        va = jnp.concatenate([vt, ones], axis=0)
        return _dot_nt(kb, q_a), _dot_nt(kb, q_b), va, va

    def body(j, carry):
        sa, sb, va, vb = scores(j)
        _online_update(sa, va, m_ref, acc_ref, 0)
        _online_update(sb, vb, m_ref, acc_ref, 1)
        return carry

    lax.fori_loop(0, i, body, 0)
    sa, sb, va, vb = scores(i)
    mask = _chunk_mask_t(sa.shape)
    _online_update(jnp.where(mask, sa, NEG), va, m_ref, acc_ref, 0)
    _online_update(jnp.where(mask, sb, NEG), vb, m_ref, acc_ref, 1)
    acc_a = acc_ref[0]
    acc_b = acc_ref[1]
    return acc_a[:dv] / acc_a[dv:dv + 1], acc_b[:dv] / acc_b[dv:dv + 1]


def _mla_attn_kernel(q_ref, k_ref, vt_ref, o_ref, m_ref, acc_ref):
    q = q_ref[0]
    o_a, o_b = _attn_loop(pl.program_id(2), q[:, :LANES], q[:, LANES:], k_ref, vt_ref, m_ref, acc_ref, True)
    o_ref[0] = jnp.concatenate([o_a, o_b], axis=0).T.astype(BF16)


def _diff_attn_kernel(q_ref, k_ref, vt_ref, lam_ref, sub_ref, o_ref, m_ref, acc_ref, *, lam_init):
    q = q_ref[0]
    lane = lax.broadcasted_iota(jnp.int32, q.shape, 1)
    zero = jnp.zeros_like(q)
    a1, a2 = _attn_loop(pl.program_id(2), jnp.where(lane < DIFF_HD, q, zero), jnp.where(lane >= DIFF_HD, q, zero),
                        k_ref, vt_ref, m_ref, acc_ref, False)
    lam = lam_ref[...]
    s1 = jnp.sum(lam[0:1] * lam[1:2], axis=-1, keepdims=True)
    s2 = jnp.sum(lam[2:3] * lam[3:4], axis=-1, keepdims=True)
    lam_full = jnp.exp(s1) - jnp.exp(s2) + lam_init
    o = (a1 - lam_full * a2).T
    o_ref[0] = (_rms(o, sub_ref[...]) * (1.0 - lam_init)).astype(BF16)


def _attn_scratch(bq, dv):
    return [pltpu.VMEM((2, 1, bq), F32), pltpu.VMEM((2, dv + ONES_ROWS, bq), F32)]


def _vt_spec(nk):
    return pl.BlockSpec((1, 1, nk, LANES, ATTN_BLOCK), lambda bi, h, i: (bi, h, 0, 0, 0))


def _mla_attn(qf, kf, vt):
    b, s, _ = qf.shape
    bq = ATTN_BLOCK
    return pl.pallas_call(
        _mla_attn_kernel,
        out_shape=jax.ShapeDtypeStruct((b, s, MLA_HEADS * MLA_V), BF16),
        grid=(b, MLA_HEADS // 2, s // bq),
        in_specs=[pl.BlockSpec((1, bq, 2 * LANES), lambda bi, h, i: (bi, i, h)),
                  pl.BlockSpec((1, s, 2 * LANES), lambda bi, h, i: (bi, 0, h)),
                  _vt_spec(s // bq)],
        out_specs=pl.BlockSpec((1, bq, LANES), lambda bi, h, i: (bi, i, h)),
        scratch_shapes=_attn_scratch(bq, MLA_V),
        compiler_params=_cparams(("parallel", "parallel", "arbitrary")),
        name="mla_attn",
    )(qf, kf, vt)


def _diff_attn(dq, dk, dvt, lam, subln, lam_init):
    b, s, _ = dq.shape
    bq = ATTN_BLOCK
    blk = lambda rows, im: pl.BlockSpec((1, rows, LANES), im)
    return pl.pallas_call(
        functools.partial(_diff_attn_kernel, lam_init=lam_init),
        out_shape=jax.ShapeDtypeStruct((b, s, DIFF_HEADS * DIFF_V), BF16),
        grid=(b, DIFF_HEADS, s // bq),
        in_specs=[blk(bq, lambda bi, h, i: (bi, i, h)), blk(s, lambda bi, h, i: (bi, 0, h)),
                  _vt_spec(s // bq),
                  pl.BlockSpec(lam.shape, lambda bi, h, i: (0, 0)),
                  pl.BlockSpec(subln.shape, lambda bi, h, i: (0, 0))],
        out_specs=blk(bq, lambda bi, h, i: (bi, i, h)),
        scratch_shapes=_attn_scratch(bq, DIFF_V),
        compiler_params=_cparams(("parallel", "parallel", "arbitrary")),
        name="diff_attn",
    )(dq, dk, dvt, lam, subln)


def _mix_out_kernel(h_ref, a_ref, b_ref, w_ref, g_ref, rwh_ref, rwl_ref, rb_ref,
                    hout_ref, hf_ref, route_ref):
    mixed = jnp.concatenate([a_ref[...], b_ref[...]], axis=-1)
    h = h_ref[...] + _dot(mixed, w_ref[...])
    hout_ref[...] = h
    hf = _rms(h, g_ref[...])
    hi = hf.astype(BF16)
    hf_ref[...] = hi
    lo = (hf - hi.astype(F32)).astype(BF16)
    z = _dot(hi, rwh_ref[...]) + _dot(lo, rwh_ref[...]) + _dot(hi, rwl_ref[...]) + rb_ref[...]
    lane_i = lax.broadcasted_iota(jnp.int32, z.shape, 1)
    lane = lane_i.astype(F32)
    big = float(LANES)
    is_g = lane_i < MOE_GROUPS
    zg = jnp.where(is_g, z, NEG)
    mg = jnp.max(zg, axis=-1, keepdims=True)
    g_idx = jnp.min(jnp.where(zg == mg, lane, big), axis=-1, keepdims=True)
    pg_top = 1.0 / jnp.sum(jnp.where(is_g, jnp.exp(zg - mg), 0.0), axis=-1, keepdims=True)
    grp_of_lane = jnp.right_shift(lane_i - MOE_GROUPS, int(math.log2(MOE_EPG))).astype(F32)
    in_grp = (lane_i >= MOE_GROUPS) & (lane_i < MOE_GROUPS + MOE_EXPERTS) & (grp_of_lane == g_idx)
    ze = jnp.where(in_grp, z, NEG)
    m1 = jnp.max(ze, axis=-1, keepdims=True)
    i1 = jnp.min(jnp.where(ze == m1, lane, big), axis=-1, keepdims=True)
    ze2 = jnp.where(lane == i1, NEG, ze)
    m2 = jnp.max(ze2, axis=-1, keepdims=True)
    i2 = jnp.min(jnp.where(ze2 == m2, lane, big), axis=-1, keepdims=True)
    r = jnp.exp(m2 - m1)
    w1 = 1.0 / (1.0 + r)
    w2 = r / (1.0 + r)
    e1 = i1 - float(MOE_GROUPS)
    e2 = i2 - float(MOE_GROUPS)
    out = jnp.where(lane_i == 0, e1, jnp.where(lane_i == 1, e2, jnp.where(lane_i == 2, pg_top * w1,
                    jnp.where(lane_i == 3, pg_top * w2, 0.0))))
    route_ref[...] = out


def _mix_out(h, a, b, w, g, rwh, rwl, rb):
    n, d = h.shape
    tm = ROW_TILE
    row = lambda i: (i, 0)
    full = lambda i: (0, 0)
    wspec = lambda t: pl.BlockSpec(t.shape, full)
    return pl.pallas_call(
        _mix_out_kernel,
        out_shape=[jax.ShapeDtypeStruct((n, d), F32), jax.ShapeDtypeStruct((n, d), BF16),
                   jax.ShapeDtypeStruct((n, LANES), F32)],
        grid=(n // tm,),
        in_specs=[pl.BlockSpec((tm, d), row), pl.BlockSpec((tm, a.shape[1]), row), pl.BlockSpec((tm, b.shape[1]), row),
                  wspec(w), wspec(g), wspec(rwh), wspec(rwl), wspec(rb)],
        out_specs=[pl.BlockSpec((tm, d), row), pl.BlockSpec((tm, d), row), pl.BlockSpec((tm, LANES), row)],
        compiler_params=_cparams(("parallel",)),
        name="mix_out",
    )(h, a, b, w, g, rwh, rwl, rb)


def _expert_kernel(te_ref, nv_ref, x_ref, wg_ref, wu_ref, wd_ref, y_ref):
    t = pl.program_id(0)

    @pl.when(t < nv_ref[0])
    def _():
        x = x_ref[...]
        a = _dot(x, wg_ref[0])
        up = _dot(x, wu_ref[0])
        act = (a * _sigmoid(a) * up).astype(BF16)
        y_ref[...] = _dot(act, wd_ref[0]).astype(BF16)

    @pl.when(t >= nv_ref[0])
    def _():
        y_ref[...] = jnp.zeros(y_ref.shape, BF16)


def _experts(xs, tile_expert, n_valid, wg, wu, wd):
    p, d = xs.shape
    tm = EXPERT_TILE
    ff = wg.shape[2]
    grid_spec = pltpu.PrefetchScalarGridSpec(
        num_scalar_prefetch=2,
        grid=(p // tm,),
        in_specs=[pl.BlockSpec((tm, d), lambda t, te, nv: (t, 0)),
                  pl.BlockSpec((1, d, ff), lambda t, te, nv: (te[t], 0, 0)),
                  pl.BlockSpec((1, d, ff), lambda t, te, nv: (te[t], 0, 0)),
                  pl.BlockSpec((1, ff, d), lambda t, te, nv: (te[t], 0, 0))],
        out_specs=pl.BlockSpec((tm, d), lambda t, te, nv: (t, 0)),
    )
    return pl.pallas_call(
        _expert_kernel,
        out_shape=jax.ShapeDtypeStruct((p, d), BF16),
        grid_spec=grid_spec,
        compiler_params=_cparams(("arbitrary",)),
        name="experts",
    )(tile_expert, n_valid, xs, wg, wu, wd)


def _combine_kernel(h_ref, y_ref, route_ref, g_ref, o_ref, *, final):
    d = h_ref.shape[1]
    route = route_ref[...]
    y = y_ref[...]
    out = h_ref[...] + route[:, 2:3] * y[:, :d].astype(F32) + route[:, 3:4] * y[:, d:].astype(F32)
    if final:
        out = _rms(out, g_ref[...])
    o_ref[...] = out


def _combine(h, ypair, route, g, final):
    n, d = h.shape
    tm = ROW_TILE
    row = lambda i: (i, 0)
    return pl.pallas_call(
        functools.partial(_combine_kernel, final=final),
        out_shape=jax.ShapeDtypeStruct((n, d), F32),
        grid=(n // tm,),
        in_specs=[pl.BlockSpec((tm, d), row), pl.BlockSpec((tm, 2 * d), row), pl.BlockSpec((tm, LANES), row),
                  pl.BlockSpec(g.shape, lambda i: (0, 0))],
        out_specs=pl.BlockSpec((tm, d), row),
        compiler_params=_cparams(("parallel",)),
        name="moe_combine",
    )(h, ypair, route, g)


def _moe(h, hf, route, wg, wu, wd, g_final, final):
    n, d = h.shape
    tm = EXPERT_TILE
    e_flat = route[:, 0:2].astype(jnp.int32).reshape(-1)
    onehot = (e_flat[:, None] == jnp.arange(MOE_EXPERTS, dtype=jnp.int32)[None, :]).astype(jnp.int32)
    counts = jnp.sum(onehot, axis=0)
    padded = ((counts + tm - 1) // tm) * tm
    ends = jnp.cumsum(padded)
    starts = ends - padded
    order = jnp.argsort(e_flat, stable=True)
    sorted_e = e_flat[order]
    first_of_e = jnp.cumsum(counts) - counts
    rank = jnp.arange(2 * n, dtype=jnp.int32) - first_of_e[sorted_e]
    dest_sorted = starts[sorted_e] + rank
    n_rows = 2 * n + MOE_EXPERTS * tm
    tok_for_row = jnp.zeros((n_rows,), jnp.int32).at[dest_sorted].set((order // 2).astype(jnp.int32))
    dest = jnp.zeros((2 * n,), jnp.int32).at[order].set(dest_sorted.astype(jnp.int32))
    tile_start = jnp.arange(n_rows // tm, dtype=jnp.int32) * tm
    tile_expert = jnp.minimum(jnp.searchsorted(ends, tile_start, side="right"), MOE_EXPERTS - 1).astype(jnp.int32)
    n_valid = (ends[-1] // tm).astype(jnp.int32).reshape(1)
    xs = jnp.take(hf, tok_for_row, axis=0)
    ys = _experts(xs, tile_expert, n_valid, wg, wu, wd)
    ypair = jnp.take(ys, dest, axis=0).reshape(n, 2 * d)
    return _combine(h, ypair, route, g_final, final)


def _od_in_kernel(x_ref, g_ref, w_ref, ur_ref, uh_ref):
    hn = _rms(x_ref[...], g_ref[...]).astype(BF16)
    u = _dot(hn, w_ref[...])
    ur_ref[...] = u[:, :RWKV_IN]
    uh_ref[...] = u[:, RWKV_IN:]


def _od_in(x2, g, w):
    n, d = x2.shape
    tm = ROW_TILE
    row = lambda i: (i, 0)
    return pl.pallas_call(
        _od_in_kernel,
        out_shape=[jax.ShapeDtypeStruct((n, RWKV_IN), F32), jax.ShapeDtypeStruct((n, HGRN_IN), F32)],
        grid=(n // tm,),
        in_specs=[pl.BlockSpec((tm, d), row), pl.BlockSpec(g.shape, lambda i: (0, 0)),
                  pl.BlockSpec(w.shape, lambda i: (0, 0))],
        out_specs=[pl.BlockSpec((tm, RWKV_IN), row), pl.BlockSpec((tm, HGRN_IN), row)],
        compiler_params=_cparams(("parallel",)),
        name="od_in",
    )(x2, g, w)


def _rwkv_kernel(u_ref, mu_ref, w0_ref, w2_ref, a0_ref, a2_ref, g2_ref, kk_ref, ka_ref, rk_ref,
                 lnw_ref, lnb_ref, tri_ref, seg_ref, o_ref, prev_ref, ht_ref):
    c = pl.program_id(1)

    @pl.when(c == 0)
    def _():
        prev_ref[...] = jnp.zeros(prev_ref.shape, F32)
        ht_ref[...] = jnp.zeros(ht_ref.shape, F32)

    u = u_ref[0]
    rows = lax.broadcasted_iota(jnp.int32, u.shape, 0)
    u_prev = jnp.where(rows == 0, prev_ref[...], pltpu.roll(u, 1, 0))
    prev_ref[...] = u[CHUNK - 1:CHUNK, :]
    us = u + (u_prev - u) * mu_ref[...]
    w = RWKV_W
    r = us[:, 0:w]
    k = us[:, w:2 * w]
    v = us[:, 2 * w:3 * w]
    x12 = us[:, 3 * w:3 * w + LANES]
    xg = us[:, 3 * w + LANES:]
    seg = seg_ref[...]
    tri = tri_ref[...]

    wl = w0_ref[...] + _dot(jnp.tanh(x12).astype(BF16), w2_ref[...])
    nwl = -wl
    softplus = jnp.maximum(nwl, 0.0) + jnp.log1p(jnp.exp(-jnp.abs(nwl)))
    lw = -jnp.exp(-softplus - 0.5)
    a = _sigmoid(a0_ref[...] + _dot(x12.astype(BF16), a2_ref[...]))
    g = _dot(_sigmoid(xg).astype(BF16), g2_ref[...])
    kk = k * kk_ref[...]
    kk = kk * lax.rsqrt(jnp.maximum(_split_dot(kk * kk, seg), 1e-24))
    k2 = k * (1.0 + (a - 1.0) * ka_ref[...])
    a_in = -kk
    b_in = kk * a

    lg = _split_dot(lw, tri, left=True)
    lg_c = lg[CHUNK - 1:CHUNK, :]
    e_neg = jnp.exp(-lg)
    e_rel = jnp.exp(lg_c - lg)
    g_c = jnp.exp(lg_c)
    at = a_in * jnp.exp(lg - lw)
    rt = r * jnp.exp(lg)
    kt = k2 * e_neg
    bt = b_in * e_neg
    kh = k2 * e_rel
    bh = b_in * e_rel

    c2 = 2 * CHUNK
    ri = lax.broadcasted_iota(jnp.int32, (c2, c2), 0)
    ci = lax.broadcasted_iota(jnp.int32, (c2, c2), 1)
    sh = int(math.log2(CHUNK))
    same = jnp.right_shift(ri, sh) == jnp.right_shift(ci, sh)
    strict = same & (ri > ci)
    incl = same & (ri >= ci)
    eye = jnp.where(ri == ci, 1.0, 0.0).astype(F32)
    lane = lax.broadcasted_iota(jnp.int32, (CHUNK, LANES), 1)
    lo_half = lane < RWKV_HD

    def bd(x):
        return jnp.concatenate([jnp.where(lo_half, x, 0.0), jnp.where(lo_half, 0.0, x)], axis=0)

    ys = []
    for p in range(RWKV_HEADS // 2):
        sl = slice(p * LANES, (p + 1) * LANES)
        at_b, rt_b, kt_b, bt_b = bd(at[:, sl]), bd(rt[:, sl]), bd(kt[:, sl]), bd(bt[:, sl])
        kh_b, bh_b, v_b = bd(kh[:, sl]), bd(bh[:, sl]), bd(v[:, sl])
        sc = _dot_nt(jnp.concatenate([at_b, rt_b], axis=0).astype(BF16),
                     jnp.concatenate([kt_b, bt_b], axis=0).astype(BF16))
        a_ak = jnp.where(strict, sc[:c2, :c2], 0.0).astype(BF16)
        a_ab = jnp.where(strict, sc[:c2, c2:], 0.0)
        a_rk = jnp.where(incl, sc[c2:, :c2], 0.0).astype(BF16)
        a_rb = jnp.where(incl, sc[c2:, c2:], 0.0).astype(BF16)
        pw = a_ab
        t_inv = eye + a_ab
        for _ in range(int(math.log2(CHUNK)) - 1):
            pwb = pw.astype(BF16)
            pw = _dot(pwb, pwb)
            t_inv = t_inv + _dot(t_inv.astype(BF16), pw.astype(BF16))
        v_bb = v_b.astype(BF16)
        x2 = _dot(a_ak, v_bb)
        tx = _dot(t_inv.astype(BF16), jnp.concatenate([at_b, x2], axis=1).astype(BF16))
        txb = tx.astype(BF16)
        qe = _dot(a_rb, txb)
        q_b = rt_b + qe[:, :LANES]
        e_b = _dot(a_rk, v_bb) + qe[:, LANES:]
        bhb = bh_b.astype(BF16)
        gd = _dot_tn(txb, bhb)
        g2t = gd[:LANES]
        dmt = _dot_tn(v_bb, kh_b.astype(BF16)) + gd[LANES:]
        ht = ht_ref[p]
        htb = ht.astype(BF16)
        y_b = _dot_nt(q_b.astype(BF16), htb) + e_b
        ht_ref[p] = jnp.where(same, ht * g_c[:, sl] + _dot(htb, g2t.astype(BF16)) + dmt, 0.0)
        ys.append(y_b[:CHUNK] + y_b[CHUNK:])
    y = jnp.concatenate(ys, axis=1)

    inv_n = 1.0 / RWKV_HD
    mean = _split_dot(y, seg) * inv_n
    dlt = y - mean
    var = _split_dot(dlt * dlt, seg) * inv_n
    yn = dlt * lax.rsqrt(var + RWKV_GN_EPS) * lnw_ref[...] + lnb_ref[...]
    bonus = _split_dot(r * k2 * rk_ref[...], seg) * v
    o_ref[0] = ((yn + bonus) * g).astype(BF16)


def _rwkv(u, mu, w0, w2p, a0, a2p, g2, k_k, k_a, r_k, ln_w, ln_b, tri, seg):
    b, s, _ = u.shape
    full = lambda bi, ci: (0, 0)
    wspec = lambda t: pl.BlockSpec(t.shape, full)
    params = [mu, w0, w2p, a0, a2p, g2, k_k, k_a, r_k, ln_w, ln_b, tri, seg]
    return pl.pallas_call(
        _rwkv_kernel,
        out_shape=jax.ShapeDtypeStruct((b, s, RWKV_W), BF16),
        grid=(b, s // CHUNK),
        in_specs=[pl.BlockSpec((1, CHUNK, RWKV_IN), lambda bi, ci: (bi, ci, 0))] + [wspec(t) for t in params],
        out_specs=pl.BlockSpec((1, CHUNK, RWKV_W), lambda bi, ci: (bi, ci, 0)),
        scratch_shapes=[pltpu.VMEM((1, RWKV_IN), F32), pltpu.VMEM((RWKV_HEADS // 2, LANES, LANES), F32)],
        compiler_params=_cparams(("parallel", "arbitrary")),
        name="rwkv7",
    )(u, *params)


def _hgrn_kernel(u_ref, lb_ref, on_ref, tri_ref, ones_ref, o_ref, st_ref):
    c = pl.program_id(1)

    @pl.when(c == 0)
    def _():
        st_ref[...] = jnp.zeros(st_ref.shape, F32)

    u = u_ref[0]
    lbp = lb_ref[...]
    mx = jnp.maximum(lbp[0:1], lbp[1:2])
    e0 = jnp.exp(lbp[0:1] - mx)
    e1 = jnp.exp(lbp[1:2] - mx)
    p0 = e0 / (e0 + e1)
    p1 = e1 / (e0 + e1)
    lb = (p0 + p1) - p0
    tri = tri_ref[...]
    ones = ones_ref[...]
    d = HGRN_D
    srow = lax.broadcasted_iota(jnp.int32, (SUB, d), 0)
    for h in range(HGRN_HEADS):
        sl = slice(h * d, (h + 1) * d)
        q = u[:, sl]
        z = u[:, HGRN_W + h * d:HGRN_W + (h + 1) * d]
        iv = u[:, 2 * HGRN_W + h * d:2 * HGRN_W + (h + 1) * d]
        gt = u[:, 3 * HGRN_W + h * d:3 * HGRN_W + (h + 1) * d]
        lbh = lb[:, sl]
        qs = q * _sigmoid(q)
        log_sig = jnp.minimum(z, 0.0) - jnp.log1p(jnp.exp(-jnp.abs(z)))
        x1 = jnp.log(lbh)
        x2 = jnp.log1p(-lbh) + log_sig
        log_f = jnp.maximum(x1, x2) + jnp.log1p(jnp.exp(-jnp.abs(x1 - x2)))
        key = (1.0 - lbh) * _sigmoid(-z)
        bc = _split_dot(log_f, tri, left=True)
        st = st_ref[h]
        o = _dot_nt((qs * jnp.exp(bc)).astype(BF16), st.astype(BF16))
        ivb = iv.astype(BF16)
        blocks = []
        for blk in range(CHUNK // SUB):
            r0 = blk * SUB
            b_i = bc[r0:r0 + SUB]
            q_i = qs[r0:r0 + SUB]
            k_i = key[r0:r0 + SUB]
            v_i = iv[r0:r0 + SUB]
            p_rows = []
            for t in range(SUB):
                wgt = jnp.where(srow <= t, jnp.exp(b_i[t:t + 1] - b_i), 0.0)
                p_rows.append(q_i[t:t + 1] * k_i * wgt)
            rs = _dot(jnp.concatenate(p_rows, axis=0).astype(BF16), ones)
            o_rows = [jnp.sum(rs[t * SUB:(t + 1) * SUB] * v_i, axis=0, keepdims=True) for t in range(SUB)]
            o_i = jnp.concatenate(o_rows, axis=0)
            if blk > 0:
                b_m = bc[r0 - 1:r0]
                qp = (q_i * jnp.exp(b_i - b_m)).astype(BF16)
                kp = (key[:r0] * jnp.exp(b_m - bc[:r0])).astype(BF16)
                o_i = o_i + _dot(_dot_nt(qp, kp).astype(BF16), ivb[:r0])
            blocks.append(o_i)
        o = o + jnp.concatenate(blocks, axis=0)
        b_l = bc[CHUNK - 1:CHUNK]
        st_ref[h] = st * jnp.exp(b_l) + _dot_tn(ivb, (key * jnp.exp(b_l - bc)).astype(BF16))
        on = _rms(o, on_ref[:, sl])
        o_ref[0, :, sl] = (on * (gt * _sigmoid(gt))).astype(BF16)


def _hgrn(u, lbp, o_norm, tri, ones):
    b, s, _ = u.shape
    full = lambda bi, ci: (0, 0)
    wspec = lambda t: pl.BlockSpec(t.shape, full)
    return pl.pallas_call(
        _hgrn_kernel,
        out_shape=jax.ShapeDtypeStruct((b, s, HGRN_W), BF16),
        grid=(b, s // CHUNK),
        in_specs=[pl.BlockSpec((1, CHUNK, HGRN_IN), lambda bi, ci: (bi, ci, 0)), wspec(lbp), wspec(o_norm),
                  wspec(tri), wspec(ones)],
        out_specs=pl.BlockSpec((1, CHUNK, HGRN_W), lambda bi, ci: (bi, ci, 0)),
        scratch_shapes=[pltpu.VMEM((HGRN_HEADS, HGRN_D, HGRN_D), F32)],
        compiler_params=_cparams(("parallel", "arbitrary")),
        name="hgrn2",
    )(u, lbp, o_norm, tri, ones)


def _rope_tables(seq):
    pos = jnp.arange(seq, dtype=F32)[:, None]

    def cs(half):
        inv = jnp.power(ROPE_THETA, -jnp.arange(half, dtype=F32) / half)
        ang = pos * inv[None, :]
        return jnp.cos(ang), jnp.sin(ang)

    cd, sd = cs(DIFF_HD // 2)
    cos_d = jnp.tile(jnp.concatenate([cd, cd], axis=1), (1, LANES // DIFF_HD))
    sin_d = jnp.tile(jnp.concatenate([-sd, sd], axis=1), (1, LANES // DIFF_HD))
    cm, sm = cs(MLA_ROPE // 2)
    one = jnp.ones((seq, MLA_NOPE), F32)
    tail = LANES - MLA_QK
    cos_m = jnp.concatenate([one, cm, cm, jnp.ones((seq, tail), F32)], axis=1)
    sin_m = jnp.concatenate([0 * one, -sm, sm, jnp.zeros((seq, tail), F32)], axis=1)
    return cos_d, sin_d, cos_m, sin_m


def _router_weights(w_group, b_group, w_expert, b_expert):
    d = w_group.shape[0]
    pad = LANES - MOE_GROUPS - MOE_EXPERTS
    w = jnp.concatenate([w_group, w_expert, jnp.zeros((d, pad), F32)], axis=1)
    bias = jnp.concatenate([b_group, b_expert, jnp.zeros((pad,), F32)])[None, :]
    hi = w.astype(BF16)
    lo = (w - hi.astype(F32)).astype(BF16)
    return hi, lo, bias


def kernel(x, norm_mix, norm_ffn, norm_final, ev_w_in, ev_w_out, mla_q_norm, mla_w_q_up, mla_kv_norm, mla_w_kv_up, diff_lambda, diff_subln, od_w_in, od_w_out, rwkv_mu, rwkv_w0, rwkv_w2, rwkv_a0, rwkv_a2, rwkv_g2, rwkv_k_k, rwkv_k_a, rwkv_r_k, rwkv_ln_w, rwkv_ln_b, hgrn_lb, hgrn_o_norm, moe_w_group, moe_b_group, moe_w_expert, moe_b_expert, moe_w_gate, moe_w_up, moe_w_down):
    b, s, d = x.shape
    n = b * s
    assert norm_mix.shape[0] == 2 and hgrn_lb.shape[0] == 2
    assert s % ATTN_BLOCK == 0 and s % ROW_TILE == 0 and ATTN_BLOCK % CHUNK == 0
    row2 = lambda t: t.reshape(1, -1)
    h = x.reshape(n, d)

    w_in = ev_w_in[0]
    o1, o2, o3 = MLA_LORA, 2 * MLA_LORA, 2 * MLA_LORA + MLA_ROPE
    kr_pad = jnp.zeros((d, LANES), F32).at[:, MLA_NOPE:MLA_QK].set(w_in[:, o2:o3])
    w0 = jnp.concatenate([w_in[:, :o2], kr_pad, w_in[:, o3:]], axis=1).astype(BF16)
    wq = mla_w_q_up[0].reshape(MLA_LORA, MLA_HEADS, MLA_QK)
    wq = jnp.pad(wq, ((0, 0), (0, 0), (0, LANES - MLA_QK))).reshape(MLA_LORA, MLA_HEADS * LANES).astype(BF16)
    wkv = mla_w_kv_up[0].reshape(MLA_LORA, MLA_HEADS, MLA_NOPE + MLA_V)
    wk = jnp.pad(wkv[:, :, :MLA_NOPE], ((0, 0), (0, 0), (0, LANES - MLA_NOPE)))
    wk = wk.reshape(MLA_LORA, MLA_HEADS * LANES).astype(BF16)
    wv = wkv[:, :, MLA_NOPE:].reshape(MLA_LORA, MLA_HEADS * MLA_V).astype(BF16)
    cos_d, sin_d, cos_m, sin_m = _rope_tables(s)
    qf, kf, vm, dq, dk, dv = _ev_in(h, row2(norm_mix[0]), w0, row2(mla_q_norm[0]), row2(mla_kv_norm[0]),
                                    wq, wk, wv, cos_d, sin_d, cos_m, sin_m, s)
    r3 = lambda t: t.reshape(b, s, t.shape[-1])
    o_mla = _mla_attn(r3(qf), r3(kf), vm)
    lam_init = 0.8 - 0.6 * math.exp(-0.3 * 0)
    o_diff = _diff_attn(r3(dq), r3(dk), dv, diff_lambda[0], row2(diff_subln[0]), lam_init)
    rwh, rwl, rb = _router_weights(moe_w_group[0], moe_b_group[0], moe_w_expert[0], moe_b_expert[0])
    h, hf, route = _mix_out(h, o_mla.reshape(n, -1), o_diff.reshape(n, -1), ev_w_out[0].astype(BF16),
                            row2(norm_ffn[0]), rwh, rwl, rb)
    h = _moe(h, hf, route, moe_w_gate[0].astype(BF16), moe_w_up[0].astype(BF16), moe_w_down[0].astype(BF16),
             row2(norm_final), False)

    ur, uh = _od_in(h, row2(norm_mix[1]), od_w_in[0].astype(BF16))
    zpad = jnp.zeros((RWKV_DECAY_LORA, RWKV_W), F32)
    w2p = jnp.concatenate([rwkv_w2[0], zpad], axis=0).astype(BF16)
    a2p = jnp.concatenate([zpad, rwkv_a2[0]], axis=0).astype(BF16)
    ci = jnp.arange(CHUNK)
    tri = (ci[None, :] <= ci[:, None]).astype(BF16)
    li = jnp.arange(RWKV_W) // RWKV_HD
    seg = (li[:, None] == li[None, :]).astype(BF16)
    o_c = _rwkv(ur.reshape(b, s, RWKV_IN), row2(rwkv_mu[0]), row2(rwkv_w0[0]), w2p, row2(rwkv_a0[0]), a2p,
                rwkv_g2[0].astype(BF16), row2(rwkv_k_k[0]), row2(rwkv_k_a[0]), row2(rwkv_r_k[0]),
                row2(rwkv_ln_w[0]), row2(rwkv_ln_b[0]), tri, seg)
    o_d = _hgrn(uh.reshape(b, s, HGRN_IN), hgrn_lb, row2(hgrn_o_norm[0]), tri, jnp.ones((LANES, LANES), BF16))
    rwh, rwl, rb = _router_weights(moe_w_group[1], moe_b_group[1], moe_w_expert[1], moe_b_expert[1])
    h, hf, route = _mix_out(h, o_c.reshape(n, -1), o_d.reshape(n, -1), od_w_out[0].astype(BF16),
                            row2(norm_ffn[1]), rwh, rwl, rb)
    out = _moe(h, hf, route, moe_w_gate[1].astype(BF16), moe_w_up[1].astype(BF16), moe_w_down[1].astype(BF16),
               row2(norm_final), True)
    return out.reshape(b, s, d)
```

```python
import functools
import math

import jax
import jax.numpy as jnp
from jax import lax
from jax.experimental import pallas as pl
from jax.experimental.pallas import tpu as pltpu

F32 = jnp.float32
BF16 = jnp.bfloat16

CHUNK = 64
ROPE_THETA = 10000.0
NORM_EPS = 1e-6
MLA_HEADS = 8
MLA_LORA = 256
MLA_NOPE = 64
MLA_ROPE = 32
MLA_V = 64
MLA_QK = MLA_NOPE + MLA_ROPE
DIFF_HEADS = 4
DIFF_HD = 64
DIFF_V = 2 * DIFF_HD
DIFF_W = DIFF_HEADS * 2 * DIFF_HD
RWKV_HEADS = 8
RWKV_HD = 64
RWKV_W = RWKV_HEADS * RWKV_HD
RWKV_DECAY_LORA = 64
RWKV_A_LORA = 64
RWKV_GATE_LORA = 128
RWKV_IN = 3 * RWKV_W + RWKV_DECAY_LORA + RWKV_A_LORA + RWKV_GATE_LORA
RWKV_GN_EPS = 64e-5
HGRN_HEADS = 4
HGRN_D = 128
HGRN_W = HGRN_HEADS * HGRN_D
HGRN_IN = 4 * HGRN_W
MOE_GROUPS = 4
MOE_EPG = 8
MOE_EXPERTS = MOE_GROUPS * MOE_EPG
MOE_FF = 512

LANES = 128
ROW_TILE = 256
ATTN_BLOCK = 512
EXPERT_TILE = 256
SUB = 16
VMEM_LIMIT = 48 * 1024 * 1024
NEG = -1e30
LOG2E = math.log2(math.e)


def _cparams(sem):
    return pltpu.CompilerParams(dimension_semantics=sem, vmem_limit_bytes=VMEM_LIMIT)


def _dot(a, b):
    return jnp.dot(a, b, preferred_element_type=F32)


def _dot_nt(a, b):
    return lax.dot_general(a, b, (((1,), (1,)), ((), ())), preferred_element_type=F32)


def _dot_tn(a, b):
    return lax.dot_general(a, b, (((0,), (0,)), ((), ())), preferred_element_type=F32)


def _rms(x, g):
    return x * lax.rsqrt(jnp.mean(x * x, axis=-1, keepdims=True) + NORM_EPS) * g


def _sigmoid(x):
    return 1.0 / (1.0 + jnp.exp(-x))


def _split_dot(x, w_bf16, left=False):
    hi = x.astype(BF16)
    lo = (x - hi.astype(F32)).astype(BF16)
    if left:
        return _dot(w_bf16, hi) + _dot(w_bf16, lo)
    return _dot(hi, w_bf16) + _dot(lo, w_bf16)


def _tile_lanes(t, n):
    return jnp.concatenate([t] * n, axis=-1) if n > 1 else t


def _rope_rot(x, half):
    w = x.shape[-1]
    lane = lax.broadcasted_iota(jnp.int32, x.shape, x.ndim - 1)
    first = (lane & (2 * half - 1)) < half
    return jnp.where(first, pltpu.roll(x, w - half, x.ndim - 1), pltpu.roll(x, half, x.ndim - 1))


def _ev_in_kernel(x_ref, g_ref, w0_ref, qg_ref, kvg_ref, wq_ref, wk_ref, wv_ref,
                  cd_ref, sd_ref, cm_ref, sm_ref,
                  qf_ref, kf_ref, vm_ref, dq_ref, dk_ref, dv_ref):
    hn = _rms(x_ref[...], g_ref[...]).astype(BF16)
    u = _dot(hn, w0_ref[...])
    c_q = u[:, 0:256]
    c_kv = u[:, 256:512]
    kr = u[:, 512:640]
    dq = u[:, 640:1152]
    dk = u[:, 1152:1664]
    dv = u[:, 1664:2176]
    cqn = _rms(c_q, qg_ref[...]).astype(BF16)
    ckn = _rms(c_kv, kvg_ref[...]).astype(BF16)
    q = _dot(cqn, wq_ref[...])
    k = _dot(ckn, wk_ref[...])
    v = _dot(ckn, wv_ref[...])
    cm = cm_ref[...]
    sm = sm_ref[...]
    m_half = MLA_ROPE // 2
    lane = lax.broadcasted_iota(jnp.int32, q.shape, 1) & (LANES - 1)
    first = (lane >= MLA_NOPE) & (lane < MLA_NOPE + m_half)
    rot_q = jnp.where(first, pltpu.roll(q, q.shape[1] - m_half, 1), pltpu.roll(q, m_half, 1))
    q = (q * _tile_lanes(cm, MLA_HEADS) + rot_q * _tile_lanes(sm, MLA_HEADS)) * (MLA_QK ** -0.5 * LOG2E)
    lane1 = lax.broadcasted_iota(jnp.int32, kr.shape, 1)
    first1 = (lane1 >= MLA_NOPE) & (lane1 < MLA_NOPE + m_half)
    rot_k = jnp.where(first1, pltpu.roll(kr, LANES - m_half, 1), pltpu.roll(kr, m_half, 1))
    kr = kr * cm + rot_k * sm
    k = k + _tile_lanes(kr, MLA_HEADS)
    qf_ref[...] = q.astype(BF16)
    kf_ref[...] = k.astype(BF16)
    vm_ref[0, :, 0] = v.T.reshape(vm_ref.shape[1], LANES, v.shape[0]).astype(BF16)
    cd = _tile_lanes(cd_ref[...], DIFF_W // LANES)
    sd = _tile_lanes(sd_ref[...], DIFF_W // LANES)
    dq = (dq * cd + _rope_rot(dq, DIFF_HD // 2) * sd) * (DIFF_HD ** -0.5 * LOG2E)
    dk = dk * cd + _rope_rot(dk, DIFF_HD // 2) * sd
    dq_ref[...] = dq.astype(BF16)
    dk_ref[...] = dk.astype(BF16)
    dv_ref[0, :, 0] = dv.T.reshape(dv_ref.shape[1], LANES, dv.shape[0]).astype(BF16)


def _ev_in(x2, g, w0, qg, kvg, wq, wk, wv, cd, sd, cm, sm, seq):
    n, d = x2.shape
    tm = ATTN_BLOCK
    nt = seq // tm
    row = lambda i: (i, 0)
    full = lambda i: (0, 0)
    tab = lambda i: (i % nt, 0)
    wspec = lambda a: pl.BlockSpec(a.shape, full)
    vt_shape = lambda w: jax.ShapeDtypeStruct((n // seq, w // LANES, nt, LANES, tm), BF16)
    vt_spec = lambda w: pl.BlockSpec((1, w // LANES, 1, LANES, tm), lambda i: (i // nt, 0, i % nt, 0, 0))
    rows = lambda w: jax.ShapeDtypeStruct((n, w), BF16)
    rspec = lambda w: pl.BlockSpec((tm, w), row)
    wide = MLA_HEADS * LANES
    return pl.pallas_call(
        _ev_in_kernel,
        out_shape=[rows(wide), rows(wide), vt_shape(MLA_HEADS * MLA_V), rows(DIFF_W), rows(DIFF_W), vt_shape(DIFF_W)],
        grid=(n // tm,),
        in_specs=[pl.BlockSpec((tm, d), row), wspec(g), wspec(w0), wspec(qg), wspec(kvg), wspec(wq), wspec(wk),
                  wspec(wv)] + [pl.BlockSpec((tm, LANES), tab)] * 4,
        out_specs=[rspec(wide), rspec(wide), vt_spec(MLA_HEADS * MLA_V), rspec(DIFF_W), rspec(DIFF_W),
                   vt_spec(DIFF_W)],
        compiler_params=_cparams(("parallel",)),
        name="ev_in",
    )(x2, g, w0, qg, kvg, wq, wk, wv, cd, sd, cm, sm)


ONES_ROWS = 16


def _online_update(st, vt, m_ref, acc_ref, idx):
    m_prev = m_ref[idx]
    m_new = jnp.maximum(m_prev, jnp.max(st, axis=0, keepdims=True))
    alpha = jnp.exp2(m_prev - m_new)
    p = jnp.exp2(st - m_new).astype(BF16)
    acc_ref[idx] = alpha * acc_ref[idx] + _dot(vt, p)
    m_ref[idx] = m_new


def _chunk_mask_t(shape):
    sh = int(math.log2(CHUNK))
    key_chunk = jnp.right_shift(lax.broadcasted_iota(jnp.int32, shape, 0), sh)
    q_chunk = jnp.right_shift(lax.broadcasted_iota(jnp.int32, shape, 1), sh)
    return key_chunk <= q_chunk


def _attn_loop(i, q_a, q_b, k_ref, vt_ref, m_ref, acc_ref, split):
    bk = ATTN_BLOCK
    m_ref[...] = jnp.full(m_ref.shape, NEG, F32)
    acc_ref[...] = jnp.zeros(acc_ref.shape, F32)
    dv = acc_ref.shape[1] - ONES_ROWS
    ones = jnp.ones((ONES_ROWS, bk), BF16)

    def scores(j):
        kb = k_ref[0, pl.ds(pl.multiple_of(j * bk, bk), bk), :]
        vt = vt_ref[0, 0, j]
        if split:
            va = jnp.concatenate([vt[:dv], ones], axis=0)
            vb = jnp.concatenate([vt[dv:], ones], axis=0)
            return _dot_nt(kb[:, :LANES], q_a), _dot_nt(kb[:, LANES:], q_b), va, vb
        va = jnp.concatenate([vt, ones], axis=0)
        return _dot_nt(kb, q_a), _dot_nt(kb, q_b), va, va

    def body(j, carry):
        sa, sb, va, vb = scores(j)
        _online_update(sa, va, m_ref, acc_ref, 0)
        _online_update(sb, vb, m_ref, acc_ref, 1)
        return carry

    lax.fori_loop(0, i, body, 0)
    sa, sb, va, vb = scores(i)
    mask = _chunk_mask_t(sa.shape)
    _online_update(jnp.where(mask, sa, NEG), va, m_ref, acc_ref, 0)
    _online_update(jnp.where(mask, sb, NEG), vb, m_ref, acc_ref, 1)
    acc_a = acc_ref[0]
    acc_b = acc_ref[1]
    return acc_a[:dv] / acc_a[dv:dv + 1], acc_b[:dv] / acc_b[dv:dv + 1]


def _mla_attn_kernel(q_ref, k_ref, vt_ref, o_ref, m_ref, acc_ref):
    q = q_ref[0]
    o_a, o_b = _attn_loop(pl.program_id(2), q[:, :LANES], q[:, LANES:], k_ref, vt_ref, m_ref, acc_ref, True)
    o_ref[0] = jnp.concatenate([o_a, o_b], axis=0).T.astype(BF16)


def _diff_attn_kernel(q_ref, k_ref, vt_ref, lam_ref, sub_ref, o_ref, m_ref, acc_ref, *, lam_init):
    q = q_ref[0]
    lane = lax.broadcasted_iota(jnp.int32, q.shape, 1)
    zero = jnp.zeros_like(q)
    a1, a2 = _attn_loop(pl.program_id(2), jnp.where(lane < DIFF_HD, q, zero), jnp.where(lane >= DIFF_HD, q, zero),
                        k_ref, vt_ref, m_ref, acc_ref, False)
    lam = lam_ref[...]
    s1 = jnp.sum(lam[0:1] * lam[1:2], axis=-1, keepdims=True)
    s2 = jnp.sum(lam[2:3] * lam[3:4], axis=-1, keepdims=True)
    lam_full = jnp.exp(s1) - jnp.exp(s2) + lam_init
    o = (a1 - lam_full * a2).T
    o_ref[0] = (_rms(o, sub_ref[...]) * (1.0 - lam_init)).astype(BF16)


def _attn_scratch(bq, dv):
    return [pltpu.VMEM((2, 1, bq), F32), pltpu.VMEM((2, dv + ONES_ROWS, bq), F32)]


def _vt_spec(nk):
    return pl.BlockSpec((1, 1, nk, LANES, ATTN_BLOCK), lambda bi, h, i: (bi, h, 0, 0, 0))


def _mla_attn(qf, kf, vt):
    b, s, _ = qf.shape
    bq = ATTN_BLOCK
    return pl.pallas_call(
        _mla_attn_kernel,
        out_shape=jax.ShapeDtypeStruct((b, s, MLA_HEADS * MLA_V), BF16),
        grid=(b, MLA_HEADS // 2, s // bq),
        in_specs=[pl.BlockSpec((1, bq, 2 * LANES), lambda bi, h, i: (bi, i, h)),
                  pl.BlockSpec((1, s, 2 * LANES), lambda bi, h, i: (bi, 0, h)),
                  _vt_spec(s // bq)],
        out_specs=pl.BlockSpec((1, bq, LANES), lambda bi, h, i: (bi, i, h)),
        scratch_shapes=_attn_scratch(bq, MLA_V),
        compiler_params=_cparams(("parallel", "parallel", "arbitrary")),
        name="mla_attn",
    )(qf, kf, vt)


def _diff_attn(dq, dk, dvt, lam, subln, lam_init):
    b, s, _ = dq.shape
    bq = ATTN_BLOCK
    blk = lambda rows, im: pl.BlockSpec((1, rows, LANES), im)
    return pl.pallas_call(
        functools.partial(_diff_attn_kernel, lam_init=lam_init),
        out_shape=jax.ShapeDtypeStruct((b, s, DIFF_HEADS * DIFF_V), BF16),
        grid=(b, DIFF_HEADS, s // bq),
        in_specs=[blk(bq, lambda bi, h, i: (bi, i, h)), blk(s, lambda bi, h, i: (bi, 0, h)),
                  _vt_spec(s // bq),
                  pl.BlockSpec(lam.shape, lambda bi, h, i: (0, 0)),
                  pl.BlockSpec(subln.shape, lambda bi, h, i: (0, 0))],
        out_specs=blk(bq, lambda bi, h, i: (bi, i, h)),
        scratch_shapes=_attn_scratch(bq, DIFF_V),
        compiler_params=_cparams(("parallel", "parallel", "arbitrary")),
        name="diff_attn",
    )(dq, dk, dvt, lam, subln)


def _mix_out_kernel(h_ref, a_ref, b_ref, w_ref, g_ref, rwh_ref, rwl_ref, rb_ref, tri_ref,
                    hout_ref, hf_ref, route_ref, cnt_ref):
    mixed = jnp.concatenate([a_ref[...], b_ref[...]], axis=-1)
    h = h_ref[...] + _dot(mixed, w_ref[...])
    hout_ref[...] = h
    hf = _rms(h, g_ref[...])
    hi = hf.astype(BF16)
    hf_ref[...] = hi
    lo = (hf - hi.astype(F32)).astype(BF16)
    z = _dot(hi, rwh_ref[...]) + _dot(lo, rwh_ref[...]) + _dot(hi, rwl_ref[...]) + rb_ref[...]
    lane_i = lax.broadcasted_iota(jnp.int32, z.shape, 1)
    lane = lane_i.astype(F32)
    big = float(LANES)
    is_g = lane_i < MOE_GROUPS
    zg = jnp.where(is_g, z, NEG)
    mg = jnp.max(zg, axis=-1, keepdims=True)
    g_idx = jnp.min(jnp.where(zg == mg, lane, big), axis=-1, keepdims=True)
    pg_top = 1.0 / jnp.sum(jnp.where(is_g, jnp.exp(zg - mg), 0.0), axis=-1, keepdims=True)
    grp_of_lane = jnp.right_shift(lane_i - MOE_GROUPS, int(math.log2(MOE_EPG))).astype(F32)
    in_grp = (lane_i >= MOE_GROUPS) & (lane_i < MOE_GROUPS + MOE_EXPERTS) & (grp_of_lane == g_idx)
    ze = jnp.where(in_grp, z, NEG)
    m1 = jnp.max(ze, axis=-1, keepdims=True)
    i1 = jnp.min(jnp.where(ze == m1, lane, big), axis=-1, keepdims=True)
    ze2 = jnp.where(lane == i1, NEG, ze)
    m2 = jnp.max(ze2, axis=-1, keepdims=True)
    i2 = jnp.min(jnp.where(ze2 == m2, lane, big), axis=-1, keepdims=True)
    r = jnp.exp(m2 - m1)
    w1 = 1.0 / (1.0 + r)
    w2 = r / (1.0 + r)
    e1 = i1 - float(MOE_GROUPS)
    e2 = i2 - float(MOE_GROUPS)
    @pl.when(pl.program_id(0) == 0)
    def _():
        cnt_ref[...] = jnp.zeros(cnt_ref.shape, F32)

    both = jnp.where((lane == i1) | (lane == i2), 1.0, 0.0)
    before = _dot(tri_ref[...], both.astype(BF16)) + cnt_ref[0:1, :]
    rank1 = jnp.sum(jnp.where(lane == i1, before, 0.0), axis=-1, keepdims=True)
    rank2 = jnp.sum(jnp.where(lane == i2, before, 0.0), axis=-1, keepdims=True)
    cnt_ref[...] = cnt_ref[...] + jnp.sum(both, axis=0, keepdims=True)
    vals = (e1, e2, pg_top * w1, pg_top * w2, rank1, rank2)
    out = jnp.zeros(z.shape, F32)
    for col, val in enumerate(vals):
        out = jnp.where(lane_i == col, val, out)
    route_ref[...] = out


def _mix_out(h, a, b, w, g, rwh, rwl, rb, tri):
    n, d = h.shape
    tm = ROW_TILE
    row = lambda i: (i, 0)
    full = lambda i: (0, 0)
    wspec = lambda t: pl.BlockSpec(t.shape, full)
    return pl.pallas_call(
        _mix_out_kernel,
        out_shape=[jax.ShapeDtypeStruct((n, d), F32), jax.ShapeDtypeStruct((n, d), BF16),
                   jax.ShapeDtypeStruct((n, LANES), F32), jax.ShapeDtypeStruct((8, LANES), F32)],
        grid=(n // tm,),
        in_specs=[pl.BlockSpec((tm, d), row), pl.BlockSpec((tm, a.shape[1]), row), pl.BlockSpec((tm, b.shape[1]), row),
                  wspec(w), wspec(g), wspec(rwh), wspec(rwl), wspec(rb), wspec(tri)],
        out_specs=[pl.BlockSpec((tm, d), row), pl.BlockSpec((tm, d), row), pl.BlockSpec((tm, LANES), row),
                   pl.BlockSpec((8, LANES), full)],
        compiler_params=_cparams(("arbitrary",)),
        name="mix_out",
    )(h, a, b, w, g, rwh, rwl, rb, tri)


def _expert_kernel(te_ref, nv_ref, x_ref, wg_ref, wu_ref, wd_ref, y_ref, wgb_ref, wub_ref, wdb_ref):
    t = pl.program_id(0)

    @pl.when((t == 0) | (te_ref[t] != te_ref[jnp.maximum(t - 1, 0)]))
    def _():
        wgb_ref[...] = wg_ref[0].astype(BF16)
        wub_ref[...] = wu_ref[0].astype(BF16)
        wdb_ref[...] = wd_ref[0].astype(BF16)

    @pl.when(t < nv_ref[0])
    def _():
        x = x_ref[...]
        a = _dot(x, wgb_ref[...])
        up = _dot(x, wub_ref[...])
        act = (a * _sigmoid(a) * up).astype(BF16)
        y_ref[...] = _dot(act, wdb_ref[...]).astype(BF16)

    @pl.when(t >= nv_ref[0])
    def _():
        y_ref[...] = jnp.zeros(y_ref.shape, BF16)


def _experts(xs, tile_expert, n_valid, wg, wu, wd):
    p, d = xs.shape
    tm = EXPERT_TILE
    ff = wg.shape[2]
    grid_spec = pltpu.PrefetchScalarGridSpec(
        num_scalar_prefetch=2,
        grid=(p // tm,),
        in_specs=[pl.BlockSpec((tm, d), lambda t, te, nv: (t, 0)),
                  pl.BlockSpec((1, d, ff), lambda t, te, nv: (te[t], 0, 0)),
                  pl.BlockSpec((1, d, ff), lambda t, te, nv: (te[t], 0, 0)),
                  pl.BlockSpec((1, ff, d), lambda t, te, nv: (te[t], 0, 0))],
        out_specs=pl.BlockSpec((tm, d), lambda t, te, nv: (t, 0)),
        scratch_shapes=[pltpu.VMEM((d, ff), BF16), pltpu.VMEM((d, ff), BF16), pltpu.VMEM((ff, d), BF16)],
    )
    return pl.pallas_call(
        _expert_kernel,
        out_shape=jax.ShapeDtypeStruct((p, d), BF16),
        grid_spec=grid_spec,
        compiler_params=_cparams(("arbitrary",)),
        name="experts",
    )(tile_expert, n_valid, xs, wg, wu, wd)


def _combine_kernel(h_ref, y1_ref, y2_ref, route_ref, g_ref, o_ref, *, final):
    route = route_ref[...]
    out = h_ref[...] + route[:, 2:3] * y1_ref[...].astype(F32) + route[:, 3:4] * y2_ref[...].astype(F32)
    if final:
        out = _rms(out, g_ref[...])
    o_ref[...] = out


def _combine(h, y1, y2, route, g, final):
    n, d = h.shape
    tm = ROW_TILE
    row = lambda i: (i, 0)
    return pl.pallas_call(
        functools.partial(_combine_kernel, final=final),
        out_shape=jax.ShapeDtypeStruct((n, d), F32),
        grid=(n // tm,),
        in_specs=[pl.BlockSpec((tm, d), row), pl.BlockSpec((tm, d), row), pl.BlockSpec((tm, d), row),
                  pl.BlockSpec((tm, LANES), row), pl.BlockSpec(g.shape, lambda i: (0, 0))],
        out_specs=pl.BlockSpec((tm, d), row),
        compiler_params=_cparams(("parallel",)),
        name="moe_combine",
    )(h, y1, y2, route, g)


def _moe(h, hf, route, counts, wg, wu, wd, g_final, final):
    n, d = h.shape
    tm = EXPERT_TILE
    ids = jnp.arange(MOE_EXPERTS, dtype=jnp.int32)
    counts = counts[0, MOE_GROUPS:MOE_GROUPS + MOE_EXPERTS].astype(jnp.int32)
    padded = ((counts + tm - 1) // tm) * tm
    ends = jnp.cumsum(padded)
    starts = ends - padded
    e = route[:, 0:2].astype(jnp.int32)
    rank = route[:, 4:6].astype(jnp.int32)
    dest = jnp.sum(jnp.where(e[:, :, None] == ids[None, None, :], starts[None, None, :], 0), axis=-1) + rank
    n_rows = 2 * n + MOE_EXPERTS * tm
    tok = jnp.arange(2 * n, dtype=jnp.int32) // 2
    tok_for_row = jnp.zeros((n_rows,), jnp.int32).at[dest.reshape(-1)].set(tok)
    tile_start = jnp.arange(n_rows // tm, dtype=jnp.int32) * tm
    tile_expert = jnp.minimum(jnp.sum((ends[None, :] <= tile_start[:, None]).astype(jnp.int32), axis=1),
                              MOE_EXPERTS - 1)
    n_valid = (ends[-1] // tm).astype(jnp.int32).reshape(1)
    xs = jnp.take(hf, tok_for_row, axis=0)
    ys = _experts(xs, tile_expert, n_valid, wg, wu, wd)
    y1 = jnp.take(ys, dest[:, 0], axis=0)
    y2 = jnp.take(ys, dest[:, 1], axis=0)
    return _combine(h, y1, y2, route, g_final, final)


def _od_in_kernel(x_ref, g_ref, w_ref, ur_ref, uh_ref):
    hn = _rms(x_ref[...], g_ref[...]).astype(BF16)
    u = _dot(hn, w_ref[...])
    ur_ref[...] = u[:, :RWKV_IN]
    uh_ref[...] = u[:, RWKV_IN:]


def _od_in(x2, g, w):
    n, d = x2.shape
    tm = ROW_TILE
    row = lambda i: (i, 0)
    return pl.pallas_call(
        _od_in_kernel,
        out_shape=[jax.ShapeDtypeStruct((n, RWKV_IN), F32), jax.ShapeDtypeStruct((n, HGRN_IN), F32)],
        grid=(n // tm,),
        in_specs=[pl.BlockSpec((tm, d), row), pl.BlockSpec(g.shape, lambda i: (0, 0)),
                  pl.BlockSpec(w.shape, lambda i: (0, 0))],
        out_specs=[pl.BlockSpec((tm, RWKV_IN), row), pl.BlockSpec((tm, HGRN_IN), row)],
        compiler_params=_cparams(("parallel",)),
        name="od_in",
    )(x2, g, w)


def _rwkv_kernel(u_ref, mu_ref, w0_ref, w2_ref, a0_ref, a2_ref, g2_ref, kk_ref, ka_ref, rk_ref,
                 lnw_ref, lnb_ref, tri_ref, seg_ref, o_ref, prev_ref, ht_ref):
    c = pl.program_id(1)

    @pl.when(c == 0)
    def _():
        prev_ref[...] = jnp.zeros(prev_ref.shape, F32)
        ht_ref[...] = jnp.zeros(ht_ref.shape, F32)

    u = u_ref[0]
    rows = lax.broadcasted_iota(jnp.int32, u.shape, 0)
    u_prev = jnp.where(rows == 0, prev_ref[...], pltpu.roll(u, 1, 0))
    prev_ref[...] = u[CHUNK - 1:CHUNK, :]
    us = u + (u_prev - u) * mu_ref[...]
    w = RWKV_W
    r = us[:, 0:w]
    k = us[:, w:2 * w]
    v = us[:, 2 * w:3 * w]
    x12 = us[:, 3 * w:3 * w + LANES]
    xg = us[:, 3 * w + LANES:]
    seg_pair = seg_ref[...]

    def seg(x):
        return jnp.concatenate([_split_dot(x[:, p * LANES:(p + 1) * LANES], seg_pair)
                                for p in range(RWKV_HEADS // 2)], axis=1)

    tri = tri_ref[...]

    wl = w0_ref[...] + _dot(jnp.tanh(x12).astype(BF16), w2_ref[...])
    nwl = -wl
    softplus = jnp.maximum(nwl, 0.0) + jnp.log1p(jnp.exp(-jnp.abs(nwl)))
    lw = -jnp.exp(-softplus - 0.5)
    a = _sigmoid(a0_ref[...] + _dot(x12.astype(BF16), a2_ref[...]))
    g = _dot(_sigmoid(xg).astype(BF16), g2_ref[...])
    kk = k * kk_ref[...]
    kk = kk * lax.rsqrt(jnp.maximum(seg(kk * kk), 1e-24))
    k2 = k * (1.0 + (a - 1.0) * ka_ref[...])
    a_in = -kk
    b_in = kk * a

    lg = _split_dot(lw, tri, left=True)
    lg_c = lg[CHUNK - 1:CHUNK, :]
    e_neg = jnp.exp(-lg)
    e_rel = jnp.exp(lg_c - lg)
    g_c = jnp.exp(lg_c)
    at = a_in * jnp.exp(lg - lw)
    rt = r * jnp.exp(lg)
    kt = k2 * e_neg
    bt = b_in * e_neg
    kh = k2 * e_rel
    bh = b_in * e_rel

    c2 = 2 * CHUNK
    ri = lax.broadcasted_iota(jnp.int32, (c2, c2), 0)
    ci = lax.broadcasted_iota(jnp.int32, (c2, c2), 1)
    sh = int(math.log2(CHUNK))
    same = jnp.right_shift(ri, sh) == jnp.right_shift(ci, sh)
    strict = same & (ri > ci)
    incl = same & (ri >= ci)
    eye = jnp.where(ri == ci, 1.0, 0.0).astype(F32)
    lane = lax.broadcasted_iota(jnp.int32, (CHUNK, LANES), 1)
    lo_half = lane < RWKV_HD

    def bd(x):
        return jnp.concatenate([jnp.where(lo_half, x, 0.0), jnp.where(lo_half, 0.0, x)], axis=0)

    pairs = range(RWKV_HEADS // 2)
    sls = [slice(p * LANES, (p + 1) * LANES) for p in pairs]
    bdb = lambda x: [bd(x[:, sl]).astype(BF16) for sl in sls]
    at_b = [bd(at[:, sl]) for sl in sls]
    rt_b = [bd(rt[:, sl]) for sl in sls]
    kt_b, bt_b, kh_b, bh_b, v_b = bdb(kt), bdb(bt), bdb(kh), bdb(bh), bdb(v)
    hts = [ht_ref[p] for p in pairs]
    htb = [t.astype(BF16) for t in hts]
    sc = [_dot_nt(jnp.concatenate([at_b[p], rt_b[p]], axis=0).astype(BF16),
                  jnp.concatenate([kt_b[p], bt_b[p]], axis=0)) for p in pairs]
    a_ak = [jnp.where(strict, s[:c2, :c2], 0.0).astype(BF16) for s in sc]
    a_ab = [jnp.where(strict, s[:c2, c2:], 0.0) for s in sc]
    a_rk = [jnp.where(incl, s[c2:, :c2], 0.0).astype(BF16) for s in sc]
    a_rb = [jnp.where(incl, s[c2:, c2:], 0.0).astype(BF16) for s in sc]
    x2 = [_dot(a_ak[p], v_b[p]) for p in pairs]
    e1 = [_dot(a_rk[p], v_b[p]) for p in pairs]
    d1 = [_dot_tn(v_b[p], kh_b[p]) for p in pairs]
    pw = a_ab
    t_inv = [eye + a for a in a_ab]
    for _ in range(int(math.log2(CHUNK)) - 1):
        pwb = [x.astype(BF16) for x in pw]
        pw = [_dot(x, x) for x in pwb]
        t_inv = [t_inv[p] + _dot(t_inv[p].astype(BF16), pw[p].astype(BF16)) for p in pairs]
    txb = [_dot(t_inv[p].astype(BF16), jnp.concatenate([at_b[p], x2[p]], axis=1).astype(BF16)).astype(BF16)
           for p in pairs]
    qe = [_dot(a_rb[p], txb[p]) for p in pairs]
    gd = [_dot_tn(txb[p], bh_b[p]) for p in pairs]
    y_b = [_dot_nt((rt_b[p] + qe[p][:, :LANES]).astype(BF16), htb[p]) + e1[p] + qe[p][:, LANES:] for p in pairs]
    for p in pairs:
        upd = hts[p] * g_c[:, sls[p]] + _dot(htb[p], gd[p][:LANES].astype(BF16)) + d1[p] + gd[p][LANES:]
        ht_ref[p] = jnp.where(same, upd, 0.0)
    y = jnp.concatenate([t[:CHUNK] + t[CHUNK:] for t in y_b], axis=1)

    inv_n = 1.0 / RWKV_HD
    mean = seg(y) * inv_n
    dlt = y - mean
    var = seg(dlt * dlt) * inv_n
    yn = dlt * lax.rsqrt(var + RWKV_GN_EPS) * lnw_ref[...] + lnb_ref[...]
    bonus = seg(r * k2 * rk_ref[...]) * v
    o_ref[0] = ((yn + bonus) * g).astype(BF16)


def _rwkv(u, mu, w0, w2p, a0, a2p, g2, k_k, k_a, r_k, ln_w, ln_b, tri, seg):
    b, s, _ = u.shape
    full = lambda bi, ci: (0, 0)
    wspec = lambda t: pl.BlockSpec(t.shape, full)
    params = [mu, w0, w2p, a0, a2p, g2, k_k, k_a, r_k, ln_w, ln_b, tri, seg]
    return pl.pallas_call(
        _rwkv_kernel,
        out_shape=jax.ShapeDtypeStruct((b, s, RWKV_W), BF16),
        grid=(b, s // CHUNK),
        in_specs=[pl.BlockSpec((1, CHUNK, RWKV_IN), lambda bi, ci: (bi, ci, 0))] + [wspec(t) for t in params],
        out_specs=pl.BlockSpec((1, CHUNK, RWKV_W), lambda bi, ci: (bi, ci, 0)),
        scratch_shapes=[pltpu.VMEM((1, RWKV_IN), F32), pltpu.VMEM((RWKV_HEADS // 2, LANES, LANES), F32)],
        compiler_params=_cparams(("parallel", "arbitrary")),
        name="rwkv7",
    )(u, *params)


def _hgrn_kernel(u_ref, lb_ref, on_ref, tri_ref, ones_ref, sel_ref, o_ref, st_ref):
    c = pl.program_id(1)

    @pl.when(c == 0)
    def _():
        st_ref[...] = jnp.zeros(st_ref.shape, F32)

    u = u_ref[0]
    lbp = lb_ref[...]
    mx = jnp.maximum(lbp[0:1], lbp[1:2])
    e0 = jnp.exp(lbp[0:1] - mx)
    e1 = jnp.exp(lbp[1:2] - mx)
    p0 = e0 / (e0 + e1)
    p1 = e1 / (e0 + e1)
    lb = (p0 + p1) - p0
    tri = tri_ref[...]
    ones = ones_ref[...]
    sel = sel_ref[...]
    d = HGRN_D
    w = HGRN_W
    heads = range(HGRN_HEADS)
    sls = [slice(h * d, (h + 1) * d) for h in heads]
    q, z, iv, gt = u[:, :w], u[:, w:2 * w], u[:, 2 * w:3 * w], u[:, 3 * w:]
    qs = q * _sigmoid(q)
    log_sig = jnp.minimum(z, 0.0) - jnp.log1p(jnp.exp(-jnp.abs(z)))
    x1 = jnp.log(lb)
    x2 = jnp.log1p(-lb) + log_sig
    log_f = jnp.maximum(x1, x2) + jnp.log1p(jnp.exp(-jnp.abs(x1 - x2)))
    key = (1.0 - lb) * _sigmoid(-z)
    bc = _split_dot(log_f, tri, left=True)
    b_l = bc[CHUNK - 1:CHUNK]
    ivb = iv.astype(BF16)
    qe = (qs * jnp.exp(bc)).astype(BF16)
    kl = (key * jnp.exp(b_l - bc)).astype(BF16)
    sts = [st_ref[h] for h in heads]
    o_inter = [_dot_nt(qe[:, sls[h]], sts[h].astype(BF16)) for h in heads]
    dec = jnp.exp(b_l)
    for h in heads:
        st_ref[h] = sts[h] * dec[:, sls[h]] + _dot_tn(ivb[:, sls[h]], kl[:, sls[h]])

    nblk = CHUNK // SUB
    srow = lax.broadcasted_iota(jnp.int32, (SUB, w), 0)
    p_all, v_tile, sc_off = [], [], []
    for blk in range(nblk):
        r0 = blk * SUB
        b_i, q_i, k_i = bc[r0:r0 + SUB], qs[r0:r0 + SUB], key[r0:r0 + SUB]
        rows = [q_i[t:t + 1] * k_i * jnp.where(srow <= t, jnp.exp(b_i[t:t + 1] - b_i), 0.0) for t in range(SUB)]
        p_all.append(jnp.concatenate(rows, axis=0).astype(BF16))
        v_tile.append(jnp.concatenate([iv[r0:r0 + SUB]] * SUB, axis=0))
        if blk > 0:
            b_m = bc[r0 - 1:r0]
            qp = (q_i * jnp.exp(b_i - b_m)).astype(BF16)
            kp = (key[:r0] * jnp.exp(b_m - bc[:r0])).astype(BF16)
            sc_off.append([_dot_nt(qp[:, sl], kp[:, sl]).astype(BF16) for sl in sls])
    rs = [jnp.concatenate([_dot(p_all[blk][:, sl], ones) for sl in sls], axis=1) for blk in range(nblk)]
    rv = [(rs[blk] * v_tile[blk]).astype(BF16) for blk in range(nblk)]
    for h in heads:
        sl = sls[h]
        parts = []
        for blk in range(nblk):
            o_i = _dot(sel, rv[blk][:, sl])
            if blk > 0:
                o_i = o_i + _dot(sc_off[blk - 1][h], ivb[:blk * SUB, sl])
            parts.append(o_i)
        o = o_inter[h] + jnp.concatenate(parts, axis=0)
        on = _rms(o, on_ref[:, sl])
        g_h = gt[:, sl]
        o_ref[0, :, sl] = (on * (g_h * _sigmoid(g_h))).astype(BF16)


def _hgrn(u, lbp, o_norm, tri, ones, sel):
    b, s, _ = u.shape
    full = lambda bi, ci: (0, 0)
    wspec = lambda t: pl.BlockSpec(t.shape, full)
    return pl.pallas_call(
        _hgrn_kernel,
        out_shape=jax.ShapeDtypeStruct((b, s, HGRN_W), BF16),
        grid=(b, s // CHUNK),
        in_specs=[pl.BlockSpec((1, CHUNK, HGRN_IN), lambda bi, ci: (bi, ci, 0)), wspec(lbp), wspec(o_norm),
                  wspec(tri), wspec(ones), wspec(sel)],
        out_specs=pl.BlockSpec((1, CHUNK, HGRN_W), lambda bi, ci: (bi, ci, 0)),
        scratch_shapes=[pltpu.VMEM((HGRN_HEADS, HGRN_D, HGRN_D), F32)],
        compiler_params=_cparams(("parallel", "arbitrary")),
        name="hgrn2",
    )(u, lbp, o_norm, tri, ones, sel)


def _rope_tables(seq):
    pos = jnp.arange(seq, dtype=F32)[:, None]

    def cs(half):
        inv = jnp.power(ROPE_THETA, -jnp.arange(half, dtype=F32) / half)
        ang = pos * inv[None, :]
        return jnp.cos(ang), jnp.sin(ang)

    cd, sd = cs(DIFF_HD // 2)
    cos_d = jnp.tile(jnp.concatenate([cd, cd], axis=1), (1, LANES // DIFF_HD))
    sin_d = jnp.tile(jnp.concatenate([-sd, sd], axis=1), (1, LANES // DIFF_HD))
    cm, sm = cs(MLA_ROPE // 2)
    one = jnp.ones((seq, MLA_NOPE), F32)
    tail = LANES - MLA_QK
    cos_m = jnp.concatenate([one, cm, cm, jnp.ones((seq, tail), F32)], axis=1)
    sin_m = jnp.concatenate([0 * one, -sm, sm, jnp.zeros((seq, tail), F32)], axis=1)
    return cos_d, sin_d, cos_m, sin_m


def _router_weights(w_group, b_group, w_expert, b_expert):
    d = w_group.shape[0]
    pad = LANES - MOE_GROUPS - MOE_EXPERTS
    w = jnp.concatenate([w_group, w_expert, jnp.zeros((d, pad), F32)], axis=1)
    bias = jnp.concatenate([b_group, b_expert, jnp.zeros((pad,), F32)])[None, :]
    hi = w.astype(BF16)
    lo = (w - hi.astype(F32)).astype(BF16)
    return hi, lo, bias


def kernel(x, norm_mix, norm_ffn, norm_final, ev_w_in, ev_w_out, mla_q_norm, mla_w_q_up, mla_kv_norm, mla_w_kv_up, diff_lambda, diff_subln, od_w_in, od_w_out, rwkv_mu, rwkv_w0, rwkv_w2, rwkv_a0, rwkv_a2, rwkv_g2, rwkv_k_k, rwkv_k_a, rwkv_r_k, rwkv_ln_w, rwkv_ln_b, hgrn_lb, hgrn_o_norm, moe_w_group, moe_b_group, moe_w_expert, moe_b_expert, moe_w_gate, moe_w_up, moe_w_down):
    b, s, d = x.shape
    n = b * s
    assert norm_mix.shape[0] == 2 and hgrn_lb.shape[0] == 2
    assert s % ATTN_BLOCK == 0 and s % ROW_TILE == 0 and ATTN_BLOCK % CHUNK == 0
    row2 = lambda t: t.reshape(1, -1)
    h = x.reshape(n, d)

    w_in = ev_w_in[0]
    o1, o2, o3 = MLA_LORA, 2 * MLA_LORA, 2 * MLA_LORA + MLA_ROPE
    kr_pad = jnp.zeros((d, LANES), F32).at[:, MLA_NOPE:MLA_QK].set(w_in[:, o2:o3])
    w0 = jnp.concatenate([w_in[:, :o2], kr_pad, w_in[:, o3:]], axis=1).astype(BF16)
    wq = mla_w_q_up[0].reshape(MLA_LORA, MLA_HEADS, MLA_QK)
    wq = jnp.pad(wq, ((0, 0), (0, 0), (0, LANES - MLA_QK))).reshape(MLA_LORA, MLA_HEADS * LANES).astype(BF16)
    wkv = mla_w_kv_up[0].reshape(MLA_LORA, MLA_HEADS, MLA_NOPE + MLA_V)
    wk = jnp.pad(wkv[:, :, :MLA_NOPE], ((0, 0), (0, 0), (0, LANES - MLA_NOPE)))
    wk = wk.reshape(MLA_LORA, MLA_HEADS * LANES).astype(BF16)
    wv = wkv[:, :, MLA_NOPE:].reshape(MLA_LORA, MLA_HEADS * MLA_V).astype(BF16)
    cos_d, sin_d, cos_m, sin_m = _rope_tables(s)
    qf, kf, vm, dq, dk, dv = _ev_in(h, row2(norm_mix[0]), w0, row2(mla_q_norm[0]), row2(mla_kv_norm[0]),
                                    wq, wk, wv, cos_d, sin_d, cos_m, sin_m, s)
    r3 = lambda t: t.reshape(b, s, t.shape[-1])
    o_mla = _mla_attn(r3(qf), r3(kf), vm)
    lam_init = 0.8 - 0.6 * math.exp(-0.3 * 0)
    o_diff = _diff_attn(r3(dq), r3(dk), dv, diff_lambda[0], row2(diff_subln[0]), lam_init)
    rwh, rwl, rb = _router_weights(moe_w_group[0], moe_b_group[0], moe_w_expert[0], moe_b_expert[0])
    ti = jnp.arange(ROW_TILE)
    tri_rows = (ti[None, :] < ti[:, None]).astype(BF16)
    h, hf, route, counts = _mix_out(h, o_mla.reshape(n, -1), o_diff.reshape(n, -1), ev_w_out[0].astype(BF16),
                                    row2(norm_ffn[0]), rwh, rwl, rb, tri_rows)
    h = _moe(h, hf, route, counts, moe_w_gate[0], moe_w_up[0], moe_w_down[0], row2(norm_final), False)

    ur, uh = _od_in(h, row2(norm_mix[1]), od_w_in[0].astype(BF16))
    zpad = jnp.zeros((RWKV_DECAY_LORA, RWKV_W), F32)
    w2p = jnp.concatenate([rwkv_w2[0], zpad], axis=0).astype(BF16)
    a2p = jnp.concatenate([zpad, rwkv_a2[0]], axis=0).astype(BF16)
    ci = jnp.arange(CHUNK)
    tri = (ci[None, :] <= ci[:, None]).astype(BF16)
    li = jnp.arange(LANES) // RWKV_HD
    seg = (li[:, None] == li[None, :]).astype(BF16)
    o_c = _rwkv(ur.reshape(b, s, RWKV_IN), row2(rwkv_mu[0]), row2(rwkv_w0[0]), w2p, row2(rwkv_a0[0]), a2p,
                rwkv_g2[0].astype(BF16), row2(rwkv_k_k[0]), row2(rwkv_k_a[0]), row2(rwkv_r_k[0]),
                row2(rwkv_ln_w[0]), row2(rwkv_ln_b[0]), tri, seg)
    pt = jnp.arange(SUB * SUB) // SUB
    ps = jnp.arange(SUB * SUB) % SUB
    sel = ((pt[None, :] == jnp.arange(SUB)[:, None]) & (ps <= pt)[None, :]).astype(BF16)
    o_d = _hgrn(uh.reshape(b, s, HGRN_IN), hgrn_lb, row2(hgrn_o_norm[0]), tri, jnp.ones((LANES, LANES), BF16), sel)
    rwh, rwl, rb = _router_weights(moe_w_group[1], moe_b_group[1], moe_w_expert[1], moe_b_expert[1])
    h, hf, route, counts = _mix_out(h, o_c.reshape(n, -1), o_d.reshape(n, -1), od_w_out[0].astype(BF16),
                                    row2(norm_ffn[1]), rwh, rwl, rb, tri_rows)
    out = _moe(h, hf, route, counts, moe_w_gate[1], moe_w_up[1], moe_w_down[1], row2(norm_final), True)
    return out.reshape(b, s, d)
```

```python
import functools
import math

import jax
import jax.numpy as jnp
from jax import lax
from jax.experimental import pallas as pl
from jax.experimental.pallas import tpu as pltpu

F32 = jnp.float32
BF16 = jnp.bfloat16

CHUNK = 64
ROPE_THETA = 10000.0
NORM_EPS = 1e-6
MLA_HEADS = 8
MLA_LORA = 256
MLA_NOPE = 64
MLA_ROPE = 32
MLA_V = 64
MLA_QK = MLA_NOPE + MLA_ROPE
DIFF_HEADS = 4
DIFF_HD = 64
DIFF_V = 2 * DIFF_HD
DIFF_W = DIFF_HEADS * 2 * DIFF_HD
RWKV_HEADS = 8
RWKV_HD = 64
RWKV_W = RWKV_HEADS * RWKV_HD
RWKV_DECAY_LORA = 64
RWKV_A_LORA = 64
RWKV_GATE_LORA = 128
RWKV_IN = 3 * RWKV_W + RWKV_DECAY_LORA + RWKV_A_LORA + RWKV_GATE_LORA
RWKV_GN_EPS = 64e-5
HGRN_HEADS = 4
HGRN_D = 128
HGRN_W = HGRN_HEADS * HGRN_D
HGRN_IN = 4 * HGRN_W
MOE_GROUPS = 4
MOE_EPG = 8
MOE_EXPERTS = MOE_GROUPS * MOE_EPG
MOE_FF = 512

LANES = 128
ROW_TILE = 256
ATTN_BLOCK = 512
EXPERT_TILE = 256
SUB = 16
RWKV_SEQS = 4
VMEM_LIMIT = 48 * 1024 * 1024
NEG = -1e30
LOG2E = math.log2(math.e)


def _cparams(sem):
    return pltpu.CompilerParams(dimension_semantics=sem, vmem_limit_bytes=VMEM_LIMIT)


def _dot(a, b):
    return jnp.dot(a, b, preferred_element_type=F32)


def _dot_nt(a, b):
    return lax.dot_general(a, b, (((1,), (1,)), ((), ())), preferred_element_type=F32)


def _dot_tn(a, b):
    return lax.dot_general(a, b, (((0,), (0,)), ((), ())), preferred_element_type=F32)


def _rms(x, g):
    return x * lax.rsqrt(jnp.mean(x * x, axis=-1, keepdims=True) + NORM_EPS) * g


def _sigmoid(x):
    return 1.0 / (1.0 + jnp.exp(-x))


def _split_dot(x, w_bf16, left=False):
    hi = x.astype(BF16)
    lo = (x - hi.astype(F32)).astype(BF16)
    if left:
        return _dot(w_bf16, hi) + _dot(w_bf16, lo)
    return _dot(hi, w_bf16) + _dot(lo, w_bf16)


def _tile_lanes(t, n):
    return jnp.concatenate([t] * n, axis=-1) if n > 1 else t


def _rope_rot(x, half):
    w = x.shape[-1]
    lane = lax.broadcasted_iota(jnp.int32, x.shape, x.ndim - 1)
    first = (lane & (2 * half - 1)) < half
    return jnp.where(first, pltpu.roll(x, w - half, x.ndim - 1), pltpu.roll(x, half, x.ndim - 1))


def _ev_in_kernel(x_ref, g_ref, w0_ref, qg_ref, kvg_ref, wq_ref, wk_ref, wv_ref,
                  cd_ref, sd_ref, cm_ref, sm_ref,
                  qf_ref, kf_ref, vm_ref, dq_ref, dk_ref, dv_ref):
    hn = _rms(x_ref[...], g_ref[...]).astype(BF16)
    u = _dot(hn, w0_ref[...])
    c_q = u[:, 0:256]
    c_kv = u[:, 256:512]
    kr = u[:, 512:640]
    dq = u[:, 640:1152]
    dk = u[:, 1152:1664]
    dv = u[:, 1664:2176]
    cqn = _rms(c_q, qg_ref[...]).astype(BF16)
    ckn = _rms(c_kv, kvg_ref[...]).astype(BF16)
    q = _dot(cqn, wq_ref[...])
    k = _dot(ckn, wk_ref[...])
    v = _dot(ckn, wv_ref[...])
    cm = cm_ref[...]
    sm = sm_ref[...]
    m_half = MLA_ROPE // 2
    lane = lax.broadcasted_iota(jnp.int32, q.shape, 1) & (LANES - 1)
    first = (lane >= MLA_NOPE) & (lane < MLA_NOPE + m_half)
    rot_q = jnp.where(first, pltpu.roll(q, q.shape[1] - m_half, 1), pltpu.roll(q, m_half, 1))
    q = (q * _tile_lanes(cm, MLA_HEADS) + rot_q * _tile_lanes(sm, MLA_HEADS)) * (MLA_QK ** -0.5 * LOG2E)
    lane1 = lax.broadcasted_iota(jnp.int32, kr.shape, 1)
    first1 = (lane1 >= MLA_NOPE) & (lane1 < MLA_NOPE + m_half)
    rot_k = jnp.where(first1, pltpu.roll(kr, LANES - m_half, 1), pltpu.roll(kr, m_half, 1))
    kr = kr * cm + rot_k * sm
    k = k + _tile_lanes(kr, MLA_HEADS)
    qf_ref[...] = q.astype(BF16)
    kf_ref[...] = k.astype(BF16)
    vm_ref[0, :, 0] = v.T.reshape(vm_ref.shape[1], LANES, v.shape[0]).astype(BF16)
    cd = _tile_lanes(cd_ref[...], DIFF_W // LANES)
    sd = _tile_lanes(sd_ref[...], DIFF_W // LANES)
    dq = (dq * cd + _rope_rot(dq, DIFF_HD // 2) * sd) * (DIFF_HD ** -0.5 * LOG2E)
    dk = dk * cd + _rope_rot(dk, DIFF_HD // 2) * sd
    dq_ref[...] = dq.astype(BF16)
    dk_ref[...] = dk.astype(BF16)
    dv_ref[0, :, 0] = dv.T.reshape(dv_ref.shape[1], LANES, dv.shape[0]).astype(BF16)


def _ev_in(x2, g, w0, qg, kvg, wq, wk, wv, cd, sd, cm, sm, seq):
    n, d = x2.shape
    tm = ATTN_BLOCK
    nt = seq // tm
    row = lambda i: (i, 0)
    full = lambda i: (0, 0)
    tab = lambda i: (i % nt, 0)
    wspec = lambda a: pl.BlockSpec(a.shape, full)
    vt_shape = lambda w: jax.ShapeDtypeStruct((n // seq, w // LANES, nt, LANES, tm), BF16)
    vt_spec = lambda w: pl.BlockSpec((1, w // LANES, 1, LANES, tm), lambda i: (i // nt, 0, i % nt, 0, 0))
    rows = lambda w: jax.ShapeDtypeStruct((n, w), BF16)
    rspec = lambda w: pl.BlockSpec((tm, w), row)
    wide = MLA_HEADS * LANES
    return pl.pallas_call(
        _ev_in_kernel,
        out_shape=[rows(wide), rows(wide), vt_shape(MLA_HEADS * MLA_V), rows(DIFF_W), rows(DIFF_W), vt_shape(DIFF_W)],
        grid=(n // tm,),
        in_specs=[pl.BlockSpec((tm, d), row), wspec(g), wspec(w0), wspec(qg), wspec(kvg), wspec(wq), wspec(wk),
                  wspec(wv)] + [pl.BlockSpec((tm, LANES), tab)] * 4,
        out_specs=[rspec(wide), rspec(wide), vt_spec(MLA_HEADS * MLA_V), rspec(DIFF_W), rspec(DIFF_W),
                   vt_spec(DIFF_W)],
        compiler_params=_cparams(("parallel",)),
        name="ev_in",
    )(x2, g, w0, qg, kvg, wq, wk, wv, cd, sd, cm, sm)


ONES_ROWS = 16


def _chunk_mask_t(shape):
    sh = int(math.log2(CHUNK))
    key_chunk = jnp.right_shift(lax.broadcasted_iota(jnp.int32, shape, 0), sh)
    q_chunk = jnp.right_shift(lax.broadcasted_iota(jnp.int32, shape, 1), sh)
    return key_chunk <= q_chunk


def _attn_streams(i, qs, key_of, val_of, k_ref, vt_ref, m_ref, acc_ref):
    bk = ATTN_BLOCK
    n = len(qs)
    m_ref[...] = jnp.full(m_ref.shape, NEG, F32)
    acc_ref[...] = jnp.zeros(acc_ref.shape, F32)
    dv = acc_ref.shape[1] - ONES_ROWS
    ones = jnp.ones((ONES_ROWS, bk), BF16)

    def step(j, masked):
        kb = k_ref[0, pl.ds(pl.multiple_of(j * bk, bk), bk), :]
        vt = vt_ref[0, :, j]
        st = [_dot_nt(key_of(kb, s), qs[s]) for s in range(n)]
        if masked:
            mask = _chunk_mask_t(st[0].shape)
            st = [jnp.where(mask, x, NEG) for x in st]
        m_prev = [m_ref[s] for s in range(n)]
        m_new = [jnp.maximum(m_prev[s], jnp.max(st[s], axis=0, keepdims=True)) for s in range(n)]
        p = [jnp.exp2(st[s] - m_new[s]).astype(BF16) for s in range(n)]
        pv = [_dot(jnp.concatenate([val_of(vt, s), ones], axis=0), p[s]) for s in range(n)]
        for s in range(n):
            acc_ref[s] = jnp.exp2(m_prev[s] - m_new[s]) * acc_ref[s] + pv[s]
            m_ref[s] = m_new[s]

    def body(j, carry):
        step(j, False)
        return carry

    lax.fori_loop(0, i, body, 0)
    step(i, True)
    accs = [acc_ref[s] for s in range(n)]
    return [a[:dv] / a[dv:dv + 1] for a in accs]


def _mla_attn_kernel(q_ref, k_ref, vt_ref, o_ref, m_ref, acc_ref):
    q = q_ref[0]
    n = q.shape[1] // LANES
    qs = [q[:, s * LANES:(s + 1) * LANES] for s in range(n)]
    key_of = lambda kb, s: kb[:, s * LANES:(s + 1) * LANES]
    val_of = lambda vt, s: vt[s // 2, (s % 2) * MLA_V:(s % 2 + 1) * MLA_V]
    outs = _attn_streams(pl.program_id(2), qs, key_of, val_of, k_ref, vt_ref, m_ref, acc_ref)
    o_ref[0] = jnp.concatenate(outs, axis=0).T.astype(BF16)


def _diff_attn_kernel(q_ref, k_ref, vt_ref, lam_ref, sub_ref, o_ref, m_ref, acc_ref, *, lam_init):
    q = q_ref[0]
    nh = q.shape[1] // LANES
    lane = lax.broadcasted_iota(jnp.int32, (q.shape[0], LANES), 1)
    qs = []
    for h in range(nh):
        q_h = q[:, h * LANES:(h + 1) * LANES]
        zero = jnp.zeros_like(q_h)
        qs += [jnp.where(lane < DIFF_HD, q_h, zero), jnp.where(lane >= DIFF_HD, q_h, zero)]
    key_of = lambda kb, s: kb[:, (s // 2) * LANES:(s // 2 + 1) * LANES]
    val_of = lambda vt, s: vt[s // 2]
    outs = _attn_streams(pl.program_id(2), qs, key_of, val_of, k_ref, vt_ref, m_ref, acc_ref)
    lam = lam_ref[...]
    s1 = jnp.sum(lam[0:1] * lam[1:2], axis=-1, keepdims=True)
    s2 = jnp.sum(lam[2:3] * lam[3:4], axis=-1, keepdims=True)
    lam_full = jnp.exp(s1) - jnp.exp(s2) + lam_init
    for h in range(nh):
        o = (outs[2 * h] - lam_full * outs[2 * h + 1]).T
        o_ref[0, :, h * LANES:(h + 1) * LANES] = (_rms(o, sub_ref[...]) * (1.0 - lam_init)).astype(BF16)


ATTN_STREAMS = 4


def _attn_scratch(bq, dv):
    return [pltpu.VMEM((ATTN_STREAMS, 1, bq), F32), pltpu.VMEM((ATTN_STREAMS, dv + ONES_ROWS, bq), F32)]


def _vt_spec(nk, groups):
    return pl.BlockSpec((1, groups, nk, LANES, ATTN_BLOCK), lambda bi, h, i: (bi, h, 0, 0, 0))


def _mla_attn(qf, kf, vt):
    b, s, _ = qf.shape
    bq = ATTN_BLOCK
    w = ATTN_STREAMS * LANES
    return pl.pallas_call(
        _mla_attn_kernel,
        out_shape=jax.ShapeDtypeStruct((b, s, MLA_HEADS * MLA_V), BF16),
        grid=(b, MLA_HEADS // ATTN_STREAMS, s // bq),
        in_specs=[pl.BlockSpec((1, bq, w), lambda bi, h, i: (bi, i, h)),
                  pl.BlockSpec((1, s, w), lambda bi, h, i: (bi, 0, h)),
                  _vt_spec(s // bq, ATTN_STREAMS // 2)],
        out_specs=pl.BlockSpec((1, bq, ATTN_STREAMS * MLA_V), lambda bi, h, i: (bi, i, h)),
        scratch_shapes=_attn_scratch(bq, MLA_V),
        compiler_params=_cparams(("parallel", "parallel", "arbitrary")),
        name="mla_attn",
    )(qf, kf, vt)


def _diff_attn(dq, dk, dvt, lam, subln, lam_init):
    b, s, _ = dq.shape
    bq = ATTN_BLOCK
    nh = ATTN_STREAMS // 2
    blk = lambda rows, im: pl.BlockSpec((1, rows, nh * LANES), im)
    return pl.pallas_call(
        functools.partial(_diff_attn_kernel, lam_init=lam_init),
        out_shape=jax.ShapeDtypeStruct((b, s, DIFF_HEADS * DIFF_V), BF16),
        grid=(b, DIFF_HEADS // nh, s // bq),
        in_specs=[blk(bq, lambda bi, h, i: (bi, i, h)), blk(s, lambda bi, h, i: (bi, 0, h)),
                  _vt_spec(s // bq, nh),
                  pl.BlockSpec(lam.shape, lambda bi, h, i: (0, 0)),
                  pl.BlockSpec(subln.shape, lambda bi, h, i: (0, 0))],
        out_specs=blk(bq, lambda bi, h, i: (bi, i, h)),
        scratch_shapes=_attn_scratch(bq, DIFF_V),
        compiler_params=_cparams(("parallel", "parallel", "arbitrary")),
        name="diff_attn",
    )(dq, dk, dvt, lam, subln)


def _mix_out_kernel(h_ref, a_ref, b_ref, w_ref, g_ref, rwh_ref, rwl_ref, rb_ref, tri_ref,
                    hout_ref, hf_ref, route_ref, idx_ref, cnt_ref):
    mixed = jnp.concatenate([a_ref[...], b_ref[...]], axis=-1)
    h = h_ref[...] + _dot(mixed, w_ref[...])
    hout_ref[...] = h
    hf = _rms(h, g_ref[...])
    hi = hf.astype(BF16)
    hf_ref[...] = hi
    lo = (hf - hi.astype(F32)).astype(BF16)
    z = _dot(hi, rwh_ref[...]) + _dot(lo, rwh_ref[...]) + _dot(hi, rwl_ref[...]) + rb_ref[...]
    lane_i = lax.broadcasted_iota(jnp.int32, z.shape, 1)
    lane = lane_i.astype(F32)
    big = float(LANES)
    is_g = lane_i < MOE_GROUPS
    zg = jnp.where(is_g, z, NEG)
    mg = jnp.max(zg, axis=-1, keepdims=True)
    g_idx = jnp.min(jnp.where(zg == mg, lane, big), axis=-1, keepdims=True)
    pg_top = 1.0 / jnp.sum(jnp.where(is_g, jnp.exp(zg - mg), 0.0), axis=-1, keepdims=True)
    grp_of_lane = jnp.right_shift(lane_i - MOE_GROUPS, int(math.log2(MOE_EPG))).astype(F32)
    in_grp = (lane_i >= MOE_GROUPS) & (lane_i < MOE_GROUPS + MOE_EXPERTS) & (grp_of_lane == g_idx)
    ze = jnp.where(in_grp, z, NEG)
    m1 = jnp.max(ze, axis=-1, keepdims=True)
    i1 = jnp.min(jnp.where(ze == m1, lane, big), axis=-1, keepdims=True)
    ze2 = jnp.where(lane == i1, NEG, ze)
    m2 = jnp.max(ze2, axis=-1, keepdims=True)
    i2 = jnp.min(jnp.where(ze2 == m2, lane, big), axis=-1, keepdims=True)
    r = jnp.exp(m2 - m1)
    w1 = 1.0 / (1.0 + r)
    w2 = r / (1.0 + r)
    e1 = i1 - float(MOE_GROUPS)
    e2 = i2 - float(MOE_GROUPS)
    @pl.when(pl.program_id(0) == 0)
    def _():
        cnt_ref[...] = jnp.zeros(cnt_ref.shape, F32)

    both = jnp.where((lane == i1) | (lane == i2), 1.0, 0.0)
    before = _dot(tri_ref[...], both.astype(BF16)) + cnt_ref[0:1, :]
    rank1 = jnp.sum(jnp.where(lane == i1, before, 0.0), axis=-1, keepdims=True)
    rank2 = jnp.sum(jnp.where(lane == i2, before, 0.0), axis=-1, keepdims=True)
    cnt_ref[...] = cnt_ref[...] + jnp.sum(both, axis=0, keepdims=True)
    vals = (e1, e2, pg_top * w1, pg_top * w2, rank1, rank2)
    out = jnp.zeros(z.shape, F32)
    for col, val in enumerate(vals):
        out = jnp.where(lane_i == col, val, out)
    route_ref[...] = out
    idx_ref[0] = out.T[0:8].astype(jnp.int32)


def _mix_out(h, a, b, w, g, rwh, rwl, rb, tri):
    n, d = h.shape
    tm = ROW_TILE
    row = lambda i: (i, 0)
    full = lambda i: (0, 0)
    wspec = lambda t: pl.BlockSpec(t.shape, full)
    return pl.pallas_call(
        _mix_out_kernel,
        out_shape=[jax.ShapeDtypeStruct((n, d), F32), jax.ShapeDtypeStruct((n, d), BF16),
                   jax.ShapeDtypeStruct((n, LANES), F32), jax.ShapeDtypeStruct((n // tm, 8, tm), jnp.int32),
                   jax.ShapeDtypeStruct((8, LANES), F32)],
        grid=(n // tm,),
        in_specs=[pl.BlockSpec((tm, d), row), pl.BlockSpec((tm, a.shape[1]), row), pl.BlockSpec((tm, b.shape[1]), row),
                  wspec(w), wspec(g), wspec(rwh), wspec(rwl), wspec(rb), wspec(tri)],
        out_specs=[pl.BlockSpec((tm, d), row), pl.BlockSpec((tm, d), row), pl.BlockSpec((tm, LANES), row),
                   pl.BlockSpec((1, 8, tm), lambda i: (i, 0, 0)), pl.BlockSpec((8, LANES), full)],
        compiler_params=_cparams(("arbitrary",)),
        name="mix_out",
    )(h, a, b, w, g, rwh, rwl, rb, tri)


def _expert_kernel(te_ref, nv_ref, x_ref, wg_ref, wu_ref, wd_ref, y_ref, wgb_ref, wub_ref, wdb_ref):
    t = pl.program_id(0)

    @pl.when((t == 0) | (te_ref[t] != te_ref[jnp.maximum(t - 1, 0)]))
    def _():
        wgb_ref[...] = wg_ref[0].astype(BF16)
        wub_ref[...] = wu_ref[0].astype(BF16)
        wdb_ref[...] = wd_ref[0].astype(BF16)

    @pl.when(t < nv_ref[0])
    def _():
        x = x_ref[...]
        a = _dot(x, wgb_ref[...])
        up = _dot(x, wub_ref[...])
        act = (a * _sigmoid(a) * up).astype(BF16)
        y_ref[...] = _dot(act, wdb_ref[...]).astype(BF16)

    @pl.when(t >= nv_ref[0])
    def _():
        y_ref[...] = jnp.zeros(y_ref.shape, BF16)


def _experts(xs, tile_expert, n_valid, wg, wu, wd):
    p, d = xs.shape
    tm = EXPERT_TILE
    ff = wg.shape[2]
    grid_spec = pltpu.PrefetchScalarGridSpec(
        num_scalar_prefetch=2,
        grid=(p // tm,),
        in_specs=[pl.BlockSpec((tm, d), lambda t, te, nv: (t, 0)),
                  pl.BlockSpec((1, d, ff), lambda t, te, nv: (te[t], 0, 0)),
                  pl.BlockSpec((1, d, ff), lambda t, te, nv: (te[t], 0, 0)),
                  pl.BlockSpec((1, ff, d), lambda t, te, nv: (te[t], 0, 0))],
        out_specs=pl.BlockSpec((tm, d), lambda t, te, nv: (t, 0)),
        scratch_shapes=[pltpu.VMEM((d, ff), BF16), pltpu.VMEM((d, ff), BF16), pltpu.VMEM((ff, d), BF16)],
    )
    return pl.pallas_call(
        _expert_kernel,
        out_shape=jax.ShapeDtypeStruct((p, d), BF16),
        grid_spec=grid_spec,
        compiler_params=_cparams(("arbitrary",)),
        name="experts",
    )(tile_expert, n_valid, xs, wg, wu, wd)


def _combine_kernel(h_ref, y1_ref, y2_ref, route_ref, g_ref, o_ref, *, final):
    route = route_ref[...]
    out = h_ref[...] + route[:, 2:3] * y1_ref[...].astype(F32) + route[:, 3:4] * y2_ref[...].astype(F32)
    if final:
        out = _rms(out, g_ref[...])
    o_ref[...] = out


def _combine(h, y1, y2, route, g, final):
    n, d = h.shape
    tm = ROW_TILE
    row = lambda i: (i, 0)
    return pl.pallas_call(
        functools.partial(_combine_kernel, final=final),
        out_shape=jax.ShapeDtypeStruct((n, d), F32),
        grid=(n // tm,),
        in_specs=[pl.BlockSpec((tm, d), row), pl.BlockSpec((tm, d), row), pl.BlockSpec((tm, d), row),
                  pl.BlockSpec((tm, LANES), row), pl.BlockSpec(g.shape, lambda i: (0, 0))],
        out_specs=pl.BlockSpec((tm, d), row),
        compiler_params=_cparams(("parallel",)),
        name="moe_combine",
    )(h, y1, y2, route, g)


def _moe(h, hf, route, idx, counts, wg, wu, wd, g_final, final):
    n, d = h.shape
    tm = EXPERT_TILE
    ids = jnp.arange(MOE_EXPERTS, dtype=jnp.int32)
    counts = counts[0, MOE_GROUPS:MOE_GROUPS + MOE_EXPERTS].astype(jnp.int32)
    padded = ((counts + tm - 1) // tm) * tm
    ends = jnp.cumsum(padded)
    starts = ends - padded
    start_of = lambda e: jnp.sum(jnp.where(e[:, None] == ids[None, :], starts[None, :], 0), axis=-1)
    e1, e2 = idx[:, 0, :].reshape(n), idx[:, 1, :].reshape(n)
    dest1 = start_of(e1) + idx[:, 4, :].reshape(n)
    dest2 = start_of(e2) + idx[:, 5, :].reshape(n)
    n_rows = 2 * n + MOE_EXPERTS * tm
    tok = jnp.arange(n, dtype=jnp.int32)
    tok_for_row = (jnp.arange(n_rows, dtype=jnp.int32) % n).at[jnp.concatenate([dest1, dest2])].set(
        jnp.concatenate([tok, tok]))
    tile_start = jnp.arange(n_rows // tm, dtype=jnp.int32) * tm
    tile_expert = jnp.minimum(jnp.sum((ends[None, :] <= tile_start[:, None]).astype(jnp.int32), axis=1),
                              MOE_EXPERTS - 1)
    n_valid = (ends[-1] // tm).astype(jnp.int32).reshape(1)
    xs = jnp.take(hf, tok_for_row, axis=0)
    ys = _experts(xs, tile_expert, n_valid, wg, wu, wd)
    y1 = jnp.take(ys, dest1, axis=0)
    y2 = jnp.take(ys, dest2, axis=0)
    return _combine(h, y1, y2, route, g_final, final)


def _od_in_kernel(x_ref, g_ref, w_ref, ur_ref, uh_ref):
    hn = _rms(x_ref[...], g_ref[...]).astype(BF16)
    u = _dot(hn, w_ref[...])
    ur_ref[...] = u[:, :RWKV_IN].astype(BF16)
    uh_ref[...] = u[:, RWKV_IN:].astype(BF16)


def _od_in(x2, g, w):
    n, d = x2.shape
    tm = ROW_TILE
    row = lambda i: (i, 0)
    return pl.pallas_call(
        _od_in_kernel,
        out_shape=[jax.ShapeDtypeStruct((n, RWKV_IN), BF16), jax.ShapeDtypeStruct((n, HGRN_IN), BF16)],
        grid=(n // tm,),
        in_specs=[pl.BlockSpec((tm, d), row), pl.BlockSpec(g.shape, lambda i: (0, 0)),
                  pl.BlockSpec(w.shape, lambda i: (0, 0))],
        out_specs=[pl.BlockSpec((tm, RWKV_IN), row), pl.BlockSpec((tm, HGRN_IN), row)],
        compiler_params=_cparams(("parallel",)),
        name="od_in",
    )(x2, g, w)


def _rwkv_kernel(u_ref, mu_ref, w0_ref, w2_ref, a0_ref, a2_ref, g2_ref, kk_ref, ka_ref, rk_ref,
                 lnw_ref, lnb_ref, tri_ref, seg_ref, o_ref, prev_ref, ht_ref):
    c = pl.program_id(1)

    @pl.when(c == 0)
    def _():
        prev_ref[...] = jnp.zeros(prev_ref.shape, F32)
        ht_ref[...] = jnp.zeros(ht_ref.shape, F32)

    nb = u_ref.shape[0]
    u = u_ref[...].reshape(nb * CHUNK, RWKV_IN).astype(F32)
    rows = lax.broadcasted_iota(jnp.int32, u.shape, 0)
    u_prev = pltpu.roll(u, 1, 0)
    for bi in range(nb):
        u_prev = jnp.where(rows == bi * CHUNK, prev_ref[bi], u_prev)
        prev_ref[bi] = u[(bi + 1) * CHUNK - 1:(bi + 1) * CHUNK, :]
    us = u + (u_prev - u) * mu_ref[...]
    w = RWKV_W
    r = us[:, 0:w]
    k = us[:, w:2 * w]
    v = us[:, 2 * w:3 * w]
    x12 = us[:, 3 * w:3 * w + LANES]
    xg = us[:, 3 * w + LANES:]
    seg_pair = seg_ref[...]

    def seg(x):
        return jnp.concatenate([_split_dot(x[:, p * LANES:(p + 1) * LANES], seg_pair)
                                for p in range(RWKV_HEADS // 2)], axis=1)

    tri = tri_ref[...]

    wl = w0_ref[...] + _dot(jnp.tanh(x12).astype(BF16), w2_ref[...])
    nwl = -wl
    softplus = jnp.maximum(nwl, 0.0) + jnp.log1p(jnp.exp(-jnp.abs(nwl)))
    lw = -jnp.exp(-softplus - 0.5)
    a = _sigmoid(a0_ref[...] + _dot(x12.astype(BF16), a2_ref[...]))
    g = _dot(_sigmoid(xg).astype(BF16), g2_ref[...])
    kk = k * kk_ref[...]
    kk = kk * lax.rsqrt(jnp.maximum(seg(kk * kk), 1e-24))
    k2 = k * (1.0 + (a - 1.0) * ka_ref[...])
    a_in = -kk
    b_in = kk * a

    lg = _split_dot(lw, tri, left=True)
    rbs = [slice(bi * CHUNK, (bi + 1) * CHUNK) for bi in range(nb)]
    lg_c = [lg[(bi + 1) * CHUNK - 1:(bi + 1) * CHUNK, :] for bi in range(nb)]
    e_neg = jnp.exp(-lg)
    e_rel = jnp.concatenate([jnp.exp(lg_c[bi] - lg[rbs[bi]]) for bi in range(nb)], axis=0)
    g_c = [jnp.exp(x) for x in lg_c]
    at = a_in * jnp.exp(lg - lw)
    rt = r * jnp.exp(lg)
    kt = k2 * e_neg
    bt = b_in * e_neg
    kh = k2 * e_rel
    bh = b_in * e_rel

    c2 = 2 * CHUNK
    ri = lax.broadcasted_iota(jnp.int32, (c2, c2), 0)
    ci = lax.broadcasted_iota(jnp.int32, (c2, c2), 1)
    sh = int(math.log2(CHUNK))
    same = jnp.right_shift(ri, sh) == jnp.right_shift(ci, sh)
    strict = same & (ri > ci)
    incl = same & (ri >= ci)
    eye = jnp.where(ri == ci, 1.0, 0.0).astype(F32)
    lane = lax.broadcasted_iota(jnp.int32, (CHUNK, LANES), 1)
    lo_half = lane < RWKV_HD

    def bd(x):
        return jnp.concatenate([jnp.where(lo_half, x, 0.0), jnp.where(lo_half, 0.0, x)], axis=0)

    npair = RWKV_HEADS // 2
    pairs = range(nb * npair)
    sls = [(rbs[c // npair], slice((c % npair) * LANES, (c % npair + 1) * LANES)) for c in pairs]
    bdb = lambda x: [bd(x[sl]).astype(BF16) for sl in sls]
    at_b = [bd(at[sl]) for sl in sls]
    rt_b = [bd(rt[sl]) for sl in sls]
    kt_b, bt_b, kh_b, bh_b, v_b = bdb(kt), bdb(bt), bdb(kh), bdb(bh), bdb(v)
    hts = [ht_ref[p] for p in pairs]
    htb = [t.astype(BF16) for t in hts]
    sc = [_dot_nt(jnp.concatenate([at_b[p], rt_b[p]], axis=0).astype(BF16),
                  jnp.concatenate([kt_b[p], bt_b[p]], axis=0)) for p in pairs]
    a_ak = [jnp.where(strict, s[:c2, :c2], 0.0).astype(BF16) for s in sc]
    a_ab = [jnp.where(strict, s[:c2, c2:], 0.0) for s in sc]
    a_rk = [jnp.where(incl, s[c2:, :c2], 0.0).astype(BF16) for s in sc]
    a_rb = [jnp.where(incl, s[c2:, c2:], 0.0).astype(BF16) for s in sc]
    x2 = [_dot(a_ak[p], v_b[p]) for p in pairs]
    e1 = [_dot(a_rk[p], v_b[p]) for p in pairs]
    d1 = [_dot_tn(v_b[p], kh_b[p]) for p in pairs]
    pw = a_ab
    t_inv = [eye + a for a in a_ab]
    for _ in range(int(math.log2(CHUNK)) - 1):
        pwb = [x.astype(BF16) for x in pw]
        pw = [_dot(x, x) for x in pwb]
        t_inv = [t_inv[p] + _dot(t_inv[p].astype(BF16), pw[p].astype(BF16)) for p in pairs]
    txb = [_dot(t_inv[p].astype(BF16), jnp.concatenate([at_b[p], x2[p]], axis=1).astype(BF16)).astype(BF16)
           for p in pairs]
    qe = [_dot(a_rb[p], txb[p]) for p in pairs]
    gd = [_dot_tn(txb[p], bh_b[p]) for p in pairs]
    y_b = [_dot_nt((rt_b[p] + qe[p][:, :LANES]).astype(BF16), htb[p]) + e1[p] + qe[p][:, LANES:] for p in pairs]
    for p in pairs:
        upd = hts[p] * g_c[p // npair][:, sls[p][1]] + _dot(htb[p], gd[p][:LANES].astype(BF16)) + d1[p] + gd[p][LANES:]
        ht_ref[p] = jnp.where(same, upd, 0.0)
    y_p = [t[:CHUNK] + t[CHUNK:] for t in y_b]
    y = jnp.concatenate([jnp.concatenate(y_p[bi * npair:(bi + 1) * npair], axis=1) for bi in range(nb)],
                        axis=0)

    inv_n = 1.0 / RWKV_HD
    mean = seg(y) * inv_n
    dlt = y - mean
    var = seg(dlt * dlt) * inv_n
    yn = dlt * lax.rsqrt(var + RWKV_GN_EPS) * lnw_ref[...] + lnb_ref[...]
    bonus = seg(r * k2 * rk_ref[...]) * v
    o_ref[...] = ((yn + bonus) * g).astype(BF16).reshape(nb, CHUNK, RWKV_W)


def _rwkv(u, mu, w0, w2p, a0, a2p, g2, k_k, k_a, r_k, ln_w, ln_b, tri, seg):
    b, s, _ = u.shape
    nb = tri.shape[0] // CHUNK
    full = lambda bi, ci: (0, 0)
    wspec = lambda t: pl.BlockSpec(t.shape, full)
    params = [mu, w0, w2p, a0, a2p, g2, k_k, k_a, r_k, ln_w, ln_b, tri, seg]
    return pl.pallas_call(
        _rwkv_kernel,
        out_shape=jax.ShapeDtypeStruct((b, s, RWKV_W), BF16),
        grid=(b // nb, s // CHUNK),
        in_specs=[pl.BlockSpec((nb, CHUNK, RWKV_IN), lambda bi, ci: (bi, ci, 0))] + [wspec(t) for t in params],
        out_specs=pl.BlockSpec((nb, CHUNK, RWKV_W), lambda bi, ci: (bi, ci, 0)),
        scratch_shapes=[pltpu.VMEM((nb, 1, RWKV_IN), F32), pltpu.VMEM((nb * RWKV_HEADS // 2, LANES, LANES), F32)],
        compiler_params=_cparams(("parallel", "arbitrary")),
        name="rwkv7",
    )(u, *params)


def _hgrn_kernel(u_ref, lb_ref, on_ref, tri_ref, ones_ref, sel_ref, o_ref, st_ref):
    c = pl.program_id(1)

    @pl.when(c == 0)
    def _():
        st_ref[...] = jnp.zeros(st_ref.shape, F32)

    u = u_ref[0].astype(F32)
    lbp = lb_ref[...]
    mx = jnp.maximum(lbp[0:1], lbp[1:2])
    e0 = jnp.exp(lbp[0:1] - mx)
    e1 = jnp.exp(lbp[1:2] - mx)
    p0 = e0 / (e0 + e1)
    p1 = e1 / (e0 + e1)
    lb = (p0 + p1) - p0
    tri = tri_ref[...]
    ones = ones_ref[...]
    sel = sel_ref[...]
    d = HGRN_D
    w = HGRN_W
    heads = range(HGRN_HEADS)
    sls = [slice(h * d, (h + 1) * d) for h in heads]
    q, z, iv, gt = u[:, :w], u[:, w:2 * w], u[:, 2 * w:3 * w], u[:, 3 * w:]
    qs = q * _sigmoid(q)
    log_sig = jnp.minimum(z, 0.0) - jnp.log1p(jnp.exp(-jnp.abs(z)))
    x1 = jnp.log(lb)
    x2 = jnp.log1p(-lb) + log_sig
    log_f = jnp.maximum(x1, x2) + jnp.log1p(jnp.exp(-jnp.abs(x1 - x2)))
    key = (1.0 - lb) * _sigmoid(-z)
    bc = _split_dot(log_f, tri, left=True)
    b_l = bc[CHUNK - 1:CHUNK]
    ivb = iv.astype(BF16)
    qe = (qs * jnp.exp(bc)).astype(BF16)
    kl = (key * jnp.exp(b_l - bc)).astype(BF16)
    sts = [st_ref[h] for h in heads]
    o_inter = [_dot_nt(qe[:, sls[h]], sts[h].astype(BF16)) for h in heads]
    dec = jnp.exp(b_l)
    for h in heads:
        st_ref[h] = sts[h] * dec[:, sls[h]] + _dot_tn(ivb[:, sls[h]], kl[:, sls[h]])

    nblk = CHUNK // SUB
    srow = lax.broadcasted_iota(jnp.int32, (SUB, w), 0)
    p_all, v_tile, sc_off = [], [], []
    for blk in range(nblk):
        r0 = blk * SUB
        b_i, q_i, k_i = bc[r0:r0 + SUB], qs[r0:r0 + SUB], key[r0:r0 + SUB]
        rows = [q_i[t:t + 1] * k_i * jnp.where(srow <= t, jnp.exp(b_i[t:t + 1] - b_i), 0.0) for t in range(SUB)]
        p_all.append(jnp.concatenate(rows, axis=0).astype(BF16))
        v_tile.append(jnp.concatenate([iv[r0:r0 + SUB]] * SUB, axis=0))
        if blk > 0:
            b_m = bc[r0 - 1:r0]
            qp = (q_i * jnp.exp(b_i - b_m)).astype(BF16)
            kp = (key[:r0] * jnp.exp(b_m - bc[:r0])).astype(BF16)
            sc_off.append([_dot_nt(qp[:, sl], kp[:, sl]).astype(BF16) for sl in sls])
    rs = [jnp.concatenate([_dot(p_all[blk][:, sl], ones) for sl in sls], axis=1) for blk in range(nblk)]
    rv = [(rs[blk] * v_tile[blk]).astype(BF16) for blk in range(nblk)]
    for h in heads:
        sl = sls[h]
        parts = []
        for blk in range(nblk):
            o_i = _dot(sel, rv[blk][:, sl])
            if blk > 0:
                o_i = o_i + _dot(sc_off[blk - 1][h], ivb[:blk * SUB, sl])
            parts.append(o_i)
        o = o_inter[h] + jnp.concatenate(parts, axis=0)
        on = _rms(o, on_ref[:, sl])
        g_h = gt[:, sl]
        o_ref[0, :, sl] = (on * (g_h * _sigmoid(g_h))).astype(BF16)


def _hgrn(u, lbp, o_norm, tri, ones, sel):
    b, s, _ = u.shape
    full = lambda bi, ci: (0, 0)
    wspec = lambda t: pl.BlockSpec(t.shape, full)
    return pl.pallas_call(
        _hgrn_kernel,
        out_shape=jax.ShapeDtypeStruct((b, s, HGRN_W), BF16),
        grid=(b, s // CHUNK),
        in_specs=[pl.BlockSpec((1, CHUNK, HGRN_IN), lambda bi, ci: (bi, ci, 0)), wspec(lbp), wspec(o_norm),
                  wspec(tri), wspec(ones), wspec(sel)],
        out_specs=pl.BlockSpec((1, CHUNK, HGRN_W), lambda bi, ci: (bi, ci, 0)),
        scratch_shapes=[pltpu.VMEM((HGRN_HEADS, HGRN_D, HGRN_D), F32)],
        compiler_params=_cparams(("parallel", "arbitrary")),
        name="hgrn2",
    )(u, lbp, o_norm, tri, ones, sel)


def _rope_tables(seq):
    pos = jnp.arange(seq, dtype=F32)[:, None]

    def cs(half):
        inv = jnp.power(ROPE_THETA, -jnp.arange(half, dtype=F32) / half)
        ang = pos * inv[None, :]
        return jnp.cos(ang), jnp.sin(ang)

    cd, sd = cs(DIFF_HD // 2)
    cos_d = jnp.tile(jnp.concatenate([cd, cd], axis=1), (1, LANES // DIFF_HD))
    sin_d = jnp.tile(jnp.concatenate([-sd, sd], axis=1), (1, LANES // DIFF_HD))
    cm, sm = cs(MLA_ROPE // 2)
    one = jnp.ones((seq, MLA_NOPE), F32)
    tail = LANES - MLA_QK
    cos_m = jnp.concatenate([one, cm, cm, jnp.ones((seq, tail), F32)], axis=1)
    sin_m = jnp.concatenate([0 * one, -sm, sm, jnp.zeros((seq, tail), F32)], axis=1)
    return cos_d, sin_d, cos_m, sin_m


def _router_weights(w_group, b_group, w_expert, b_expert):
    d = w_group.shape[0]
    pad = LANES - MOE_GROUPS - MOE_EXPERTS
    w = jnp.concatenate([w_group, w_expert, jnp.zeros((d, pad), F32)], axis=1)
    bias = jnp.concatenate([b_group, b_expert, jnp.zeros((pad,), F32)])[None, :]
    hi = w.astype(BF16)
    lo = (w - hi.astype(F32)).astype(BF16)
    return hi, lo, bias


def kernel(x, norm_mix, norm_ffn, norm_final, ev_w_in, ev_w_out, mla_q_norm, mla_w_q_up, mla_kv_norm, mla_w_kv_up, diff_lambda, diff_subln, od_w_in, od_w_out, rwkv_mu, rwkv_w0, rwkv_w2, rwkv_a0, rwkv_a2, rwkv_g2, rwkv_k_k, rwkv_k_a, rwkv_r_k, rwkv_ln_w, rwkv_ln_b, hgrn_lb, hgrn_o_norm, moe_w_group, moe_b_group, moe_w_expert, moe_b_expert, moe_w_gate, moe_w_up, moe_w_down):
    b, s, d = x.shape
    n = b * s
    assert norm_mix.shape[0] == 2 and hgrn_lb.shape[0] == 2
    assert s % ATTN_BLOCK == 0 and s % ROW_TILE == 0 and ATTN_BLOCK % CHUNK == 0
    row2 = lambda t: t.reshape(1, -1)
    h = x.reshape(n, d)

    w_in = ev_w_in[0]
    o1, o2, o3 = MLA_LORA, 2 * MLA_LORA, 2 * MLA_LORA + MLA_ROPE
    kr_pad = jnp.zeros((d, LANES), F32).at[:, MLA_NOPE:MLA_QK].set(w_in[:, o2:o3])
    w0 = jnp.concatenate([w_in[:, :o2], kr_pad, w_in[:, o3:]], axis=1).astype(BF16)
    wq = mla_w_q_up[0].reshape(MLA_LORA, MLA_HEADS, MLA_QK)
    wq = jnp.pad(wq, ((0, 0), (0, 0), (0, LANES - MLA_QK))).reshape(MLA_LORA, MLA_HEADS * LANES).astype(BF16)
    wkv = mla_w_kv_up[0].reshape(MLA_LORA, MLA_HEADS, MLA_NOPE + MLA_V)
    wk = jnp.pad(wkv[:, :, :MLA_NOPE], ((0, 0), (0, 0), (0, LANES - MLA_NOPE)))
    wk = wk.reshape(MLA_LORA, MLA_HEADS * LANES).astype(BF16)
    wv = wkv[:, :, MLA_NOPE:].reshape(MLA_LORA, MLA_HEADS * MLA_V).astype(BF16)
    cos_d, sin_d, cos_m, sin_m = _rope_tables(s)
    qf, kf, vm, dq, dk, dv = _ev_in(h, row2(norm_mix[0]), w0, row2(mla_q_norm[0]), row2(mla_kv_norm[0]),
                                    wq, wk, wv, cos_d, sin_d, cos_m, sin_m, s)
    r3 = lambda t: t.reshape(b, s, t.shape[-1])
    o_mla = _mla_attn(r3(qf), r3(kf), vm)
    lam_init = 0.8 - 0.6 * math.exp(-0.3 * 0)
    o_diff = _diff_attn(r3(dq), r3(dk), dv, diff_lambda[0], row2(diff_subln[0]), lam_init)
    rwh, rwl, rb = _router_weights(moe_w_group[0], moe_b_group[0], moe_w_expert[0], moe_b_expert[0])
    ti = jnp.arange(ROW_TILE)
    tri_rows = (ti[None, :] < ti[:, None]).astype(BF16)
    h, hf, route, ridx, counts = _mix_out(h, o_mla.reshape(n, -1), o_diff.reshape(n, -1), ev_w_out[0].astype(BF16),
                                    row2(norm_ffn[0]), rwh, rwl, rb, tri_rows)
    h = _moe(h, hf, route, ridx, counts, moe_w_gate[0], moe_w_up[0], moe_w_down[0], row2(norm_final), False)

    ur, uh = _od_in(h, row2(norm_mix[1]), od_w_in[0].astype(BF16))
    zpad = jnp.zeros((RWKV_DECAY_LORA, RWKV_W), F32)
    w2p = jnp.concatenate([rwkv_w2[0], zpad], axis=0).astype(BF16)
    a2p = jnp.concatenate([zpad, rwkv_a2[0]], axis=0).astype(BF16)
    ci = jnp.arange(CHUNK)
    tri = (ci[None, :] <= ci[:, None]).astype(BF16)
    li = jnp.arange(LANES) // RWKV_HD
    seg = (li[:, None] == li[None, :]).astype(BF16)
    o_c = _rwkv(ur.reshape(b, s, RWKV_IN), row2(rwkv_mu[0]), row2(rwkv_w0[0]), w2p, row2(rwkv_a0[0]), a2p,
                rwkv_g2[0].astype(BF16), row2(rwkv_k_k[0]), row2(rwkv_k_a[0]), row2(rwkv_r_k[0]),
                row2(rwkv_ln_w[0]), row2(rwkv_ln_b[0]),
                jnp.kron(jnp.eye(math.gcd(b, RWKV_SEQS), dtype=BF16), tri), seg)
    pt = jnp.arange(SUB * SUB) // SUB
    ps = jnp.arange(SUB * SUB) % SUB
    sel = ((pt[None, :] == jnp.arange(SUB)[:, None]) & (ps <= pt)[None, :]).astype(BF16)
    o_d = _hgrn(uh.reshape(b, s, HGRN_IN), hgrn_lb, row2(hgrn_o_norm[0]), tri, jnp.ones((LANES, LANES), BF16), sel)
    rwh, rwl, rb = _router_weights(moe_w_group[1], moe_b_group[1], moe_w_expert[1], moe_b_expert[1])
    h, hf, route, ridx, counts = _mix_out(h, o_c.reshape(n, -1), o_d.reshape(n, -1), od_w_out[0].astype(BF16),
                                    row2(norm_ffn[1]), rwh, rwl, rb, tri_rows)
    out = _moe(h, hf, route, ridx, counts, moe_w_gate[1], moe_w_up[1], moe_w_down[1], row2(norm_final), True)
    return out.reshape(b, s, d)
```

```python
import functools
import math

import jax
import jax.numpy as jnp
from jax import lax
from jax.experimental import pallas as pl
from jax.experimental.pallas import tpu as pltpu

F32 = jnp.float32
BF16 = jnp.bfloat16

CHUNK = 64
ROPE_THETA = 10000.0
NORM_EPS = 1e-6
MLA_HEADS = 8
MLA_LORA = 256
MLA_NOPE = 64
MLA_ROPE = 32
MLA_V = 64
MLA_QK = MLA_NOPE + MLA_ROPE
DIFF_HEADS = 4
DIFF_HD = 64
DIFF_V = 2 * DIFF_HD
DIFF_W = DIFF_HEADS * 2 * DIFF_HD
RWKV_HEADS = 8
RWKV_HD = 64
RWKV_W = RWKV_HEADS * RWKV_HD
RWKV_DECAY_LORA = 64
RWKV_A_LORA = 64
RWKV_GATE_LORA = 128
RWKV_IN = 3 * RWKV_W + RWKV_DECAY_LORA + RWKV_A_LORA + RWKV_GATE_LORA
RWKV_GN_EPS = 64e-5
HGRN_HEADS = 4
HGRN_D = 128
HGRN_W = HGRN_HEADS * HGRN_D
HGRN_IN = 4 * HGRN_W
MOE_GROUPS = 4
MOE_EPG = 8
MOE_EXPERTS = MOE_GROUPS * MOE_EPG
MOE_FF = 512

LANES = 128
ROW_TILE = 512
ATTN_BLOCK = 512
EXPERT_TILE = 256
SUB = 16
RWKV_SEQS = 4
HGRN_SEQS = 2
VMEM_LIMIT = 48 * 1024 * 1024
NEG = -1e30
LOG2E = math.log2(math.e)


def _cparams(sem):
    return pltpu.CompilerParams(dimension_semantics=sem, vmem_limit_bytes=VMEM_LIMIT)


def _dot(a, b):
    return jnp.dot(a, b, preferred_element_type=F32)


def _dot_nt(a, b):
    return lax.dot_general(a, b, (((1,), (1,)), ((), ())), preferred_element_type=F32)


def _dot_tn(a, b):
    return lax.dot_general(a, b, (((0,), (0,)), ((), ())), preferred_element_type=F32)


def _rms(x, g):
    return x * lax.rsqrt(jnp.mean(x * x, axis=-1, keepdims=True) + NORM_EPS) * g


def _sigmoid(x):
    return 1.0 / (1.0 + jnp.exp(-x))


def _split_dot(x, w_bf16, left=False):
    hi = x.astype(BF16)
    lo = (x - hi.astype(F32)).astype(BF16)
    if left:
        return _dot(w_bf16, hi) + _dot(w_bf16, lo)
    return _dot(hi, w_bf16) + _dot(lo, w_bf16)


def _tile_lanes(t, n):
    return jnp.concatenate([t] * n, axis=-1) if n > 1 else t


def _rope_rot(x, half):
    w = x.shape[-1]
    lane = lax.broadcasted_iota(jnp.int32, x.shape, x.ndim - 1)
    first = (lane & (2 * half - 1)) < half
    return jnp.where(first, pltpu.roll(x, w - half, x.ndim - 1), pltpu.roll(x, half, x.ndim - 1))


def _ev_in_kernel(x_ref, g_ref, w0_ref, qg_ref, kvg_ref, wq_ref, wk_ref, wv_ref,
                  cd_ref, sd_ref, cm_ref, sm_ref,
                  qf_ref, kf_ref, vm_ref, dq_ref, dk_ref, dv_ref):
    hn = _rms(x_ref[...], g_ref[...]).astype(BF16)
    u = _dot(hn, w0_ref[...])
    c_q = u[:, 0:256]
    c_kv = u[:, 256:512]
    kr = u[:, 512:640]
    dq = u[:, 640:1152]
    dk = u[:, 1152:1664]
    dv = u[:, 1664:2176]
    cqn = _rms(c_q, qg_ref[...]).astype(BF16)
    ckn = _rms(c_kv, kvg_ref[...]).astype(BF16)
    q = _dot(cqn, wq_ref[...])
    k = _dot(ckn, wk_ref[...])
    v = _dot(ckn, wv_ref[...])
    cm = cm_ref[...]
    sm = sm_ref[...]
    m_half = MLA_ROPE // 2
    lane = lax.broadcasted_iota(jnp.int32, q.shape, 1) & (LANES - 1)
    first = (lane >= MLA_NOPE) & (lane < MLA_NOPE + m_half)
    rot_q = jnp.where(first, pltpu.roll(q, q.shape[1] - m_half, 1), pltpu.roll(q, m_half, 1))
    q = (q * _tile_lanes(cm, MLA_HEADS) + rot_q * _tile_lanes(sm, MLA_HEADS)) * (MLA_QK ** -0.5 * LOG2E)
    lane1 = lax.broadcasted_iota(jnp.int32, kr.shape, 1)
    first1 = (lane1 >= MLA_NOPE) & (lane1 < MLA_NOPE + m_half)
    rot_k = jnp.where(first1, pltpu.roll(kr, LANES - m_half, 1), pltpu.roll(kr, m_half, 1))
    kr = kr * cm + rot_k * sm
    k = k + _tile_lanes(kr, MLA_HEADS)
    qf_ref[...] = q.astype(BF16)
    kf_ref[...] = k.astype(BF16)
    vm_ref[0, :, 0] = v.T.reshape(vm_ref.shape[1], LANES, v.shape[0]).astype(BF16)
    cd = _tile_lanes(cd_ref[...], DIFF_W // LANES)
    sd = _tile_lanes(sd_ref[...], DIFF_W // LANES)
    dq = (dq * cd + _rope_rot(dq, DIFF_HD // 2) * sd) * (DIFF_HD ** -0.5 * LOG2E)
    dk = dk * cd + _rope_rot(dk, DIFF_HD // 2) * sd
    dq_ref[...] = dq.astype(BF16)
    dk_ref[...] = dk.astype(BF16)
    dv_ref[0, :, 0] = dv.T.reshape(dv_ref.shape[1], LANES, dv.shape[0]).astype(BF16)


def _ev_in(x2, g, w0, qg, kvg, wq, wk, wv, cd, sd, cm, sm, seq):
    n, d = x2.shape
    tm = ATTN_BLOCK
    nt = seq // tm
    row = lambda i: (i, 0)
    full = lambda i: (0, 0)
    tab = lambda i: (i % nt, 0)
    wspec = lambda a: pl.BlockSpec(a.shape, full)
    vt_shape = lambda w: jax.ShapeDtypeStruct((n // seq, w // LANES, nt, LANES, tm), BF16)
    vt_spec = lambda w: pl.BlockSpec((1, w // LANES, 1, LANES, tm), lambda i: (i // nt, 0, i % nt, 0, 0))
    rows = lambda w: jax.ShapeDtypeStruct((n, w), BF16)
    rspec = lambda w: pl.BlockSpec((tm, w), row)
    wide = MLA_HEADS * LANES
    return pl.pallas_call(
        _ev_in_kernel,
        out_shape=[rows(wide), rows(wide), vt_shape(MLA_HEADS * MLA_V), rows(DIFF_W), rows(DIFF_W), vt_shape(DIFF_W)],
        grid=(n // tm,),
        in_specs=[pl.BlockSpec((tm, d), row), wspec(g), wspec(w0), wspec(qg), wspec(kvg), wspec(wq), wspec(wk),
                  wspec(wv)] + [pl.BlockSpec((tm, LANES), tab)] * 4,
        out_specs=[rspec(wide), rspec(wide), vt_spec(MLA_HEADS * MLA_V), rspec(DIFF_W), rspec(DIFF_W),
                   vt_spec(DIFF_W)],
        compiler_params=_cparams(("parallel",)),
        name="ev_in",
    )(x2, g, w0, qg, kvg, wq, wk, wv, cd, sd, cm, sm)


ONES_ROWS = 16


def _chunk_mask_t(shape):
    sh = int(math.log2(CHUNK))
    key_chunk = jnp.right_shift(lax.broadcasted_iota(jnp.int32, shape, 0), sh)
    q_chunk = jnp.right_shift(lax.broadcasted_iota(jnp.int32, shape, 1), sh)
    return key_chunk <= q_chunk


def _attn_streams(i, qs, key_of, val_of, k_ref, vt_ref, m_ref, acc_ref):
    bk = ATTN_BLOCK
    n = len(qs)
    m_ref[...] = jnp.full(m_ref.shape, NEG, F32)
    acc_ref[...] = jnp.zeros(acc_ref.shape, F32)
    dv = acc_ref.shape[1] - ONES_ROWS
    ones = jnp.ones((ONES_ROWS, bk), BF16)

    def step(j, masked):
        kb = k_ref[0, pl.ds(pl.multiple_of(j * bk, bk), bk), :]
        vt = vt_ref[0, :, j]
        st = [_dot_nt(key_of(kb, s), qs[s]) for s in range(n)]
        if masked:
            mask = _chunk_mask_t(st[0].shape)
            st = [jnp.where(mask, x, NEG) for x in st]
        m_prev = [m_ref[s] for s in range(n)]
        m_new = [jnp.maximum(m_prev[s], jnp.max(st[s], axis=0, keepdims=True)) for s in range(n)]
        p = [jnp.exp2(st[s] - m_new[s]).astype(BF16) for s in range(n)]
        pv = [_dot(jnp.concatenate([val_of(vt, s), ones], axis=0), p[s]) for s in range(n)]
        for s in range(n):
            acc_ref[s] = jnp.exp2(m_prev[s] - m_new[s]) * acc_ref[s] + pv[s]
            m_ref[s] = m_new[s]

    def body(j, carry):
        step(j, False)
        return carry

    lax.fori_loop(0, i, body, 0)
    step(i, True)
    accs = [acc_ref[s] for s in range(n)]
    return [a[:dv] / a[dv:dv + 1] for a in accs]


def _mla_attn_kernel(q_ref, k_ref, vt_ref, o_ref, m_ref, acc_ref):
    q = q_ref[0]
    n = q.shape[1] // LANES
    qs = [q[:, s * LANES:(s + 1) * LANES] for s in range(n)]
    key_of = lambda kb, s: kb[:, s * LANES:(s + 1) * LANES]
    val_of = lambda vt, s: vt[s // 2, (s % 2) * MLA_V:(s % 2 + 1) * MLA_V]
    outs = _attn_streams(pl.program_id(2), qs, key_of, val_of, k_ref, vt_ref, m_ref, acc_ref)
    o_ref[0] = jnp.concatenate(outs, axis=0).T.astype(BF16)


def _diff_attn_kernel(q_ref, k_ref, vt_ref, lam_ref, sub_ref, o_ref, m_ref, acc_ref, *, lam_init):
    q = q_ref[0]
    nh = q.shape[1] // LANES
    lane = lax.broadcasted_iota(jnp.int32, (q.shape[0], LANES), 1)
    qs = []
    for h in range(nh):
        q_h = q[:, h * LANES:(h + 1) * LANES]
        zero = jnp.zeros_like(q_h)
        qs += [jnp.where(lane < DIFF_HD, q_h, zero), jnp.where(lane >= DIFF_HD, q_h, zero)]
    key_of = lambda kb, s: kb[:, (s // 2) * LANES:(s // 2 + 1) * LANES]
    val_of = lambda vt, s: vt[s // 2]
    outs = _attn_streams(pl.program_id(2), qs, key_of, val_of, k_ref, vt_ref, m_ref, acc_ref)
    lam = lam_ref[...]
    s1 = jnp.sum(lam[0:1] * lam[1:2], axis=-1, keepdims=True)
    s2 = jnp.sum(lam[2:3] * lam[3:4], axis=-1, keepdims=True)
    lam_full = jnp.exp(s1) - jnp.exp(s2) + lam_init
    for h in range(nh):
        o = (outs[2 * h] - lam_full * outs[2 * h + 1]).T
        o_ref[0, :, h * LANES:(h + 1) * LANES] = (_rms(o, sub_ref[...]) * (1.0 - lam_init)).astype(BF16)


ATTN_STREAMS = 4


def _attn_scratch(bq, dv):
    return [pltpu.VMEM((ATTN_STREAMS, 1, bq), F32), pltpu.VMEM((ATTN_STREAMS, dv + ONES_ROWS, bq), F32)]


def _vt_spec(nk, groups):
    return pl.BlockSpec((1, groups, nk, LANES, ATTN_BLOCK), lambda bi, h, i: (bi, h, 0, 0, 0))


def _mla_attn(qf, kf, vt):
    b, s, _ = qf.shape
    bq = ATTN_BLOCK
    w = ATTN_STREAMS * LANES
    return pl.pallas_call(
        _mla_attn_kernel,
        out_shape=jax.ShapeDtypeStruct((b, s, MLA_HEADS * MLA_V), BF16),
        grid=(b, MLA_HEADS // ATTN_STREAMS, s // bq),
        in_specs=[pl.BlockSpec((1, bq, w), lambda bi, h, i: (bi, i, h)),
                  pl.BlockSpec((1, s, w), lambda bi, h, i: (bi, 0, h)),
                  _vt_spec(s // bq, ATTN_STREAMS // 2)],
        out_specs=pl.BlockSpec((1, bq, ATTN_STREAMS * MLA_V), lambda bi, h, i: (bi, i, h)),
        scratch_shapes=_attn_scratch(bq, MLA_V),
        compiler_params=_cparams(("parallel", "parallel", "arbitrary")),
        name="mla_attn",
    )(qf, kf, vt)


def _diff_attn(dq, dk, dvt, lam, subln, lam_init):
    b, s, _ = dq.shape
    bq = ATTN_BLOCK
    nh = ATTN_STREAMS // 2
    blk = lambda rows, im: pl.BlockSpec((1, rows, nh * LANES), im)
    return pl.pallas_call(
        functools.partial(_diff_attn_kernel, lam_init=lam_init),
        out_shape=jax.ShapeDtypeStruct((b, s, DIFF_HEADS * DIFF_V), BF16),
        grid=(b, DIFF_HEADS // nh, s // bq),
        in_specs=[blk(bq, lambda bi, h, i: (bi, i, h)), blk(s, lambda bi, h, i: (bi, 0, h)),
                  _vt_spec(s // bq, nh),
                  pl.BlockSpec(lam.shape, lambda bi, h, i: (0, 0)),
                  pl.BlockSpec(subln.shape, lambda bi, h, i: (0, 0))],
        out_specs=blk(bq, lambda bi, h, i: (bi, i, h)),
        scratch_shapes=_attn_scratch(bq, DIFF_V),
        compiler_params=_cparams(("parallel", "parallel", "arbitrary")),
        name="diff_attn",
    )(dq, dk, dvt, lam, subln)


def _mix_out_kernel(h_ref, a_ref, b_ref, w_ref, g_ref, rwh_ref, rwl_ref, rb_ref, tri_ref,
                    hout_ref, hf_ref, route_ref, idx_ref, cnt_ref):
    mixed = jnp.concatenate([a_ref[...], b_ref[...]], axis=-1)
    h = h_ref[...] + _dot(mixed, w_ref[...])
    hout_ref[...] = h
    hf = _rms(h, g_ref[...])
    hi = hf.astype(BF16)
    hf_ref[...] = hi
    lo = (hf - hi.astype(F32)).astype(BF16)
    z = _dot(hi, rwh_ref[...]) + _dot(lo, rwh_ref[...]) + _dot(hi, rwl_ref[...]) + rb_ref[...]
    lane_i = lax.broadcasted_iota(jnp.int32, z.shape, 1)
    lane = lane_i.astype(F32)
    big = float(LANES)
    is_g = lane_i < MOE_GROUPS
    zg = jnp.where(is_g, z, NEG)
    mg = jnp.max(zg, axis=-1, keepdims=True)
    g_idx = jnp.min(jnp.where(zg == mg, lane, big), axis=-1, keepdims=True)
    pg_top = 1.0 / jnp.sum(jnp.where(is_g, jnp.exp(zg - mg), 0.0), axis=-1, keepdims=True)
    grp_of_lane = jnp.right_shift(lane_i - MOE_GROUPS, int(math.log2(MOE_EPG))).astype(F32)
    in_grp = (lane_i >= MOE_GROUPS) & (lane_i < MOE_GROUPS + MOE_EXPERTS) & (grp_of_lane == g_idx)
    ze = jnp.where(in_grp, z, NEG)
    m1 = jnp.max(ze, axis=-1, keepdims=True)
    i1 = jnp.min(jnp.where(ze == m1, lane, big), axis=-1, keepdims=True)
    ze2 = jnp.where(lane == i1, NEG, ze)
    m2 = jnp.max(ze2, axis=-1, keepdims=True)
    i2 = jnp.min(jnp.where(ze2 == m2, lane, big), axis=-1, keepdims=True)
    r = jnp.exp(m2 - m1)
    w1 = 1.0 / (1.0 + r)
    w2 = r / (1.0 + r)
    e1 = i1 - float(MOE_GROUPS)
    e2 = i2 - float(MOE_GROUPS)
    @pl.when(pl.program_id(0) == 0)
    def _():
        cnt_ref[...] = jnp.zeros(cnt_ref.shape, F32)

    both = jnp.where((lane == i1) | (lane == i2), 1.0, 0.0)
    before = _dot(tri_ref[...], both.astype(BF16)) + cnt_ref[0:1, :]
    rank1 = jnp.sum(jnp.where(lane == i1, before, 0.0), axis=-1, keepdims=True)
    rank2 = jnp.sum(jnp.where(lane == i2, before, 0.0), axis=-1, keepdims=True)
    cnt_ref[...] = cnt_ref[...] + jnp.sum(both, axis=0, keepdims=True)
    vals = (e1, e2, pg_top * w1, pg_top * w2, rank1, rank2)
    out = jnp.zeros(z.shape, F32)
    for col, val in enumerate(vals):
        out = jnp.where(lane_i == col, val, out)
    route_ref[...] = out
    idx_ref[0] = out.T[0:8].astype(jnp.int32)


def _mix_out(h, a, b, w, g, rwh, rwl, rb, tri):
    n, d = h.shape
    tm = ROW_TILE
    row = lambda i: (i, 0)
    full = lambda i: (0, 0)
    wspec = lambda t: pl.BlockSpec(t.shape, full)
    return pl.pallas_call(
        _mix_out_kernel,
        out_shape=[jax.ShapeDtypeStruct((n, d), F32), jax.ShapeDtypeStruct((n, d), BF16),
                   jax.ShapeDtypeStruct((n, LANES), F32), jax.ShapeDtypeStruct((n // tm, 8, tm), jnp.int32),
                   jax.ShapeDtypeStruct((8, LANES), F32)],
        grid=(n // tm,),
        in_specs=[pl.BlockSpec((tm, d), row), pl.BlockSpec((tm, a.shape[1]), row), pl.BlockSpec((tm, b.shape[1]), row),
                  wspec(w), wspec(g), wspec(rwh), wspec(rwl), wspec(rb), wspec(tri)],
        out_specs=[pl.BlockSpec((tm, d), row), pl.BlockSpec((tm, d), row), pl.BlockSpec((tm, LANES), row),
                   pl.BlockSpec((1, 8, tm), lambda i: (i, 0, 0)), pl.BlockSpec((8, LANES), full)],
        compiler_params=_cparams(("arbitrary",)),
        name="mix_out",
    )(h, a, b, w, g, rwh, rwl, rb, tri)


def _expert_kernel(te_ref, nv_ref, x_ref, wg_ref, wu_ref, wd_ref, y_ref, wgb_ref, wub_ref, wdb_ref):
    t = pl.program_id(0)

    @pl.when((t == 0) | (te_ref[t] != te_ref[jnp.maximum(t - 1, 0)]))
    def _():
        wgb_ref[...] = wg_ref[0, 0].astype(BF16)
        wub_ref[...] = wu_ref[0, 0].astype(BF16)
        wdb_ref[...] = wd_ref[0, 0].astype(BF16)

    @pl.when(t < nv_ref[0])
    def _():
        x = x_ref[...]
        a = _dot(x, wgb_ref[...])
        up = _dot(x, wub_ref[...])
        act = (a * _sigmoid(a) * up).astype(BF16)
        y_ref[...] = _dot(act, wdb_ref[...]).astype(BF16)

    @pl.when(t >= nv_ref[0])
    def _():
        y_ref[...] = jnp.zeros(y_ref.shape, BF16)


def _experts(xs, tile_expert, n_valid, wg, wu, wd, layer):
    p, d = xs.shape
    tm = EXPERT_TILE
    ff = wg.shape[3]
    grid_spec = pltpu.PrefetchScalarGridSpec(
        num_scalar_prefetch=2,
        grid=(p // tm,),
        in_specs=[pl.BlockSpec((tm, d), lambda t, te, nv: (t, 0)),
                  pl.BlockSpec((1, 1, d, ff), lambda t, te, nv: (layer, te[t], 0, 0)),
                  pl.BlockSpec((1, 1, d, ff), lambda t, te, nv: (layer, te[t], 0, 0)),
                  pl.BlockSpec((1, 1, ff, d), lambda t, te, nv: (layer, te[t], 0, 0))],
        out_specs=pl.BlockSpec((tm, d), lambda t, te, nv: (t, 0)),
        scratch_shapes=[pltpu.VMEM((d, ff), BF16), pltpu.VMEM((d, ff), BF16), pltpu.VMEM((ff, d), BF16)],
    )
    return pl.pallas_call(
        _expert_kernel,
        out_shape=jax.ShapeDtypeStruct((p, d), BF16),
        grid_spec=grid_spec,
        compiler_params=_cparams(("arbitrary",)),
        name="experts",
    )(tile_expert, n_valid, xs, wg, wu, wd)


def _combine_kernel(h_ref, y1_ref, y2_ref, route_ref, g_ref, o_ref, *, final):
    route = route_ref[...]
    out = h_ref[...] + route[:, 2:3] * y1_ref[...].astype(F32) + route[:, 3:4] * y2_ref[...].astype(F32)
    if final:
        out = _rms(out, g_ref[...])
    o_ref[...] = out


def _combine(h, y1, y2, route, g, final):
    n, d = h.shape
    tm = ROW_TILE
    row = lambda i: (i, 0)
    return pl.pallas_call(
        functools.partial(_combine_kernel, final=final),
        out_shape=jax.ShapeDtypeStruct((n, d), F32),
        grid=(n // tm,),
        in_specs=[pl.BlockSpec((tm, d), row), pl.BlockSpec((tm, d), row), pl.BlockSpec((tm, d), row),
                  pl.BlockSpec((tm, LANES), row), pl.BlockSpec(g.shape, lambda i: (0, 0))],
        out_specs=pl.BlockSpec((tm, d), row),
        compiler_params=_cparams(("parallel",)),
        name="moe_combine",
    )(h, y1, y2, route, g)


def _moe(h, hf, route, idx, counts, wg, wu, wd, layer, g_final, final):
    n, d = h.shape
    tm = EXPERT_TILE
    ids = jnp.arange(MOE_EXPERTS, dtype=jnp.int32)
    counts = counts[0, MOE_GROUPS:MOE_GROUPS + MOE_EXPERTS].astype(jnp.int32)
    padded = ((counts + tm - 1) // tm) * tm
    ends = jnp.cumsum(padded)
    starts = ends - padded
    start_of = lambda e: jnp.sum(jnp.where(e[:, None] == ids[None, :], starts[None, :], 0), axis=-1)
    e1, e2 = idx[:, 0, :].reshape(n), idx[:, 1, :].reshape(n)
    dest1 = start_of(e1) + idx[:, 4, :].reshape(n)
    dest2 = start_of(e2) + idx[:, 5, :].reshape(n)
    n_rows = 2 * n + MOE_EXPERTS * tm
    tok = jnp.arange(n, dtype=jnp.int32)
    tok_for_row = (jnp.arange(n_rows, dtype=jnp.int32) % n).at[jnp.concatenate([dest1, dest2])].set(
        jnp.concatenate([tok, tok]), mode="promise_in_bounds", unique_indices=True)
    take = functools.partial(jnp.take, axis=0, mode="clip")
    tile_start = jnp.arange(n_rows // tm, dtype=jnp.int32) * tm
    tile_expert = jnp.minimum(jnp.sum((ends[None, :] <= tile_start[:, None]).astype(jnp.int32), axis=1),
                              MOE_EXPERTS - 1)
    n_valid = (ends[-1] // tm).astype(jnp.int32).reshape(1)
    xs = take(hf, tok_for_row)
    ys = _experts(xs, tile_expert, n_valid, wg, wu, wd, layer)
    return _combine(h, take(ys, dest1), take(ys, dest2), route, g_final, final)


def _od_in_kernel(x_ref, g_ref, w_ref, ur_ref, uh_ref):
    hn = _rms(x_ref[...], g_ref[...]).astype(BF16)
    u = _dot(hn, w_ref[...])
    ur_ref[...] = u[:, :RWKV_IN].astype(BF16)
    uh_ref[...] = u[:, RWKV_IN:].astype(BF16)


def _od_in(x2, g, w):
    n, d = x2.shape
    tm = ROW_TILE
    row = lambda i: (i, 0)
    return pl.pallas_call(
        _od_in_kernel,
        out_shape=[jax.ShapeDtypeStruct((n, RWKV_IN), BF16), jax.ShapeDtypeStruct((n, HGRN_IN), BF16)],
        grid=(n // tm,),
        in_specs=[pl.BlockSpec((tm, d), row), pl.BlockSpec(g.shape, lambda i: (0, 0)),
                  pl.BlockSpec(w.shape, lambda i: (0, 0))],
        out_specs=[pl.BlockSpec((tm, RWKV_IN), row), pl.BlockSpec((tm, HGRN_IN), row)],
        compiler_params=_cparams(("parallel",)),
        name="od_in",
    )(x2, g, w)


def _rwkv_kernel(u_ref, mu_ref, w0_ref, w2_ref, a0_ref, a2_ref, g2_ref, kk_ref, ka_ref, rk_ref,
                 lnw_ref, lnb_ref, tri_ref, seg_ref, o_ref, prev_ref, ht_ref):
    c = pl.program_id(1)

    @pl.when(c == 0)
    def _():
        prev_ref[...] = jnp.zeros(prev_ref.shape, F32)
        ht_ref[...] = jnp.zeros(ht_ref.shape, F32)

    nb = u_ref.shape[0]
    u = u_ref[...].reshape(nb * CHUNK, RWKV_IN).astype(F32)
    rows = lax.broadcasted_iota(jnp.int32, u.shape, 0)
    u_prev = pltpu.roll(u, 1, 0)
    for bi in range(nb):
        u_prev = jnp.where(rows == bi * CHUNK, prev_ref[bi], u_prev)
        prev_ref[bi] = u[(bi + 1) * CHUNK - 1:(bi + 1) * CHUNK, :]
    us = u + (u_prev - u) * mu_ref[...]
    w = RWKV_W
    r = us[:, 0:w]
    k = us[:, w:2 * w]
    v = us[:, 2 * w:3 * w]
    x12 = us[:, 3 * w:3 * w + LANES]
    xg = us[:, 3 * w + LANES:]
    seg_pair = seg_ref[...]

    def seg(x):
        return jnp.concatenate([_split_dot(x[:, p * LANES:(p + 1) * LANES], seg_pair)
                                for p in range(RWKV_HEADS // 2)], axis=1)

    tri = tri_ref[...]

    wl = w0_ref[...] + _dot(jnp.tanh(x12).astype(BF16), w2_ref[...])
    nwl = -wl
    softplus = jnp.maximum(nwl, 0.0) + jnp.log1p(jnp.exp(-jnp.abs(nwl)))
    lw = -jnp.exp(-softplus - 0.5)
    a = _sigmoid(a0_ref[...] + _dot(x12.astype(BF16), a2_ref[...]))
    g = _dot(_sigmoid(xg).astype(BF16), g2_ref[...])
    kk = k * kk_ref[...]
    kk = kk * lax.rsqrt(jnp.maximum(seg(kk * kk), 1e-24))
    k2 = k * (1.0 + (a - 1.0) * ka_ref[...])
    a_in = -kk
    b_in = kk * a

    lg = _split_dot(lw, tri, left=True)
    rbs = [slice(bi * CHUNK, (bi + 1) * CHUNK) for bi in range(nb)]
    lg_c = [lg[(bi + 1) * CHUNK - 1:(bi + 1) * CHUNK, :] for bi in range(nb)]
    e_neg = jnp.exp(-lg)
    e_rel = jnp.concatenate([jnp.exp(lg_c[bi] - lg[rbs[bi]]) for bi in range(nb)], axis=0)
    g_c = [jnp.exp(x) for x in lg_c]
    at = a_in * jnp.exp(lg - lw)
    rt = r * jnp.exp(lg)
    kt = k2 * e_neg
    bt = b_in * e_neg
    kh = k2 * e_rel
    bh = b_in * e_rel

    c2 = 2 * CHUNK
    ri = lax.broadcasted_iota(jnp.int32, (c2, c2), 0)
    ci = lax.broadcasted_iota(jnp.int32, (c2, c2), 1)
    sh = int(math.log2(CHUNK))
    same = jnp.right_shift(ri, sh) == jnp.right_shift(ci, sh)
    strict = same & (ri > ci)
    incl = same & (ri >= ci)
    eye = jnp.where(ri == ci, 1.0, 0.0).astype(F32)
    lane = lax.broadcasted_iota(jnp.int32, (CHUNK, LANES), 1)
    lo_half = lane < RWKV_HD

    def bd(x):
        return jnp.concatenate([jnp.where(lo_half, x, 0.0), jnp.where(lo_half, 0.0, x)], axis=0)

    npair = RWKV_HEADS // 2
    pairs = range(nb * npair)
    sls = [(rbs[c // npair], slice((c % npair) * LANES, (c % npair + 1) * LANES)) for c in pairs]
    bdb = lambda x: [bd(x[sl]).astype(BF16) for sl in sls]
    at_b = [bd(at[sl]) for sl in sls]
    rt_b = [bd(rt[sl]) for sl in sls]
    kt_b, bt_b, kh_b, bh_b, v_b = bdb(kt), bdb(bt), bdb(kh), bdb(bh), bdb(v)
    hts = [ht_ref[p] for p in pairs]
    htb = [t.astype(BF16) for t in hts]
    sc = [_dot_nt(jnp.concatenate([at_b[p], rt_b[p]], axis=0).astype(BF16),
                  jnp.concatenate([kt_b[p], bt_b[p]], axis=0)) for p in pairs]
    a_ak = [jnp.where(strict, s[:c2, :c2], 0.0).astype(BF16) for s in sc]
    a_ab = [jnp.where(strict, s[:c2, c2:], 0.0) for s in sc]
    a_rk = [jnp.where(incl, s[c2:, :c2], 0.0).astype(BF16) for s in sc]
    a_rb = [jnp.where(incl, s[c2:, c2:], 0.0).astype(BF16) for s in sc]
    x2 = [_dot(a_ak[p], v_b[p]) for p in pairs]
    e1 = [_dot(a_rk[p], v_b[p]) for p in pairs]
    d1 = [_dot_tn(v_b[p], kh_b[p]) for p in pairs]
    pw = a_ab
    t_inv = [eye + a for a in a_ab]
    for _ in range(int(math.log2(CHUNK)) - 1):
        pwb = [x.astype(BF16) for x in pw]
        pw = [_dot(x, x) for x in pwb]
        t_inv = [t_inv[p] + _dot(t_inv[p].astype(BF16), pw[p].astype(BF16)) for p in pairs]
    txb = [_dot(t_inv[p].astype(BF16), jnp.concatenate([at_b[p], x2[p]], axis=1).astype(BF16)).astype(BF16)
           for p in pairs]
    qe = [_dot(a_rb[p], txb[p]) for p in pairs]
    gd = [_dot_tn(txb[p], bh_b[p]) for p in pairs]
    y_b = [_dot_nt((rt_b[p] + qe[p][:, :LANES]).astype(BF16), htb[p]) + e1[p] + qe[p][:, LANES:] for p in pairs]
    for p in pairs:
        upd = hts[p] * g_c[p // npair][:, sls[p][1]] + _dot(htb[p], gd[p][:LANES].astype(BF16)) + d1[p] + gd[p][LANES:]
        ht_ref[p] = jnp.where(same, upd, 0.0)
    y_p = [t[:CHUNK] + t[CHUNK:] for t in y_b]
    y = jnp.concatenate([jnp.concatenate(y_p[bi * npair:(bi + 1) * npair], axis=1) for bi in range(nb)],
                        axis=0)

    inv_n = 1.0 / RWKV_HD
    mean = seg(y) * inv_n
    dlt = y - mean
    var = seg(dlt * dlt) * inv_n
    yn = dlt * lax.rsqrt(var + RWKV_GN_EPS) * lnw_ref[...] + lnb_ref[...]
    bonus = seg(r * k2 * rk_ref[...]) * v
    o_ref[...] = ((yn + bonus) * g).astype(BF16).reshape(nb, CHUNK, RWKV_W)


def _rwkv(u, mu, w0, w2p, a0, a2p, g2, k_k, k_a, r_k, ln_w, ln_b, tri, seg):
    b, s, _ = u.shape
    nb = tri.shape[0] // CHUNK
    full = lambda bi, ci: (0, 0)
    wspec = lambda t: pl.BlockSpec(t.shape, full)
    params = [mu, w0, w2p, a0, a2p, g2, k_k, k_a, r_k, ln_w, ln_b, tri, seg]
    return pl.pallas_call(
        _rwkv_kernel,
        out_shape=jax.ShapeDtypeStruct((b, s, RWKV_W), BF16),
        grid=(b // nb, s // CHUNK),
        in_specs=[pl.BlockSpec((nb, CHUNK, RWKV_IN), lambda bi, ci: (bi, ci, 0))] + [wspec(t) for t in params],
        out_specs=pl.BlockSpec((nb, CHUNK, RWKV_W), lambda bi, ci: (bi, ci, 0)),
        scratch_shapes=[pltpu.VMEM((nb, 1, RWKV_IN), F32), pltpu.VMEM((nb * RWKV_HEADS // 2, LANES, LANES), F32)],
        compiler_params=_cparams(("parallel", "arbitrary")),
        name="rwkv7",
    )(u, *params)


def _hgrn_kernel(u_ref, lb_ref, on_ref, tri_ref, ones_ref, sel_ref, o_ref, st_ref):
    c = pl.program_id(1)

    @pl.when(c == 0)
    def _():
        st_ref[...] = jnp.zeros(st_ref.shape, F32)

    nb = u_ref.shape[0]
    u = u_ref[...].reshape(nb * CHUNK, HGRN_IN).astype(F32)
    lbp = lb_ref[...]
    mx = jnp.maximum(lbp[0:1], lbp[1:2])
    e0 = jnp.exp(lbp[0:1] - mx)
    e1 = jnp.exp(lbp[1:2] - mx)
    p0 = e0 / (e0 + e1)
    p1 = e1 / (e0 + e1)
    lb = (p0 + p1) - p0
    tri = tri_ref[...]
    ones = ones_ref[...]
    sel = sel_ref[...]
    d = HGRN_D
    w = HGRN_W
    heads = range(HGRN_HEADS)
    sls = [slice(h * d, (h + 1) * d) for h in heads]
    q, z, iv, gt = u[:, :w], u[:, w:2 * w], u[:, 2 * w:3 * w], u[:, 3 * w:]
    qs = q * _sigmoid(q)
    log_sig = jnp.minimum(z, 0.0) - jnp.log1p(jnp.exp(-jnp.abs(z)))
    x1 = jnp.log(lb)
    x2 = jnp.log1p(-lb) + log_sig
    log_f = jnp.maximum(x1, x2) + jnp.log1p(jnp.exp(-jnp.abs(x1 - x2)))
    key = (1.0 - lb) * _sigmoid(-z)
    bc = _split_dot(log_f, tri, left=True)
    seqs = range(nb)
    rbs = [slice(i * CHUNK, (i + 1) * CHUNK) for i in seqs]
    b_l = [bc[(i + 1) * CHUNK - 1:(i + 1) * CHUNK] for i in seqs]
    ivb = iv.astype(BF16)
    qe = (qs * jnp.exp(bc)).astype(BF16)
    kl = (key * jnp.concatenate([jnp.exp(b_l[i] - bc[rbs[i]]) for i in seqs], axis=0)).astype(BF16)
    nh = HGRN_HEADS
    sts = [st_ref[c] for c in range(nb * nh)]
    o_inter = [_dot_nt(qe[rbs[c // nh], sls[c % nh]], sts[c].astype(BF16)) for c in range(nb * nh)]
    for c in range(nb * nh):
        i, sl = c // nh, sls[c % nh]
        st_ref[c] = sts[c] * jnp.exp(b_l[i])[:, sl] + _dot_tn(ivb[rbs[i], sl], kl[rbs[i], sl])

    nblk = CHUNK // SUB
    srow = lax.broadcasted_iota(jnp.int32, (SUB, w), 0)
    p_all, v_tile, sc_off = [], [], []
    for blk in range(nb * nblk):
        base = (blk // nblk) * CHUNK
        r0 = base + (blk % nblk) * SUB
        b_i, q_i, k_i = bc[r0:r0 + SUB], qs[r0:r0 + SUB], key[r0:r0 + SUB]
        rows = [q_i[t:t + 1] * k_i * jnp.where(srow <= t, jnp.exp(b_i[t:t + 1] - b_i), 0.0) for t in range(SUB)]
        p_all.append(jnp.concatenate(rows, axis=0).astype(BF16))
        v_tile.append(jnp.concatenate([iv[r0:r0 + SUB]] * SUB, axis=0))
        if r0 > base:
            b_m = bc[r0 - 1:r0]
            qp = (q_i * jnp.exp(b_i - b_m)).astype(BF16)
            kp = (key[base:r0] * jnp.exp(b_m - bc[base:r0])).astype(BF16)
            sc_off.append([_dot_nt(qp[:, sl], kp[:, sl]).astype(BF16) for sl in sls])
        else:
            sc_off.append(None)
    rs = [jnp.concatenate([_dot(p[:, sl], ones) for sl in sls], axis=1) for p in p_all]
    rv = [(rs[blk] * v_tile[blk]).astype(BF16) for blk in range(nb * nblk)]
    for c in range(nb * nh):
        i, sl = c // nh, sls[c % nh]
        parts = []
        for blk in range(i * nblk, (i + 1) * nblk):
            o_i = _dot(sel, rv[blk][:, sl])
            if sc_off[blk] is not None:
                o_i = o_i + _dot(sc_off[blk][c % nh], ivb[i * CHUNK:i * CHUNK + (blk % nblk) * SUB, sl])
            parts.append(o_i)
        o = o_inter[c] + jnp.concatenate(parts, axis=0)
        on = _rms(o, on_ref[:, sl])
        g_h = gt[rbs[i], sl]
        o_ref[i, :, sl] = (on * (g_h * _sigmoid(g_h))).astype(BF16)


def _hgrn(u, lbp, o_norm, tri, ones, sel):
    b, s, _ = u.shape
    nb = tri.shape[0] // CHUNK
    full = lambda bi, ci: (0, 0)
    wspec = lambda t: pl.BlockSpec(t.shape, full)
    return pl.pallas_call(
        _hgrn_kernel,
        out_shape=jax.ShapeDtypeStruct((b, s, HGRN_W), BF16),
        grid=(b // nb, s // CHUNK),
        in_specs=[pl.BlockSpec((nb, CHUNK, HGRN_IN), lambda bi, ci: (bi, ci, 0)), wspec(lbp), wspec(o_norm),
                  wspec(tri), wspec(ones), wspec(sel)],
        out_specs=pl.BlockSpec((nb, CHUNK, HGRN_W), lambda bi, ci: (bi, ci, 0)),
        scratch_shapes=[pltpu.VMEM((nb * HGRN_HEADS, HGRN_D, HGRN_D), F32)],
        compiler_params=_cparams(("parallel", "arbitrary")),
        name="hgrn2",
    )(u, lbp, o_norm, tri, ones, sel)


def _rope_tables(seq):
    pos = jnp.arange(seq, dtype=F32)[:, None]

    def cs(half):
        inv = jnp.power(ROPE_THETA, -jnp.arange(half, dtype=F32) / half)
        ang = pos * inv[None, :]
        return jnp.cos(ang), jnp.sin(ang)

    cd, sd = cs(DIFF_HD // 2)
    cos_d = jnp.tile(jnp.concatenate([cd, cd], axis=1), (1, LANES // DIFF_HD))
    sin_d = jnp.tile(jnp.concatenate([-sd, sd], axis=1), (1, LANES // DIFF_HD))
    cm, sm = cs(MLA_ROPE // 2)
    one = jnp.ones((seq, MLA_NOPE), F32)
    tail = LANES - MLA_QK
    cos_m = jnp.concatenate([one, cm, cm, jnp.ones((seq, tail), F32)], axis=1)
    sin_m = jnp.concatenate([0 * one, -sm, sm, jnp.zeros((seq, tail), F32)], axis=1)
    return cos_d, sin_d, cos_m, sin_m


def _router_weights(w_group, b_group, w_expert, b_expert):
    d = w_group.shape[0]
    pad = LANES - MOE_GROUPS - MOE_EXPERTS
    w = jnp.concatenate([w_group, w_expert, jnp.zeros((d, pad), F32)], axis=1)
    bias = jnp.concatenate([b_group, b_expert, jnp.zeros((pad,), F32)])[None, :]
    hi = w.astype(BF16)
    lo = (w - hi.astype(F32)).astype(BF16)
    return hi, lo, bias


def kernel(x, norm_mix, norm_ffn, norm_final, ev_w_in, ev_w_out, mla_q_norm, mla_w_q_up, mla_kv_norm, mla_w_kv_up, diff_lambda, diff_subln, od_w_in, od_w_out, rwkv_mu, rwkv_w0, rwkv_w2, rwkv_a0, rwkv_a2, rwkv_g2, rwkv_k_k, rwkv_k_a, rwkv_r_k, rwkv_ln_w, rwkv_ln_b, hgrn_lb, hgrn_o_norm, moe_w_group, moe_b_group, moe_w_expert, moe_b_expert, moe_w_gate, moe_w_up, moe_w_down):
    b, s, d = x.shape
    n = b * s
    assert norm_mix.shape[0] == 2 and hgrn_lb.shape[0] == 2
    assert s % ATTN_BLOCK == 0 and s % ROW_TILE == 0 and ATTN_BLOCK % CHUNK == 0
    row2 = lambda t: t.reshape(1, -1)
    h = x.reshape(n, d)

    w_in = ev_w_in[0]
    o1, o2, o3 = MLA_LORA, 2 * MLA_LORA, 2 * MLA_LORA + MLA_ROPE
    kr_pad = jnp.zeros((d, LANES), F32).at[:, MLA_NOPE:MLA_QK].set(w_in[:, o2:o3])
    w0 = jnp.concatenate([w_in[:, :o2], kr_pad, w_in[:, o3:]], axis=1).astype(BF16)
    wq = mla_w_q_up[0].reshape(MLA_LORA, MLA_HEADS, MLA_QK)
    wq = jnp.pad(wq, ((0, 0), (0, 0), (0, LANES - MLA_QK))).reshape(MLA_LORA, MLA_HEADS * LANES).astype(BF16)
    wkv = mla_w_kv_up[0].reshape(MLA_LORA, MLA_HEADS, MLA_NOPE + MLA_V)
    wk = jnp.pad(wkv[:, :, :MLA_NOPE], ((0, 0), (0, 0), (0, LANES - MLA_NOPE)))
    wk = wk.reshape(MLA_LORA, MLA_HEADS * LANES).astype(BF16)
    wv = wkv[:, :, MLA_NOPE:].reshape(MLA_LORA, MLA_HEADS * MLA_V).astype(BF16)
    cos_d, sin_d, cos_m, sin_m = _rope_tables(s)
    qf, kf, vm, dq, dk, dv = _ev_in(h, row2(norm_mix[0]), w0, row2(mla_q_norm[0]), row2(mla_kv_norm[0]),
                                    wq, wk, wv, cos_d, sin_d, cos_m, sin_m, s)
    r3 = lambda t: t.reshape(b, s, t.shape[-1])
    o_mla = _mla_attn(r3(qf), r3(kf), vm)
    lam_init = 0.8 - 0.6 * math.exp(-0.3 * 0)
    o_diff = _diff_attn(r3(dq), r3(dk), dv, diff_lambda[0], row2(diff_subln[0]), lam_init)
    rwh, rwl, rb = _router_weights(moe_w_group[0], moe_b_group[0], moe_w_expert[0], moe_b_expert[0])
    ti = jnp.arange(ROW_TILE)
    tri_rows = (ti[None, :] < ti[:, None]).astype(BF16)
    h, hf, route, ridx, counts = _mix_out(h, o_mla.reshape(n, -1), o_diff.reshape(n, -1), ev_w_out[0].astype(BF16),
                                    row2(norm_ffn[0]), rwh, rwl, rb, tri_rows)
    h = _moe(h, hf, route, ridx, counts, moe_w_gate, moe_w_up, moe_w_down, 0, row2(norm_final), False)

    ur, uh = _od_in(h, row2(norm_mix[1]), od_w_in[0].astype(BF16))
    zpad = jnp.zeros((RWKV_DECAY_LORA, RWKV_W), F32)
    w2p = jnp.concatenate([rwkv_w2[0], zpad], axis=0).astype(BF16)
    a2p = jnp.concatenate([zpad, rwkv_a2[0]], axis=0).astype(BF16)
    ci = jnp.arange(CHUNK)
    tri = (ci[None, :] <= ci[:, None]).astype(BF16)
    li = jnp.arange(LANES) // RWKV_HD
    seg = (li[:, None] == li[None, :]).astype(BF16)
    o_c = _rwkv(ur.reshape(b, s, RWKV_IN), row2(rwkv_mu[0]), row2(rwkv_w0[0]), w2p, row2(rwkv_a0[0]), a2p,
                rwkv_g2[0].astype(BF16), row2(rwkv_k_k[0]), row2(rwkv_k_a[0]), row2(rwkv_r_k[0]),
                row2(rwkv_ln_w[0]), row2(rwkv_ln_b[0]),
                jnp.kron(jnp.eye(math.gcd(b, RWKV_SEQS), dtype=BF16), tri), seg)
    pt = jnp.arange(SUB * SUB) // SUB
    ps = jnp.arange(SUB * SUB) % SUB
    sel = ((pt[None, :] == jnp.arange(SUB)[:, None]) & (ps <= pt)[None, :]).astype(BF16)
    o_d = _hgrn(uh.reshape(b, s, HGRN_IN), hgrn_lb, row2(hgrn_o_norm[0]),
                jnp.kron(jnp.eye(math.gcd(b, HGRN_SEQS), dtype=BF16), tri), jnp.ones((LANES, LANES), BF16), sel)
    rwh, rwl, rb = _router_weights(moe_w_group[1], moe_b_group[1], moe_w_expert[1], moe_b_expert[1])
    h, hf, route, ridx, counts = _mix_out(h, o_c.reshape(n, -1), o_d.reshape(n, -1), od_w_out[0].astype(BF16),
                                    row2(norm_ffn[1]), rwh, rwl, rb, tri_rows)
    out = _moe(h, hf, route, ridx, counts, moe_w_gate, moe_w_up, moe_w_down, 1, row2(norm_final), True)
    return out.reshape(b, s, d)
```

```python
import functools
import math

import jax
import jax.numpy as jnp
from jax import lax
from jax.experimental import pallas as pl
from jax.experimental.pallas import tpu as pltpu

F32 = jnp.float32
BF16 = jnp.bfloat16

CHUNK = 64
ROPE_THETA = 10000.0
NORM_EPS = 1e-6
MLA_HEADS = 8
MLA_LORA = 256
MLA_NOPE = 64
MLA_ROPE = 32
MLA_V = 64
MLA_QK = MLA_NOPE + MLA_ROPE
DIFF_HEADS = 4
DIFF_HD = 64
DIFF_V = 2 * DIFF_HD
DIFF_W = DIFF_HEADS * 2 * DIFF_HD
RWKV_HEADS = 8
RWKV_HD = 64
RWKV_W = RWKV_HEADS * RWKV_HD
RWKV_DECAY_LORA = 64
RWKV_A_LORA = 64
RWKV_GATE_LORA = 128
RWKV_IN = 3 * RWKV_W + RWKV_DECAY_LORA + RWKV_A_LORA + RWKV_GATE_LORA
RWKV_GN_EPS = 64e-5
HGRN_HEADS = 4
HGRN_D = 128
HGRN_W = HGRN_HEADS * HGRN_D
HGRN_IN = 4 * HGRN_W
MOE_GROUPS = 4
MOE_EPG = 8
MOE_EXPERTS = MOE_GROUPS * MOE_EPG
MOE_FF = 512

LANES = 128
ROW_TILE = 512
ATTN_BLOCK = 512
EXPERT_TILE = 512
SUB = 16
RWKV_SEQS = 4
HGRN_SEQS = 2
VMEM_LIMIT = 48 * 1024 * 1024
NEG = -1e30
LOG2E = math.log2(math.e)


def _cparams(sem):
    return pltpu.CompilerParams(dimension_semantics=sem, vmem_limit_bytes=VMEM_LIMIT)


def _dot(a, b):
    return jnp.dot(a, b, preferred_element_type=F32)


def _dot_nt(a, b):
    return lax.dot_general(a, b, (((1,), (1,)), ((), ())), preferred_element_type=F32)


def _dot_tn(a, b):
    return lax.dot_general(a, b, (((0,), (0,)), ((), ())), preferred_element_type=F32)


def _rms(x, g):
    return x * lax.rsqrt(jnp.mean(x * x, axis=-1, keepdims=True) + NORM_EPS) * g


def _sigmoid(x):
    return 1.0 / (1.0 + jnp.exp(-x))


def _split_dot(x, w_bf16, left=False):
    hi = x.astype(BF16)
    lo = (x - hi.astype(F32)).astype(BF16)
    if left:
        return _dot(w_bf16, hi) + _dot(w_bf16, lo)
    return _dot(hi, w_bf16) + _dot(lo, w_bf16)


def _tile_lanes(t, n):
    return jnp.concatenate([t] * n, axis=-1) if n > 1 else t


def _rope_rot(x, half):
    w = x.shape[-1]
    lane = lax.broadcasted_iota(jnp.int32, x.shape, x.ndim - 1)
    first = (lane & (2 * half - 1)) < half
    return jnp.where(first, pltpu.roll(x, w - half, x.ndim - 1), pltpu.roll(x, half, x.ndim - 1))


def _ev_in_kernel(x_ref, g_ref, w0_ref, qg_ref, kvg_ref, wq_ref, wk_ref, wv_ref,
                  cd_ref, sd_ref, cm_ref, sm_ref,
                  qf_ref, kf_ref, vm_ref, dq_ref, dk_ref, dv_ref):
    hn = _rms(x_ref[...], g_ref[...]).astype(BF16)
    u = _dot(hn, w0_ref[...])
    c_q = u[:, 0:256]
    c_kv = u[:, 256:512]
    kr = u[:, 512:640]
    dq = u[:, 640:1152]
    dk = u[:, 1152:1664]
    dv = u[:, 1664:2176]
    cqn = _rms(c_q, qg_ref[...]).astype(BF16)
    ckn = _rms(c_kv, kvg_ref[...]).astype(BF16)
    q = _dot(cqn, wq_ref[...])
    k = _dot(ckn, wk_ref[...])
    v = _dot(ckn, wv_ref[...])
    cm = cm_ref[...]
    sm = sm_ref[...]
    m_half = MLA_ROPE // 2
    lane = lax.broadcasted_iota(jnp.int32, q.shape, 1) & (LANES - 1)
    first = (lane >= MLA_NOPE) & (lane < MLA_NOPE + m_half)
    rot_q = jnp.where(first, pltpu.roll(q, q.shape[1] - m_half, 1), pltpu.roll(q, m_half, 1))
    q = (q * _tile_lanes(cm, MLA_HEADS) + rot_q * _tile_lanes(sm, MLA_HEADS)) * (MLA_QK ** -0.5 * LOG2E)
    lane1 = lax.broadcasted_iota(jnp.int32, kr.shape, 1)
    first1 = (lane1 >= MLA_NOPE) & (lane1 < MLA_NOPE + m_half)
    rot_k = jnp.where(first1, pltpu.roll(kr, LANES - m_half, 1), pltpu.roll(kr, m_half, 1))
    kr = kr * cm + rot_k * sm
    k = k + _tile_lanes(kr, MLA_HEADS)
    qf_ref[...] = q.astype(BF16)
    kf_ref[...] = k.astype(BF16)
    vm_ref[0, :, 0] = v.T.reshape(vm_ref.shape[1], LANES, v.shape[0]).astype(BF16)
    cd = _tile_lanes(cd_ref[...], DIFF_W // LANES)
    sd = _tile_lanes(sd_ref[...], DIFF_W // LANES)
    dq = (dq * cd + _rope_rot(dq, DIFF_HD // 2) * sd) * (DIFF_HD ** -0.5 * LOG2E)
    dk = dk * cd + _rope_rot(dk, DIFF_HD // 2) * sd
    dq_ref[...] = dq.astype(BF16)
    dk_ref[...] = dk.astype(BF16)
    dv_ref[0, :, 0] = dv.T.reshape(dv_ref.shape[1], LANES, dv.shape[0]).astype(BF16)


def _ev_in(x2, g, w0, qg, kvg, wq, wk, wv, cd, sd, cm, sm, seq):
    n, d = x2.shape
    tm = ATTN_BLOCK
    nt = seq // tm
    row = lambda i: (i, 0)
    full = lambda i: (0, 0)
    tab = lambda i: (i % nt, 0)
    wspec = lambda a: pl.BlockSpec(a.shape, full)
    vt_shape = lambda w: jax.ShapeDtypeStruct((n // seq, w // LANES, nt, LANES, tm), BF16)
    vt_spec = lambda w: pl.BlockSpec((1, w // LANES, 1, LANES, tm), lambda i: (i // nt, 0, i % nt, 0, 0))
    rows = lambda w: jax.ShapeDtypeStruct((n, w), BF16)
    rspec = lambda w: pl.BlockSpec((tm, w), row)
    wide = MLA_HEADS * LANES
    return pl.pallas_call(
        _ev_in_kernel,
        out_shape=[rows(wide), rows(wide), vt_shape(MLA_HEADS * MLA_V), rows(DIFF_W), rows(DIFF_W), vt_shape(DIFF_W)],
        grid=(n // tm,),
        in_specs=[pl.BlockSpec((tm, d), row), wspec(g), wspec(w0), wspec(qg), wspec(kvg), wspec(wq), wspec(wk),
                  wspec(wv)] + [pl.BlockSpec((tm, LANES), tab)] * 4,
        out_specs=[rspec(wide), rspec(wide), vt_spec(MLA_HEADS * MLA_V), rspec(DIFF_W), rspec(DIFF_W),
                   vt_spec(DIFF_W)],
        compiler_params=_cparams(("parallel",)),
        name="ev_in",
    )(x2, g, w0, qg, kvg, wq, wk, wv, cd, sd, cm, sm)


ONES_ROWS = 16


def _chunk_mask_t(shape):
    sh = int(math.log2(CHUNK))
    key_chunk = jnp.right_shift(lax.broadcasted_iota(jnp.int32, shape, 0), sh)
    q_chunk = jnp.right_shift(lax.broadcasted_iota(jnp.int32, shape, 1), sh)
    return key_chunk <= q_chunk


def _attn_streams(i, qs, key_of, val_of, k_ref, vt_ref, m_ref, acc_ref):
    bk = ATTN_BLOCK
    n = len(qs)
    m_ref[...] = jnp.full(m_ref.shape, NEG, F32)
    acc_ref[...] = jnp.zeros(acc_ref.shape, F32)
    dv = acc_ref.shape[1] - ONES_ROWS
    ones = jnp.ones((ONES_ROWS, bk), BF16)

    def step(j, masked):
        kb = k_ref[0, pl.ds(pl.multiple_of(j * bk, bk), bk), :]
        vt = vt_ref[0, :, j]
        st = [_dot_nt(key_of(kb, s), qs[s]) for s in range(n)]
        if masked:
            mask = _chunk_mask_t(st[0].shape)
            st = [jnp.where(mask, x, NEG) for x in st]
        m_prev = [m_ref[s] for s in range(n)]
        m_new = [jnp.maximum(m_prev[s], jnp.max(st[s], axis=0, keepdims=True)) for s in range(n)]
        p = [jnp.exp2(st[s] - m_new[s]).astype(BF16) for s in range(n)]
        pv = [_dot(jnp.concatenate([val_of(vt, s), ones], axis=0), p[s]) for s in range(n)]
        for s in range(n):
            acc_ref[s] = jnp.exp2(m_prev[s] - m_new[s]) * acc_ref[s] + pv[s]
            m_ref[s] = m_new[s]

    def body(j, carry):
        step(j, False)
        return carry

    lax.fori_loop(0, i, body, 0)
    step(i, True)
    accs = [acc_ref[s] for s in range(n)]
    return [a[:dv] / a[dv:dv + 1] for a in accs]


def _mla_attn_kernel(q_ref, k_ref, vt_ref, o_ref, m_ref, acc_ref):
    q = q_ref[0]
    n = q.shape[1] // LANES
    qs = [q[:, s * LANES:(s + 1) * LANES] for s in range(n)]
    key_of = lambda kb, s: kb[:, s * LANES:(s + 1) * LANES]
    val_of = lambda vt, s: vt[s // 2, (s % 2) * MLA_V:(s % 2 + 1) * MLA_V]
    outs = _attn_streams(pl.program_id(2), qs, key_of, val_of, k_ref, vt_ref, m_ref, acc_ref)
    o_ref[0] = jnp.concatenate(outs, axis=0).T.astype(BF16)


def _diff_attn_kernel(q_ref, k_ref, vt_ref, lam_ref, sub_ref, o_ref, m_ref, acc_ref, *, lam_init):
    q = q_ref[0]
    nh = q.shape[1] // LANES
    lane = lax.broadcasted_iota(jnp.int32, (q.shape[0], LANES), 1)
    qs = []
    for h in range(nh):
        q_h = q[:, h * LANES:(h + 1) * LANES]
        zero = jnp.zeros_like(q_h)
        qs += [jnp.where(lane < DIFF_HD, q_h, zero), jnp.where(lane >= DIFF_HD, q_h, zero)]
    key_of = lambda kb, s: kb[:, (s // 2) * LANES:(s // 2 + 1) * LANES]
    val_of = lambda vt, s: vt[s // 2]
    outs = _attn_streams(pl.program_id(2), qs, key_of, val_of, k_ref, vt_ref, m_ref, acc_ref)
    lam = lam_ref[...]
    s1 = jnp.sum(lam[0:1] * lam[1:2], axis=-1, keepdims=True)
    s2 = jnp.sum(lam[2:3] * lam[3:4], axis=-1, keepdims=True)
    lam_full = jnp.exp(s1) - jnp.exp(s2) + lam_init
    for h in range(nh):
        o = (outs[2 * h] - lam_full * outs[2 * h + 1]).T
        o_ref[0, :, h * LANES:(h + 1) * LANES] = (_rms(o, sub_ref[...]) * (1.0 - lam_init)).astype(BF16)


ATTN_STREAMS = 4


def _attn_scratch(bq, dv):
    return [pltpu.VMEM((ATTN_STREAMS, 1, bq), F32), pltpu.VMEM((ATTN_STREAMS, dv + ONES_ROWS, bq), F32)]


def _vt_spec(nk, groups):
    return pl.BlockSpec((1, groups, nk, LANES, ATTN_BLOCK), lambda bi, h, i: (bi, h, 0, 0, 0))


def _mla_attn(qf, kf, vt):
    b, s, _ = qf.shape
    bq = ATTN_BLOCK
    w = ATTN_STREAMS * LANES
    return pl.pallas_call(
        _mla_attn_kernel,
        out_shape=jax.ShapeDtypeStruct((b, s, MLA_HEADS * MLA_V), BF16),
        grid=(b, MLA_HEADS // ATTN_STREAMS, s // bq),
        in_specs=[pl.BlockSpec((1, bq, w), lambda bi, h, i: (bi, i, h)),
                  pl.BlockSpec((1, s, w), lambda bi, h, i: (bi, 0, h)),
                  _vt_spec(s // bq, ATTN_STREAMS // 2)],
        out_specs=pl.BlockSpec((1, bq, ATTN_STREAMS * MLA_V), lambda bi, h, i: (bi, i, h)),
        scratch_shapes=_attn_scratch(bq, MLA_V),
        compiler_params=_cparams(("parallel", "parallel", "arbitrary")),
        name="mla_attn",
    )(qf, kf, vt)


def _diff_attn(dq, dk, dvt, lam, subln, lam_init):
    b, s, _ = dq.shape
    bq = ATTN_BLOCK
    nh = ATTN_STREAMS // 2
    blk = lambda rows, im: pl.BlockSpec((1, rows, nh * LANES), im)
    return pl.pallas_call(
        functools.partial(_diff_attn_kernel, lam_init=lam_init),
        out_shape=jax.ShapeDtypeStruct((b, s, DIFF_HEADS * DIFF_V), BF16),
        grid=(b, DIFF_HEADS // nh, s // bq),
        in_specs=[blk(bq, lambda bi, h, i: (bi, i, h)), blk(s, lambda bi, h, i: (bi, 0, h)),
                  _vt_spec(s // bq, nh),
                  pl.BlockSpec(lam.shape, lambda bi, h, i: (0, 0)),
                  pl.BlockSpec(subln.shape, lambda bi, h, i: (0, 0))],
        out_specs=blk(bq, lambda bi, h, i: (bi, i, h)),
        scratch_shapes=_attn_scratch(bq, DIFF_V),
        compiler_params=_cparams(("parallel", "parallel", "arbitrary")),
        name="diff_attn",
    )(dq, dk, dvt, lam, subln)


def _mix_out_kernel(h_ref, a_ref, b_ref, w_ref, g_ref, rwh_ref, rwl_ref, rb_ref, tri_ref,
                    hout_ref, hf_ref, route_ref, idx_ref, cnt_ref):
    mixed = jnp.concatenate([a_ref[...], b_ref[...]], axis=-1)
    h = h_ref[...] + _dot(mixed, w_ref[...])
    hout_ref[...] = h
    hf = _rms(h, g_ref[...])
    hi = hf.astype(BF16)
    hf_ref[...] = hi
    lo = (hf - hi.astype(F32)).astype(BF16)
    z = _dot(hi, rwh_ref[...]) + _dot(lo, rwh_ref[...]) + _dot(hi, rwl_ref[...]) + rb_ref[...]
    lane_i = lax.broadcasted_iota(jnp.int32, z.shape, 1)
    lane = lane_i.astype(F32)
    big = float(LANES)
    is_g = lane_i < MOE_GROUPS
    zg = jnp.where(is_g, z, NEG)
    mg = jnp.max(zg, axis=-1, keepdims=True)
    g_idx = jnp.min(jnp.where(zg == mg, lane, big), axis=-1, keepdims=True)
    pg_top = 1.0 / jnp.sum(jnp.where(is_g, jnp.exp(zg - mg), 0.0), axis=-1, keepdims=True)
    grp_of_lane = jnp.right_shift(lane_i - MOE_GROUPS, int(math.log2(MOE_EPG))).astype(F32)
    in_grp = (lane_i >= MOE_GROUPS) & (lane_i < MOE_GROUPS + MOE_EXPERTS) & (grp_of_lane == g_idx)
    ze = jnp.where(in_grp, z, NEG)
    m1 = jnp.max(ze, axis=-1, keepdims=True)
    i1 = jnp.min(jnp.where(ze == m1, lane, big), axis=-1, keepdims=True)
    ze2 = jnp.where(lane == i1, NEG, ze)
    m2 = jnp.max(ze2, axis=-1, keepdims=True)
    i2 = jnp.min(jnp.where(ze2 == m2, lane, big), axis=-1, keepdims=True)
    r = jnp.exp(m2 - m1)
    w1 = 1.0 / (1.0 + r)
    w2 = r / (1.0 + r)
    e1 = i1 - float(MOE_GROUPS)
    e2 = i2 - float(MOE_GROUPS)
    @pl.when(pl.program_id(0) == 0)
    def _():
        cnt_ref[...] = jnp.zeros(cnt_ref.shape, F32)

    both = jnp.where((lane == i1) | (lane == i2), 1.0, 0.0)
    before = _dot(tri_ref[...], both.astype(BF16)) + cnt_ref[0:1, :]
    rank1 = jnp.sum(jnp.where(lane == i1, before, 0.0), axis=-1, keepdims=True)
    rank2 = jnp.sum(jnp.where(lane == i2, before, 0.0), axis=-1, keepdims=True)
    cnt_ref[...] = cnt_ref[...] + jnp.sum(both, axis=0, keepdims=True)
    vals = (e1, e2, pg_top * w1, pg_top * w2, rank1, rank2)
    out = jnp.zeros(z.shape, F32)
    for col, val in enumerate(vals):
        out = jnp.where(lane_i == col, val, out)
    route_ref[...] = out
    idx_ref[0] = out.T[0:8].astype(jnp.int32)


def _mix_out(h, a, b, w, g, rwh, rwl, rb, tri):
    n, d = h.shape
    tm = ROW_TILE
    row = lambda i: (i, 0)
    full = lambda i: (0, 0)
    wspec = lambda t: pl.BlockSpec(t.shape, full)
    return pl.pallas_call(
        _mix_out_kernel,
        out_shape=[jax.ShapeDtypeStruct((n, d), F32), jax.ShapeDtypeStruct((n, d), BF16),
                   jax.ShapeDtypeStruct((n, LANES), F32), jax.ShapeDtypeStruct((n // tm, 8, tm), jnp.int32),
                   jax.ShapeDtypeStruct((8, LANES), F32)],
        grid=(n // tm,),
        in_specs=[pl.BlockSpec((tm, d), row), pl.BlockSpec((tm, a.shape[1]), row), pl.BlockSpec((tm, b.shape[1]), row),
                  wspec(w), wspec(g), wspec(rwh), wspec(rwl), wspec(rb), wspec(tri)],
        out_specs=[pl.BlockSpec((tm, d), row), pl.BlockSpec((tm, d), row), pl.BlockSpec((tm, LANES), row),
                   pl.BlockSpec((1, 8, tm), lambda i: (i, 0, 0)), pl.BlockSpec((8, LANES), full)],
        compiler_params=_cparams(("arbitrary",)),
        name="mix_out",
    )(h, a, b, w, g, rwh, rwl, rb, tri)


def _expert_kernel(te_ref, nv_ref, x_ref, wg_ref, wu_ref, wd_ref, y_ref, wgb_ref, wub_ref, wdb_ref):
    t = pl.program_id(0)

    @pl.when((t == 0) | (te_ref[t] != te_ref[jnp.maximum(t - 1, 0)]))
    def _():
        wgb_ref[...] = wg_ref[0, 0].astype(BF16)
        wub_ref[...] = wu_ref[0, 0].astype(BF16)
        wdb_ref[...] = wd_ref[0, 0].astype(BF16)

    @pl.when(t < nv_ref[0])
    def _():
        x = x_ref[...]
        a = _dot(x, wgb_ref[...])
        up = _dot(x, wub_ref[...])
        act = (a * _sigmoid(a) * up).astype(BF16)
        y_ref[...] = _dot(act, wdb_ref[...]).astype(BF16)

    @pl.when(t >= nv_ref[0])
    def _():
        y_ref[...] = jnp.zeros(y_ref.shape, BF16)


def _experts(xs, tile_expert, n_valid, wg, wu, wd, layer):
    p, d = xs.shape
    tm = EXPERT_TILE
    ff = wg.shape[3]
    grid_spec = pltpu.PrefetchScalarGridSpec(
        num_scalar_prefetch=2,
        grid=(p // tm,),
        in_specs=[pl.BlockSpec((tm, d), lambda t, te, nv: (t, 0)),
                  pl.BlockSpec((1, 1, d, ff), lambda t, te, nv: (layer, te[t], 0, 0)),
                  pl.BlockSpec((1, 1, d, ff), lambda t, te, nv: (layer, te[t], 0, 0)),
                  pl.BlockSpec((1, 1, ff, d), lambda t, te, nv: (layer, te[t], 0, 0))],
        out_specs=pl.BlockSpec((tm, d), lambda t, te, nv: (t, 0)),
        scratch_shapes=[pltpu.VMEM((d, ff), BF16), pltpu.VMEM((d, ff), BF16), pltpu.VMEM((ff, d), BF16)],
    )
    return pl.pallas_call(
        _expert_kernel,
        out_shape=jax.ShapeDtypeStruct((p, d), BF16),
        grid_spec=grid_spec,
        compiler_params=_cparams(("arbitrary",)),
        name="experts",
    )(tile_expert, n_valid, xs, wg, wu, wd)


def _add_experts(h_ref, y1_ref, y2_ref, route_ref):
    route = route_ref[...]
    return h_ref[...] + route[:, 2:3] * y1_ref[...].astype(F32) + route[:, 3:4] * y2_ref[...].astype(F32)


def _final_kernel(h_ref, y1_ref, y2_ref, route_ref, g_ref, o_ref):
    o_ref[...] = _rms(_add_experts(h_ref, y1_ref, y2_ref, route_ref), g_ref[...])


def _final(h, y1, y2, route, g):
    n, d = h.shape
    tm = ROW_TILE
    row = lambda i: (i, 0)
    return pl.pallas_call(
        _final_kernel,
        out_shape=jax.ShapeDtypeStruct((n, d), F32),
        grid=(n // tm,),
        in_specs=[pl.BlockSpec((tm, d), row), pl.BlockSpec((tm, d), row), pl.BlockSpec((tm, d), row),
                  pl.BlockSpec((tm, LANES), row), pl.BlockSpec(g.shape, lambda i: (0, 0))],
        out_specs=pl.BlockSpec((tm, d), row),
        compiler_params=_cparams(("parallel",)),
        name="moe_final",
    )(h, y1, y2, route, g)


def _moe(hf, idx, counts, wg, wu, wd, layer):
    n, d = hf.shape
    tm = EXPERT_TILE
    ids = jnp.arange(MOE_EXPERTS, dtype=jnp.int32)
    counts = counts[0, MOE_GROUPS:MOE_GROUPS + MOE_EXPERTS].astype(jnp.int32)
    padded = ((counts + tm - 1) // tm) * tm
    ends = jnp.cumsum(padded)
    starts = ends - padded
    start_of = lambda e: jnp.sum(jnp.where(e[:, None] == ids[None, :], starts[None, :], 0), axis=-1)
    e1, e2 = idx[:, 0, :].reshape(n), idx[:, 1, :].reshape(n)
    dest1 = start_of(e1) + idx[:, 4, :].reshape(n)
    dest2 = start_of(e2) + idx[:, 5, :].reshape(n)
    n_rows = 2 * n + MOE_EXPERTS * tm
    tok = jnp.arange(n, dtype=jnp.int32)
    tok_for_row = (jnp.arange(n_rows, dtype=jnp.int32) % n).at[jnp.concatenate([dest1, dest2])].set(
        jnp.concatenate([tok, tok]), mode="promise_in_bounds", unique_indices=True)
    take = functools.partial(jnp.take, axis=0, mode="clip")
    tile_start = jnp.arange(n_rows // tm, dtype=jnp.int32) * tm
    tile_expert = jnp.minimum(jnp.sum((ends[None, :] <= tile_start[:, None]).astype(jnp.int32), axis=1),
                              MOE_EXPERTS - 1)
    n_valid = (ends[-1] // tm).astype(jnp.int32).reshape(1)
    xs = take(hf, tok_for_row)
    ys = _experts(xs, tile_expert, n_valid, wg, wu, wd, layer)
    return take(ys, dest1), take(ys, dest2)


def _od_in_kernel(h_ref, y1_ref, y2_ref, route_ref, g_ref, w_ref, hout_ref, ur_ref, uh_ref):
    h = _add_experts(h_ref, y1_ref, y2_ref, route_ref)
    hout_ref[...] = h
    hn = _rms(h, g_ref[...]).astype(BF16)
    u = _dot(hn, w_ref[...])
    ur_ref[...] = u[:, :RWKV_IN].astype(BF16)
    uh_ref[...] = u[:, RWKV_IN:].astype(BF16)


def _od_in(h, y1, y2, route, g, w):
    n, d = h.shape
    tm = ROW_TILE
    row = lambda i: (i, 0)
    rspec = lambda width: pl.BlockSpec((tm, width), row)
    return pl.pallas_call(
        _od_in_kernel,
        out_shape=[jax.ShapeDtypeStruct((n, d), F32), jax.ShapeDtypeStruct((n, RWKV_IN), BF16),
                   jax.ShapeDtypeStruct((n, HGRN_IN), BF16)],
        grid=(n // tm,),
        in_specs=[rspec(d), rspec(d), rspec(d), rspec(LANES), pl.BlockSpec(g.shape, lambda i: (0, 0)),
                  pl.BlockSpec(w.shape, lambda i: (0, 0))],
        out_specs=[rspec(d), rspec(RWKV_IN), rspec(HGRN_IN)],
        compiler_params=_cparams(("parallel",)),
        name="od_in",
    )(h, y1, y2, route, g, w)


def _rwkv_kernel(u_ref, mu_ref, w0_ref, w2_ref, a0_ref, a2_ref, g2_ref, kk_ref, ka_ref, rk_ref,
                 lnw_ref, lnb_ref, tri_ref, seg_ref, o_ref, prev_ref, ht_ref):
    c = pl.program_id(1)

    @pl.when(c == 0)
    def _():
        prev_ref[...] = jnp.zeros(prev_ref.shape, F32)
        ht_ref[...] = jnp.zeros(ht_ref.shape, F32)

    nb = u_ref.shape[0]
    u = u_ref[...].reshape(nb * CHUNK, RWKV_IN).astype(F32)
    rows = lax.broadcasted_iota(jnp.int32, u.shape, 0)
    u_prev = pltpu.roll(u, 1, 0)
    for bi in range(nb):
        u_prev = jnp.where(rows == bi * CHUNK, prev_ref[bi], u_prev)
        prev_ref[bi] = u[(bi + 1) * CHUNK - 1:(bi + 1) * CHUNK, :]
    us = u + (u_prev - u) * mu_ref[...]
    w = RWKV_W
    r = us[:, 0:w]
    k = us[:, w:2 * w]
    v = us[:, 2 * w:3 * w]
    x12 = us[:, 3 * w:3 * w + LANES]
    xg = us[:, 3 * w + LANES:]
    seg_pair = seg_ref[...]

    def seg(x):
        return jnp.concatenate([_split_dot(x[:, p * LANES:(p + 1) * LANES], seg_pair)
                                for p in range(RWKV_HEADS // 2)], axis=1)

    tri = tri_ref[...]

    wl = w0_ref[...] + _dot(jnp.tanh(x12).astype(BF16), w2_ref[...])
    nwl = -wl
    softplus = jnp.maximum(nwl, 0.0) + jnp.log1p(jnp.exp(-jnp.abs(nwl)))
    lw = -jnp.exp(-softplus - 0.5)
    a = _sigmoid(a0_ref[...] + _dot(x12.astype(BF16), a2_ref[...]))
    g = _dot(_sigmoid(xg).astype(BF16), g2_ref[...])
    kk = k * kk_ref[...]
    kk = kk * lax.rsqrt(jnp.maximum(seg(kk * kk), 1e-24))
    k2 = k * (1.0 + (a - 1.0) * ka_ref[...])
    a_in = -kk
    b_in = kk * a

    lg = _split_dot(lw, tri, left=True)
    rbs = [slice(bi * CHUNK, (bi + 1) * CHUNK) for bi in range(nb)]
    lg_c = [lg[(bi + 1) * CHUNK - 1:(bi + 1) * CHUNK, :] for bi in range(nb)]
    e_neg = jnp.exp(-lg)
    e_rel = jnp.concatenate([jnp.exp(lg_c[bi] - lg[rbs[bi]]) for bi in range(nb)], axis=0)
    g_c = [jnp.exp(x) for x in lg_c]
    at = a_in * jnp.exp(lg - lw)
    rt = r * jnp.exp(lg)
    kt = k2 * e_neg
    bt = b_in * e_neg
    kh = k2 * e_rel
    bh = b_in * e_rel

    c2 = 2 * CHUNK
    ri = lax.broadcasted_iota(jnp.int32, (c2, c2), 0)
    ci = lax.broadcasted_iota(jnp.int32, (c2, c2), 1)
    sh = int(math.log2(CHUNK))
    same = jnp.right_shift(ri, sh) == jnp.right_shift(ci, sh)
    strict = same & (ri > ci)
    incl = same & (ri >= ci)
    eye = jnp.where(ri == ci, 1.0, 0.0).astype(F32)
    lane = lax.broadcasted_iota(jnp.int32, (CHUNK, LANES), 1)
    lo_half = lane < RWKV_HD

    def bd(x):
        return jnp.concatenate([jnp.where(lo_half, x, 0.0), jnp.where(lo_half, 0.0, x)], axis=0)

    npair = RWKV_HEADS // 2
    pairs = range(nb * npair)
    sls = [(rbs[c // npair], slice((c % npair) * LANES, (c % npair + 1) * LANES)) for c in pairs]
    bdb = lambda x: [bd(x[sl]).astype(BF16) for sl in sls]
    at_b = [bd(at[sl]) for sl in sls]
    rt_b = [bd(rt[sl]) for sl in sls]
    kt_b, bt_b, kh_b, bh_b, v_b = bdb(kt), bdb(bt), bdb(kh), bdb(bh), bdb(v)
    hts = [ht_ref[p] for p in pairs]
    htb = [t.astype(BF16) for t in hts]
    sc = [_dot_nt(jnp.concatenate([at_b[p], rt_b[p]], axis=0).astype(BF16),
                  jnp.concatenate([kt_b[p], bt_b[p]], axis=0)) for p in pairs]
    a_ak = [jnp.where(strict, s[:c2, :c2], 0.0).astype(BF16) for s in sc]
    a_ab = [jnp.where(strict, s[:c2, c2:], 0.0) for s in sc]
    a_rk = [jnp.where(incl, s[c2:, :c2], 0.0).astype(BF16) for s in sc]
    a_rb = [jnp.where(incl, s[c2:, c2:], 0.0).astype(BF16) for s in sc]
    x2 = [_dot(a_ak[p], v_b[p]) for p in pairs]
    e1 = [_dot(a_rk[p], v_b[p]) for p in pairs]
    d1 = [_dot_tn(v_b[p], kh_b[p]) for p in pairs]
    pw = a_ab
    t_inv = [eye + a for a in a_ab]
    for _ in range(int(math.log2(CHUNK)) - 1):
        pwb = [x.astype(BF16) for x in pw]
        pw = [_dot(x, x) for x in pwb]
        t_inv = [t_inv[p] + _dot(t_inv[p].astype(BF16), pw[p].astype(BF16)) for p in pairs]
    txb = [_dot(t_inv[p].astype(BF16), jnp.concatenate([at_b[p], x2[p]], axis=1).astype(BF16)).astype(BF16)
           for p in pairs]
    qe = [_dot(a_rb[p], txb[p]) for p in pairs]
    gd = [_dot_tn(txb[p], bh_b[p]) for p in pairs]
    y_b = [_dot_nt((rt_b[p] + qe[p][:, :LANES]).astype(BF16), htb[p]) + e1[p] + qe[p][:, LANES:] for p in pairs]
    for p in pairs:
        upd = hts[p] * g_c[p // npair][:, sls[p][1]] + _dot(htb[p], gd[p][:LANES].astype(BF16)) + d1[p] + gd[p][LANES:]
        ht_ref[p] = jnp.where(same, upd, 0.0)
    y_p = [t[:CHUNK] + t[CHUNK:] for t in y_b]
    y = jnp.concatenate([jnp.concatenate(y_p[bi * npair:(bi + 1) * npair], axis=1) for bi in range(nb)],
                        axis=0)

    inv_n = 1.0 / RWKV_HD
    mean = seg(y) * inv_n
    dlt = y - mean
    var = seg(dlt * dlt) * inv_n
    yn = dlt * lax.rsqrt(var + RWKV_GN_EPS) * lnw_ref[...] + lnb_ref[...]
    bonus = seg(r * k2 * rk_ref[...]) * v
    o_ref[...] = ((yn + bonus) * g).astype(BF16).reshape(nb, CHUNK, RWKV_W)


def _rwkv(u, mu, w0, w2p, a0, a2p, g2, k_k, k_a, r_k, ln_w, ln_b, tri, seg):
    b, s, _ = u.shape
    nb = tri.shape[0] // CHUNK
    full = lambda bi, ci: (0, 0)
    wspec = lambda t: pl.BlockSpec(t.shape, full)
    params = [mu, w0, w2p, a0, a2p, g2, k_k, k_a, r_k, ln_w, ln_b, tri, seg]
    return pl.pallas_call(
        _rwkv_kernel,
        out_shape=jax.ShapeDtypeStruct((b, s, RWKV_W), BF16),
        grid=(b // nb, s // CHUNK),
        in_specs=[pl.BlockSpec((nb, CHUNK, RWKV_IN), lambda bi, ci: (bi, ci, 0))] + [wspec(t) for t in params],
        out_specs=pl.BlockSpec((nb, CHUNK, RWKV_W), lambda bi, ci: (bi, ci, 0)),
        scratch_shapes=[pltpu.VMEM((nb, 1, RWKV_IN), F32), pltpu.VMEM((nb * RWKV_HEADS // 2, LANES, LANES), F32)],
        compiler_params=_cparams(("parallel", "arbitrary")),
        name="rwkv7",
    )(u, *params)


def _hgrn_kernel(u_ref, lb_ref, on_ref, tri_ref, ones_ref, sel_ref, o_ref, st_ref):
    c = pl.program_id(1)

    @pl.when(c == 0)
    def _():
        st_ref[...] = jnp.zeros(st_ref.shape, F32)

    nb = u_ref.shape[0]
    u = u_ref[...].reshape(nb * CHUNK, HGRN_IN).astype(F32)
    lbp = lb_ref[...]
    mx = jnp.maximum(lbp[0:1], lbp[1:2])
    e0 = jnp.exp(lbp[0:1] - mx)
    e1 = jnp.exp(lbp[1:2] - mx)
    p0 = e0 / (e0 + e1)
    p1 = e1 / (e0 + e1)
    lb = (p0 + p1) - p0
    tri = tri_ref[...]
    ones = ones_ref[...]
    sel = sel_ref[...]
    d = HGRN_D
    w = HGRN_W
    heads = range(HGRN_HEADS)
    sls = [slice(h * d, (h + 1) * d) for h in heads]
    q, z, iv, gt = u[:, :w], u[:, w:2 * w], u[:, 2 * w:3 * w], u[:, 3 * w:]
    qs = q * _sigmoid(q)
    log_sig = jnp.minimum(z, 0.0) - jnp.log1p(jnp.exp(-jnp.abs(z)))
    x1 = jnp.log(lb)
    x2 = jnp.log1p(-lb) + log_sig
    log_f = jnp.maximum(x1, x2) + jnp.log1p(jnp.exp(-jnp.abs(x1 - x2)))
    key = (1.0 - lb) * _sigmoid(-z)
    bc = _split_dot(log_f, tri, left=True)
    seqs = range(nb)
    rbs = [slice(i * CHUNK, (i + 1) * CHUNK) for i in seqs]
    b_l = [bc[(i + 1) * CHUNK - 1:(i + 1) * CHUNK] for i in seqs]
    ivb = iv.astype(BF16)
    qe = (qs * jnp.exp(bc)).astype(BF16)
    kl = (key * jnp.concatenate([jnp.exp(b_l[i] - bc[rbs[i]]) for i in seqs], axis=0)).astype(BF16)
    nh = HGRN_HEADS
    sts = [st_ref[c] for c in range(nb * nh)]
    o_inter = [_dot_nt(qe[rbs[c // nh], sls[c % nh]], sts[c].astype(BF16)) for c in range(nb * nh)]
    for c in range(nb * nh):
        i, sl = c // nh, sls[c % nh]
        st_ref[c] = sts[c] * jnp.exp(b_l[i])[:, sl] + _dot_tn(ivb[rbs[i], sl], kl[rbs[i], sl])

    nblk = CHUNK // SUB
    srow = lax.broadcasted_iota(jnp.int32, (SUB, w), 0)
    p_all, v_tile, sc_off = [], [], []
    for blk in range(nb * nblk):
        base = (blk // nblk) * CHUNK
        r0 = base + (blk % nblk) * SUB
        b_i, q_i, k_i = bc[r0:r0 + SUB], qs[r0:r0 + SUB], key[r0:r0 + SUB]
        rows = [q_i[t:t + 1] * k_i * jnp.where(srow <= t, jnp.exp(b_i[t:t + 1] - b_i), 0.0) for t in range(SUB)]
        p_all.append(jnp.concatenate(rows, axis=0).astype(BF16))
        v_tile.append(jnp.concatenate([iv[r0:r0 + SUB]] * SUB, axis=0))
        if r0 > base:
            b_m = bc[r0 - 1:r0]
            qp = (q_i * jnp.exp(b_i - b_m)).astype(BF16)
            kp = (key[base:r0] * jnp.exp(b_m - bc[base:r0])).astype(BF16)
            sc_off.append([_dot_nt(qp[:, sl], kp[:, sl]).astype(BF16) for sl in sls])
        else:
            sc_off.append(None)
    rs = [jnp.concatenate([_dot(p[:, sl], ones) for sl in sls], axis=1) for p in p_all]
    rv = [(rs[blk] * v_tile[blk]).astype(BF16) for blk in range(nb * nblk)]
    for c in range(nb * nh):
        i, sl = c // nh, sls[c % nh]
        parts = []
        for blk in range(i * nblk, (i + 1) * nblk):
            o_i = _dot(sel, rv[blk][:, sl])
            if sc_off[blk] is not None:
                o_i = o_i + _dot(sc_off[blk][c % nh], ivb[i * CHUNK:i * CHUNK + (blk % nblk) * SUB, sl])
            parts.append(o_i)
        o = o_inter[c] + jnp.concatenate(parts, axis=0)
        on = _rms(o, on_ref[:, sl])
        g_h = gt[rbs[i], sl]
        o_ref[i, :, sl] = (on * (g_h * _sigmoid(g_h))).astype(BF16)


def _hgrn(u, lbp, o_norm, tri, ones, sel):
    b, s, _ = u.shape
    nb = tri.shape[0] // CHUNK
    full = lambda bi, ci: (0, 0)
    wspec = lambda t: pl.BlockSpec(t.shape, full)
    return pl.pallas_call(
        _hgrn_kernel,
        out_shape=jax.ShapeDtypeStruct((b, s, HGRN_W), BF16),
        grid=(b // nb, s // CHUNK),
        in_specs=[pl.BlockSpec((nb, CHUNK, HGRN_IN), lambda bi, ci: (bi, ci, 0)), wspec(lbp), wspec(o_norm),
                  wspec(tri), wspec(ones), wspec(sel)],
        out_specs=pl.BlockSpec((nb, CHUNK, HGRN_W), lambda bi, ci: (bi, ci, 0)),
        scratch_shapes=[pltpu.VMEM((nb * HGRN_HEADS, HGRN_D, HGRN_D), F32)],
        compiler_params=_cparams(("parallel", "arbitrary")),
        name="hgrn2",
    )(u, lbp, o_norm, tri, ones, sel)


def _rope_tables(seq):
    pos = jnp.arange(seq, dtype=F32)[:, None]

    def cs(half):
        inv = jnp.power(ROPE_THETA, -jnp.arange(half, dtype=F32) / half)
        ang = pos * inv[None, :]
        return jnp.cos(ang), jnp.sin(ang)

    cd, sd = cs(DIFF_HD // 2)
    cos_d = jnp.tile(jnp.concatenate([cd, cd], axis=1), (1, LANES // DIFF_HD))
    sin_d = jnp.tile(jnp.concatenate([-sd, sd], axis=1), (1, LANES // DIFF_HD))
    cm, sm = cs(MLA_ROPE // 2)
    one = jnp.ones((seq, MLA_NOPE), F32)
    tail = LANES - MLA_QK
    cos_m = jnp.concatenate([one, cm, cm, jnp.ones((seq, tail), F32)], axis=1)
    sin_m = jnp.concatenate([0 * one, -sm, sm, jnp.zeros((seq, tail), F32)], axis=1)
    return cos_d, sin_d, cos_m, sin_m


def _router_weights(w_group, b_group, w_expert, b_expert):
    d = w_group.shape[0]
    pad = LANES - MOE_GROUPS - MOE_EXPERTS
    w = jnp.concatenate([w_group, w_expert, jnp.zeros((d, pad), F32)], axis=1)
    bias = jnp.concatenate([b_group, b_expert, jnp.zeros((pad,), F32)])[None, :]
    hi = w.astype(BF16)
    lo = (w - hi.astype(F32)).astype(BF16)
    return hi, lo, bias


def kernel(x, norm_mix, norm_ffn, norm_final, ev_w_in, ev_w_out, mla_q_norm, mla_w_q_up, mla_kv_norm, mla_w_kv_up, diff_lambda, diff_subln, od_w_in, od_w_out, rwkv_mu, rwkv_w0, rwkv_w2, rwkv_a0, rwkv_a2, rwkv_g2, rwkv_k_k, rwkv_k_a, rwkv_r_k, rwkv_ln_w, rwkv_ln_b, hgrn_lb, hgrn_o_norm, moe_w_group, moe_b_group, moe_w_expert, moe_b_expert, moe_w_gate, moe_w_up, moe_w_down):
    b, s, d = x.shape
    n = b * s
    assert norm_mix.shape[0] == 2 and hgrn_lb.shape[0] == 2
    assert s % ATTN_BLOCK == 0 and s % ROW_TILE == 0 and ATTN_BLOCK % CHUNK == 0
    row2 = lambda t: t.reshape(1, -1)
    h = x.reshape(n, d)

    w_in = ev_w_in[0]
    o1, o2, o3 = MLA_LORA, 2 * MLA_LORA, 2 * MLA_LORA + MLA_ROPE
    kr_pad = jnp.zeros((d, LANES), F32).at[:, MLA_NOPE:MLA_QK].set(w_in[:, o2:o3])
    w0 = jnp.concatenate([w_in[:, :o2], kr_pad, w_in[:, o3:]], axis=1).astype(BF16)
    wq = mla_w_q_up[0].reshape(MLA_LORA, MLA_HEADS, MLA_QK)
    wq = jnp.pad(wq, ((0, 0), (0, 0), (0, LANES - MLA_QK))).reshape(MLA_LORA, MLA_HEADS * LANES).astype(BF16)
    wkv = mla_w_kv_up[0].reshape(MLA_LORA, MLA_HEADS, MLA_NOPE + MLA_V)
    wk = jnp.pad(wkv[:, :, :MLA_NOPE], ((0, 0), (0, 0), (0, LANES - MLA_NOPE)))
    wk = wk.reshape(MLA_LORA, MLA_HEADS * LANES).astype(BF16)
    wv = wkv[:, :, MLA_NOPE:].reshape(MLA_LORA, MLA_HEADS * MLA_V).astype(BF16)
    cos_d, sin_d, cos_m, sin_m = _rope_tables(s)
    qf, kf, vm, dq, dk, dv = _ev_in(h, row2(norm_mix[0]), w0, row2(mla_q_norm[0]), row2(mla_kv_norm[0]),
                                    wq, wk, wv, cos_d, sin_d, cos_m, sin_m, s)
    r3 = lambda t: t.reshape(b, s, t.shape[-1])
    o_mla = _mla_attn(r3(qf), r3(kf), vm)
    lam_init = 0.8 - 0.6 * math.exp(-0.3 * 0)
    o_diff = _diff_attn(r3(dq), r3(dk), dv, diff_lambda[0], row2(diff_subln[0]), lam_init)
    rwh, rwl, rb = _router_weights(moe_w_group[0], moe_b_group[0], moe_w_expert[0], moe_b_expert[0])
    ti = jnp.arange(ROW_TILE)
    tri_rows = (ti[None, :] < ti[:, None]).astype(BF16)
    h, hf, route, ridx, counts = _mix_out(h, o_mla.reshape(n, -1), o_diff.reshape(n, -1), ev_w_out[0].astype(BF16),
                                    row2(norm_ffn[0]), rwh, rwl, rb, tri_rows)
    y1, y2 = _moe(hf, ridx, counts, moe_w_gate, moe_w_up, moe_w_down, 0)

    h, ur, uh = _od_in(h, y1, y2, route, row2(norm_mix[1]), od_w_in[0].astype(BF16))
    zpad = jnp.zeros((RWKV_DECAY_LORA, RWKV_W), F32)
    w2p = jnp.concatenate([rwkv_w2[0], zpad], axis=0).astype(BF16)
    a2p = jnp.concatenate([zpad, rwkv_a2[0]], axis=0).astype(BF16)
    ci = jnp.arange(CHUNK)
    tri = (ci[None, :] <= ci[:, None]).astype(BF16)
    li = jnp.arange(LANES) // RWKV_HD
    seg = (li[:, None] == li[None, :]).astype(BF16)
    o_c = _rwkv(ur.reshape(b, s, RWKV_IN), row2(rwkv_mu[0]), row2(rwkv_w0[0]), w2p, row2(rwkv_a0[0]), a2p,
                rwkv_g2[0].astype(BF16), row2(rwkv_k_k[0]), row2(rwkv_k_a[0]), row2(rwkv_r_k[0]),
                row2(rwkv_ln_w[0]), row2(rwkv_ln_b[0]),
                jnp.kron(jnp.eye(math.gcd(b, RWKV_SEQS), dtype=BF16), tri), seg)
    pt = jnp.arange(SUB * SUB) // SUB
    ps = jnp.arange(SUB * SUB) % SUB
    sel = ((pt[None, :] == jnp.arange(SUB)[:, None]) & (ps <= pt)[None, :]).astype(BF16)
    o_d = _hgrn(uh.reshape(b, s, HGRN_IN), hgrn_lb, row2(hgrn_o_norm[0]),
                jnp.kron(jnp.eye(math.gcd(b, HGRN_SEQS), dtype=BF16), tri), jnp.ones((LANES, LANES), BF16), sel)
    rwh, rwl, rb = _router_weights(moe_w_group[1], moe_b_group[1], moe_w_expert[1], moe_b_expert[1])
    h, hf, route, ridx, counts = _mix_out(h, o_c.reshape(n, -1), o_d.reshape(n, -1), od_w_out[0].astype(BF16),
                                    row2(norm_ffn[1]), rwh, rwl, rb, tri_rows)
    y1, y2 = _moe(hf, ridx, counts, moe_w_gate, moe_w_up, moe_w_down, 1)
    out = _final(h, y1, y2, route, row2(norm_final))
    return out.reshape(b, s, d)
```

```python
import functools
import math

import jax
import jax.numpy as jnp
from jax import lax
from jax.experimental import pallas as pl
from jax.experimental.pallas import tpu as pltpu

F32 = jnp.float32
BF16 = jnp.bfloat16

CHUNK = 64
ROPE_THETA = 10000.0
NORM_EPS = 1e-6
MLA_HEADS = 8
MLA_LORA = 256
MLA_NOPE = 64
MLA_ROPE = 32
MLA_V = 64
MLA_QK = MLA_NOPE + MLA_ROPE
DIFF_HEADS = 4
DIFF_HD = 64
DIFF_V = 2 * DIFF_HD
DIFF_W = DIFF_HEADS * 2 * DIFF_HD
RWKV_HEADS = 8
RWKV_HD = 64
RWKV_W = RWKV_HEADS * RWKV_HD
RWKV_DECAY_LORA = 64
RWKV_A_LORA = 64
RWKV_GATE_LORA = 128
RWKV_IN = 3 * RWKV_W + RWKV_DECAY_LORA + RWKV_A_LORA + RWKV_GATE_LORA
RWKV_GN_EPS = 64e-5
HGRN_HEADS = 4
HGRN_D = 128
HGRN_W = HGRN_HEADS * HGRN_D
HGRN_IN = 4 * HGRN_W
MOE_GROUPS = 4
MOE_EPG = 8
MOE_EXPERTS = MOE_GROUPS * MOE_EPG
MOE_FF = 512

LANES = 128
ROW_TILE = 512
ATTN_BLOCK = 512
EXPERT_TILE = 512
SUB = 16
HALF = 8
RWKV_SEQS = 4
HGRN_SEQS = 4
VMEM_LIMIT = 48 * 1024 * 1024
NEG = -1e30
LOG2E = math.log2(math.e)


def _cparams(sem):
    return pltpu.CompilerParams(dimension_semantics=sem, vmem_limit_bytes=VMEM_LIMIT)


def _dot(a, b):
    return jnp.dot(a, b, preferred_element_type=F32)


def _dot_nt(a, b):
    return lax.dot_general(a, b, (((1,), (1,)), ((), ())), preferred_element_type=F32)


def _dot_tn(a, b):
    return lax.dot_general(a, b, (((0,), (0,)), ((), ())), preferred_element_type=F32)


def _rms(x, g):
    return x * lax.rsqrt(jnp.mean(x * x, axis=-1, keepdims=True) + NORM_EPS) * g


def _sigmoid(x):
    return 1.0 / (1.0 + jnp.exp(-x))


def _split_dot(x, w_bf16, left=False):
    hi = x.astype(BF16)
    lo = (x - hi.astype(F32)).astype(BF16)
    if left:
        return _dot(w_bf16, hi) + _dot(w_bf16, lo)
    return _dot(hi, w_bf16) + _dot(lo, w_bf16)


def _tile_lanes(t, n):
    return jnp.concatenate([t] * n, axis=-1) if n > 1 else t


def _rope_rot(x, half):
    w = x.shape[-1]
    lane = lax.broadcasted_iota(jnp.int32, x.shape, x.ndim - 1)
    first = (lane & (2 * half - 1)) < half
    return jnp.where(first, pltpu.roll(x, w - half, x.ndim - 1), pltpu.roll(x, half, x.ndim - 1))


def _ev_in_kernel(x_ref, g_ref, w0_ref, qg_ref, kvg_ref, wq_ref, wk_ref, wv_ref,
                  cd_ref, sd_ref, cm_ref, sm_ref,
                  qf_ref, kf_ref, vm_ref, dq_ref, dk_ref, dv_ref):
    hn = _rms(x_ref[...], g_ref[...]).astype(BF16)
    u = _dot(hn, w0_ref[...])
    c_q = u[:, 0:256]
    c_kv = u[:, 256:512]
    kr = u[:, 512:640]
    dq = u[:, 640:1152]
    dk = u[:, 1152:1664]
    dv = u[:, 1664:2176]
    cqn = _rms(c_q, qg_ref[...]).astype(BF16)
    ckn = _rms(c_kv, kvg_ref[...]).astype(BF16)
    q = _dot(cqn, wq_ref[...])
    k = _dot(ckn, wk_ref[...])
    v = _dot(ckn, wv_ref[...])
    cm = cm_ref[...]
    sm = sm_ref[...]
    m_half = MLA_ROPE // 2
    lane = lax.broadcasted_iota(jnp.int32, q.shape, 1) & (LANES - 1)
    first = (lane >= MLA_NOPE) & (lane < MLA_NOPE + m_half)
    rot_q = jnp.where(first, pltpu.roll(q, q.shape[1] - m_half, 1), pltpu.roll(q, m_half, 1))
    q = (q * _tile_lanes(cm, MLA_HEADS) + rot_q * _tile_lanes(sm, MLA_HEADS)) * (MLA_QK ** -0.5 * LOG2E)
    lane1 = lax.broadcasted_iota(jnp.int32, kr.shape, 1)
    first1 = (lane1 >= MLA_NOPE) & (lane1 < MLA_NOPE + m_half)
    rot_k = jnp.where(first1, pltpu.roll(kr, LANES - m_half, 1), pltpu.roll(kr, m_half, 1))
    kr = kr * cm + rot_k * sm
    k = k + _tile_lanes(kr, MLA_HEADS)
    qf_ref[...] = q.astype(BF16)
    kf_ref[...] = k.astype(BF16)
    vm_ref[0, :, 0] = v.T.reshape(vm_ref.shape[1], LANES, v.shape[0]).astype(BF16)
    cd = _tile_lanes(cd_ref[...], DIFF_W // LANES)
    sd = _tile_lanes(sd_ref[...], DIFF_W // LANES)
    dq = (dq * cd + _rope_rot(dq, DIFF_HD // 2) * sd) * (DIFF_HD ** -0.5 * LOG2E)
    dk = dk * cd + _rope_rot(dk, DIFF_HD // 2) * sd
    dq_ref[...] = dq.astype(BF16)
    dk_ref[...] = dk.astype(BF16)
    dv_ref[0, :, 0] = dv.T.reshape(dv_ref.shape[1], LANES, dv.shape[0]).astype(BF16)


def _ev_in(x2, g, w0, qg, kvg, wq, wk, wv, cd, sd, cm, sm, seq):
    n, d = x2.shape
    tm = ATTN_BLOCK
    nt = seq // tm
    row = lambda i: (i, 0)
    full = lambda i: (0, 0)
    tab = lambda i: (i % nt, 0)
    wspec = lambda a: pl.BlockSpec(a.shape, full)
    vt_shape = lambda w: jax.ShapeDtypeStruct((n // seq, w // LANES, nt, LANES, tm), BF16)
    vt_spec = lambda w: pl.BlockSpec((1, w // LANES, 1, LANES, tm), lambda i: (i // nt, 0, i % nt, 0, 0))
    rows = lambda w: jax.ShapeDtypeStruct((n, w), BF16)
    rspec = lambda w: pl.BlockSpec((tm, w), row)
    wide = MLA_HEADS * LANES
    return pl.pallas_call(
        _ev_in_kernel,
        out_shape=[rows(wide), rows(wide), vt_shape(MLA_HEADS * MLA_V), rows(DIFF_W), rows(DIFF_W), vt_shape(DIFF_W)],
        grid=(n // tm,),
        in_specs=[pl.BlockSpec((tm, d), row), wspec(g), wspec(w0), wspec(qg), wspec(kvg), wspec(wq), wspec(wk),
                  wspec(wv)] + [pl.BlockSpec((tm, LANES), tab)] * 4,
        out_specs=[rspec(wide), rspec(wide), vt_spec(MLA_HEADS * MLA_V), rspec(DIFF_W), rspec(DIFF_W),
                   vt_spec(DIFF_W)],
        compiler_params=_cparams(("parallel",)),
        name="ev_in",
    )(x2, g, w0, qg, kvg, wq, wk, wv, cd, sd, cm, sm)


ONES_ROWS = 16


def _chunk_mask_t(shape):
    sh = int(math.log2(CHUNK))
    key_chunk = jnp.right_shift(lax.broadcasted_iota(jnp.int32, shape, 0), sh)
    q_chunk = jnp.right_shift(lax.broadcasted_iota(jnp.int32, shape, 1), sh)
    return key_chunk <= q_chunk


def _attn_streams(i, qs, key_of, val_of, k_ref, vt_ref, m_ref, acc_ref):
    bk = ATTN_BLOCK
    n = len(qs)
    m_ref[...] = jnp.full(m_ref.shape, NEG, F32)
    acc_ref[...] = jnp.zeros(acc_ref.shape, F32)
    dv = acc_ref.shape[1] - ONES_ROWS
    ones = jnp.ones((ONES_ROWS, bk), BF16)

    def step(j, masked):
        kb = k_ref[0, pl.ds(pl.multiple_of(j * bk, bk), bk), :]
        vt = vt_ref[0, :, j]
        st = [_dot_nt(key_of(kb, s), qs[s]) for s in range(n)]
        if masked:
            mask = _chunk_mask_t(st[0].shape)
            st = [jnp.where(mask, x, NEG) for x in st]
        m_prev = [m_ref[s] for s in range(n)]
        m_new = [jnp.maximum(m_prev[s], jnp.max(st[s], axis=0, keepdims=True)) for s in range(n)]
        p = [jnp.exp2(st[s] - m_new[s]).astype(BF16) for s in range(n)]
        pv = [_dot(jnp.concatenate([val_of(vt, s), ones], axis=0), p[s]) for s in range(n)]
        for s in range(n):
            acc_ref[s] = jnp.exp2(m_prev[s] - m_new[s]) * acc_ref[s] + pv[s]
            m_ref[s] = m_new[s]

    def body(j, carry):
        step(j, False)
        return carry

    lax.fori_loop(0, i, body, 0)
    step(i, True)
    accs = [acc_ref[s] for s in range(n)]
    return [a[:dv] / a[dv:dv + 1] for a in accs]


def _mla_attn_kernel(q_ref, k_ref, vt_ref, o_ref, m_ref, acc_ref):
    q = q_ref[0]
    n = q.shape[1] // LANES
    qs = [q[:, s * LANES:(s + 1) * LANES] for s in range(n)]
    key_of = lambda kb, s: kb[:, s * LANES:(s + 1) * LANES]
    val_of = lambda vt, s: vt[s // 2, (s % 2) * MLA_V:(s % 2 + 1) * MLA_V]
    outs = _attn_streams(pl.program_id(2), qs, key_of, val_of, k_ref, vt_ref, m_ref, acc_ref)
    o_ref[0] = jnp.concatenate(outs, axis=0).T.astype(BF16)


def _diff_attn_kernel(q_ref, k_ref, vt_ref, lam_ref, sub_ref, o_ref, m_ref, acc_ref, *, lam_init):
    q = q_ref[0]
    nh = q.shape[1] // LANES
    lane = lax.broadcasted_iota(jnp.int32, (q.shape[0], LANES), 1)
    qs = []
    for h in range(nh):
        q_h = q[:, h * LANES:(h + 1) * LANES]
        zero = jnp.zeros_like(q_h)
        qs += [jnp.where(lane < DIFF_HD, q_h, zero), jnp.where(lane >= DIFF_HD, q_h, zero)]
    key_of = lambda kb, s: kb[:, (s // 2) * LANES:(s // 2 + 1) * LANES]
    val_of = lambda vt, s: vt[s // 2]
    outs = _attn_streams(pl.program_id(2), qs, key_of, val_of, k_ref, vt_ref, m_ref, acc_ref)
    lam = lam_ref[...]
    s1 = jnp.sum(lam[0:1] * lam[1:2], axis=-1, keepdims=True)
    s2 = jnp.sum(lam[2:3] * lam[3:4], axis=-1, keepdims=True)
    lam_full = jnp.exp(s1) - jnp.exp(s2) + lam_init
    for h in range(nh):
        o = (outs[2 * h] - lam_full * outs[2 * h + 1]).T
        o_ref[0, :, h * LANES:(h + 1) * LANES] = (_rms(o, sub_ref[...]) * (1.0 - lam_init)).astype(BF16)


ATTN_STREAMS = 4


def _attn_scratch(bq, dv):
    return [pltpu.VMEM((ATTN_STREAMS, 1, bq), F32), pltpu.VMEM((ATTN_STREAMS, dv + ONES_ROWS, bq), F32)]


def _vt_spec(nk, groups):
    return pl.BlockSpec((1, groups, nk, LANES, ATTN_BLOCK), lambda bi, h, i: (bi, h, 0, 0, 0))


def _mla_attn(qf, kf, vt):
    b, s, _ = qf.shape
    bq = ATTN_BLOCK
    w = ATTN_STREAMS * LANES
    return pl.pallas_call(
        _mla_attn_kernel,
        out_shape=jax.ShapeDtypeStruct((b, s, MLA_HEADS * MLA_V), BF16),
        grid=(b, MLA_HEADS // ATTN_STREAMS, s // bq),
        in_specs=[pl.BlockSpec((1, bq, w), lambda bi, h, i: (bi, i, h)),
                  pl.BlockSpec((1, s, w), lambda bi, h, i: (bi, 0, h)),
                  _vt_spec(s // bq, ATTN_STREAMS // 2)],
        out_specs=pl.BlockSpec((1, bq, ATTN_STREAMS * MLA_V), lambda bi, h, i: (bi, i, h)),
        scratch_shapes=_attn_scratch(bq, MLA_V),
        compiler_params=_cparams(("parallel", "parallel", "arbitrary")),
        name="mla_attn",
    )(qf, kf, vt)


def _diff_attn(dq, dk, dvt, lam, subln, lam_init):
    b, s, _ = dq.shape
    bq = ATTN_BLOCK
    nh = ATTN_STREAMS // 2
    blk = lambda rows, im: pl.BlockSpec((1, rows, nh * LANES), im)
    return pl.pallas_call(
        functools.partial(_diff_attn_kernel, lam_init=lam_init),
        out_shape=jax.ShapeDtypeStruct((b, s, DIFF_HEADS * DIFF_V), BF16),
        grid=(b, DIFF_HEADS // nh, s // bq),
        in_specs=[blk(bq, lambda bi, h, i: (bi, i, h)), blk(s, lambda bi, h, i: (bi, 0, h)),
                  _vt_spec(s // bq, nh),
                  pl.BlockSpec(lam.shape, lambda bi, h, i: (0, 0)),
                  pl.BlockSpec(subln.shape, lambda bi, h, i: (0, 0))],
        out_specs=blk(bq, lambda bi, h, i: (bi, i, h)),
        scratch_shapes=_attn_scratch(bq, DIFF_V),
        compiler_params=_cparams(("parallel", "parallel", "arbitrary")),
        name="diff_attn",
    )(dq, dk, dvt, lam, subln)


def _mix_out_kernel(h_ref, a_ref, b_ref, w_ref, g_ref, rwh_ref, rwl_ref, rb_ref, tri_ref,
                    hout_ref, hf_ref, route_ref, idx_ref, cnt_ref):
    mixed = jnp.concatenate([a_ref[...], b_ref[...]], axis=-1)
    h = h_ref[...] + _dot(mixed, w_ref[...])
    hout_ref[...] = h
    hf = _rms(h, g_ref[...])
    hi = hf.astype(BF16)
    hf_ref[...] = hi
    lo = (hf - hi.astype(F32)).astype(BF16)
    z = _dot(hi, rwh_ref[...]) + _dot(lo, rwh_ref[...]) + _dot(hi, rwl_ref[...]) + rb_ref[...]
    lane_i = lax.broadcasted_iota(jnp.int32, z.shape, 1)
    lane = lane_i.astype(F32)
    big = float(LANES)
    is_g = lane_i < MOE_GROUPS
    zg = jnp.where(is_g, z, NEG)
    mg = jnp.max(zg, axis=-1, keepdims=True)
    g_idx = jnp.min(jnp.where(zg == mg, lane, big), axis=-1, keepdims=True)
    pg_top = 1.0 / jnp.sum(jnp.where(is_g, jnp.exp(zg - mg), 0.0), axis=-1, keepdims=True)
    grp_of_lane = jnp.right_shift(lane_i - MOE_GROUPS, int(math.log2(MOE_EPG))).astype(F32)
    in_grp = (lane_i >= MOE_GROUPS) & (lane_i < MOE_GROUPS + MOE_EXPERTS) & (grp_of_lane == g_idx)
    ze = jnp.where(in_grp, z, NEG)
    m1 = jnp.max(ze, axis=-1, keepdims=True)
    i1 = jnp.min(jnp.where(ze == m1, lane, big), axis=-1, keepdims=True)
    ze2 = jnp.where(lane == i1, NEG, ze)
    m2 = jnp.max(ze2, axis=-1, keepdims=True)
    i2 = jnp.min(jnp.where(ze2 == m2, lane, big), axis=-1, keepdims=True)
    r = jnp.exp(m2 - m1)
    w1 = 1.0 / (1.0 + r)
    w2 = r / (1.0 + r)
    e1 = i1 - float(MOE_GROUPS)
    e2 = i2 - float(MOE_GROUPS)
    @pl.when(pl.program_id(0) == 0)
    def _():
        cnt_ref[...] = jnp.zeros(cnt_ref.shape, F32)

    both = jnp.where((lane == i1) | (lane == i2), 1.0, 0.0)
    before = _dot(tri_ref[...], both.astype(BF16)) + cnt_ref[0:1, :]
    rank1 = jnp.sum(jnp.where(lane == i1, before, 0.0), axis=-1, keepdims=True)
    rank2 = jnp.sum(jnp.where(lane == i2, before, 0.0), axis=-1, keepdims=True)
    cnt_ref[...] = cnt_ref[...] + jnp.sum(both, axis=0, keepdims=True)
    vals = (e1, e2, pg_top * w1, pg_top * w2, rank1, rank2)
    out = jnp.zeros(z.shape, F32)
    for col, val in enumerate(vals):
        out = jnp.where(lane_i == col, val, out)
    route_ref[...] = out
    idx_ref[0] = out.T[0:8].astype(jnp.int32)


def _mix_out(h, a, b, w, g, rwh, rwl, rb, tri):
    n, d = h.shape
    tm = ROW_TILE
    row = lambda i: (i, 0)
    full = lambda i: (0, 0)
    wspec = lambda t: pl.BlockSpec(t.shape, full)
    return pl.pallas_call(
        _mix_out_kernel,
        out_shape=[jax.ShapeDtypeStruct((n, d), F32), jax.ShapeDtypeStruct((n, d), BF16),
                   jax.ShapeDtypeStruct((n, LANES), F32), jax.ShapeDtypeStruct((n // tm, 8, tm), jnp.int32),
                   jax.ShapeDtypeStruct((8, LANES), F32)],
        grid=(n // tm,),
        in_specs=[pl.BlockSpec((tm, d), row), pl.BlockSpec((tm, a.shape[1]), row), pl.BlockSpec((tm, b.shape[1]), row),
                  wspec(w), wspec(g), wspec(rwh), wspec(rwl), wspec(rb), wspec(tri)],
        out_specs=[pl.BlockSpec((tm, d), row), pl.BlockSpec((tm, d), row), pl.BlockSpec((tm, LANES), row),
                   pl.BlockSpec((1, 8, tm), lambda i: (i, 0, 0)), pl.BlockSpec((8, LANES), full)],
        compiler_params=_cparams(("arbitrary",)),
        name="mix_out",
    )(h, a, b, w, g, rwh, rwl, rb, tri)


def _expert_kernel(te_ref, nv_ref, x_ref, wg_ref, wu_ref, wd_ref, y_ref, wgb_ref, wub_ref, wdb_ref):
    t = pl.program_id(0)

    @pl.when((t == 0) | (te_ref[t] != te_ref[jnp.maximum(t - 1, 0)]))
    def _():
        wgb_ref[...] = wg_ref[0, 0].astype(BF16)
        wub_ref[...] = wu_ref[0, 0].astype(BF16)
        wdb_ref[...] = wd_ref[0, 0].astype(BF16)

    @pl.when(t < nv_ref[0])
    def _():
        x = x_ref[...]
        a = _dot(x, wgb_ref[...])
        up = _dot(x, wub_ref[...])
        act = (a * _sigmoid(a) * up).astype(BF16)
        y_ref[...] = _dot(act, wdb_ref[...]).astype(BF16)

    @pl.when(t >= nv_ref[0])
    def _():
        y_ref[...] = jnp.zeros(y_ref.shape, BF16)


def _experts(xs, tile_expert, n_valid, wg, wu, wd, layer):
    p, d = xs.shape
    tm = EXPERT_TILE
    ff = wg.shape[3]
    grid_spec = pltpu.PrefetchScalarGridSpec(
        num_scalar_prefetch=2,
        grid=(p // tm,),
        in_specs=[pl.BlockSpec((tm, d), lambda t, te, nv: (t, 0)),
                  pl.BlockSpec((1, 1, d, ff), lambda t, te, nv: (layer, te[t], 0, 0)),
                  pl.BlockSpec((1, 1, d, ff), lambda t, te, nv: (layer, te[t], 0, 0)),
                  pl.BlockSpec((1, 1, ff, d), lambda t, te, nv: (layer, te[t], 0, 0))],
        out_specs=pl.BlockSpec((tm, d), lambda t, te, nv: (t, 0)),
        scratch_shapes=[pltpu.VMEM((d, ff), BF16), pltpu.VMEM((d, ff), BF16), pltpu.VMEM((ff, d), BF16)],
    )
    return pl.pallas_call(
        _expert_kernel,
        out_shape=jax.ShapeDtypeStruct((p, d), BF16),
        grid_spec=grid_spec,
        compiler_params=_cparams(("arbitrary",)),
        name="experts",
    )(tile_expert, n_valid, xs, wg, wu, wd)


def _add_experts(h_ref, y1_ref, y2_ref, route_ref):
    route = route_ref[...]
    return h_ref[...] + route[:, 2:3] * y1_ref[...].astype(F32) + route[:, 3:4] * y2_ref[...].astype(F32)


def _final_kernel(h_ref, y1_ref, y2_ref, route_ref, g_ref, o_ref):
    o_ref[...] = _rms(_add_experts(h_ref, y1_ref, y2_ref, route_ref), g_ref[...])


def _final(h, y1, y2, route, g):
    n, d = h.shape
    tm = ROW_TILE
    row = lambda i: (i, 0)
    return pl.pallas_call(
        _final_kernel,
        out_shape=jax.ShapeDtypeStruct((n, d), F32),
        grid=(n // tm,),
        in_specs=[pl.BlockSpec((tm, d), row), pl.BlockSpec((tm, d), row), pl.BlockSpec((tm, d), row),
                  pl.BlockSpec((tm, LANES), row), pl.BlockSpec(g.shape, lambda i: (0, 0))],
        out_specs=pl.BlockSpec((tm, d), row),
        compiler_params=_cparams(("parallel",)),
        name="moe_final",
    )(h, y1, y2, route, g)


def _moe(hf, idx, counts, wg, wu, wd, layer):
    n, d = hf.shape
    tm = EXPERT_TILE
    ids = jnp.arange(MOE_EXPERTS, dtype=jnp.int32)
    counts = counts[0, MOE_GROUPS:MOE_GROUPS + MOE_EXPERTS].astype(jnp.int32)
    padded = ((counts + tm - 1) // tm) * tm
    ends = jnp.cumsum(padded)
    starts = ends - padded
    start_of = lambda e: jnp.sum(jnp.where(e[:, None] == ids[None, :], starts[None, :], 0), axis=-1)
    e1, e2 = idx[:, 0, :].reshape(n), idx[:, 1, :].reshape(n)
    dest1 = start_of(e1) + idx[:, 4, :].reshape(n)
    dest2 = start_of(e2) + idx[:, 5, :].reshape(n)
    n_rows = 2 * n + MOE_EXPERTS * tm
    tok = jnp.arange(n, dtype=jnp.int32)
    tok_for_row = (jnp.arange(n_rows, dtype=jnp.int32) % n).at[jnp.concatenate([dest1, dest2])].set(
        jnp.concatenate([tok, tok]), mode="promise_in_bounds", unique_indices=True)
    take = functools.partial(jnp.take, axis=0, mode="clip")
    tile_start = jnp.arange(n_rows // tm, dtype=jnp.int32) * tm
    tile_expert = jnp.minimum(jnp.sum((ends[None, :] <= tile_start[:, None]).astype(jnp.int32), axis=1),
                              MOE_EXPERTS - 1)
    n_valid = (ends[-1] // tm).astype(jnp.int32).reshape(1)
    xs = take(hf, tok_for_row)
    ys = _experts(xs, tile_expert, n_valid, wg, wu, wd, layer)
    return take(ys, dest1), take(ys, dest2)


def _od_in_kernel(h_ref, y1_ref, y2_ref, route_ref, g_ref, w_ref, hout_ref, ur_ref, uh_ref):
    h = _add_experts(h_ref, y1_ref, y2_ref, route_ref)
    hout_ref[...] = h
    hn = _rms(h, g_ref[...]).astype(BF16)
    u = _dot(hn, w_ref[...])
    ur_ref[...] = u[:, :RWKV_IN].astype(BF16)
    uh_ref[...] = u[:, RWKV_IN:].astype(BF16)


def _od_in(h, y1, y2, route, g, w):
    n, d = h.shape
    tm = ROW_TILE
    row = lambda i: (i, 0)
    rspec = lambda width: pl.BlockSpec((tm, width), row)
    return pl.pallas_call(
        _od_in_kernel,
        out_shape=[jax.ShapeDtypeStruct((n, d), F32), jax.ShapeDtypeStruct((n, RWKV_IN), BF16),
                   jax.ShapeDtypeStruct((n, HGRN_IN), BF16)],
        grid=(n // tm,),
        in_specs=[rspec(d), rspec(d), rspec(d), rspec(LANES), pl.BlockSpec(g.shape, lambda i: (0, 0)),
                  pl.BlockSpec(w.shape, lambda i: (0, 0))],
        out_specs=[rspec(d), rspec(RWKV_IN), rspec(HGRN_IN)],
        compiler_params=_cparams(("parallel",)),
        name="od_in",
    )(h, y1, y2, route, g, w)


def _rwkv_kernel(u_ref, mu_ref, w0_ref, w2_ref, a0_ref, a2_ref, g2_ref, kk_ref, ka_ref, rk_ref,
                 lnw_ref, lnb_ref, tri_ref, seg_ref, o_ref, prev_ref, ht_ref):
    c = pl.program_id(1)

    @pl.when(c == 0)
    def _():
        prev_ref[...] = jnp.zeros(prev_ref.shape, F32)
        ht_ref[...] = jnp.zeros(ht_ref.shape, F32)

    nb = u_ref.shape[0]
    u = u_ref[...].reshape(nb * CHUNK, RWKV_IN).astype(F32)
    rows = lax.broadcasted_iota(jnp.int32, u.shape, 0)
    u_prev = pltpu.roll(u, 1, 0)
    for bi in range(nb):
        u_prev = jnp.where(rows == bi * CHUNK, prev_ref[bi], u_prev)
        prev_ref[bi] = u[(bi + 1) * CHUNK - 1:(bi + 1) * CHUNK, :]
    us = u + (u_prev - u) * mu_ref[...]
    w = RWKV_W
    r = us[:, 0:w]
    k = us[:, w:2 * w]
    v = us[:, 2 * w:3 * w]
    x12 = us[:, 3 * w:3 * w + LANES]
    xg = us[:, 3 * w + LANES:]
    seg_pair = seg_ref[...]

    def seg(x):
        return jnp.concatenate([_split_dot(x[:, p * LANES:(p + 1) * LANES], seg_pair)
                                for p in range(RWKV_HEADS // 2)], axis=1)

    tri = tri_ref[...]

    wl = w0_ref[...] + _dot(jnp.tanh(x12).astype(BF16), w2_ref[...])
    nwl = -wl
    softplus = jnp.maximum(nwl, 0.0) + jnp.log1p(jnp.exp(-jnp.abs(nwl)))
    lw = -jnp.exp(-softplus - 0.5)
    a = _sigmoid(a0_ref[...] + _dot(x12.astype(BF16), a2_ref[...]))
    g = _dot(_sigmoid(xg).astype(BF16), g2_ref[...])
    kk = k * kk_ref[...]
    kk = kk * lax.rsqrt(jnp.maximum(seg(kk * kk), 1e-24))
    k2 = k * (1.0 + (a - 1.0) * ka_ref[...])
    a_in = -kk
    b_in = kk * a

    lg = _split_dot(lw, tri, left=True)
    rbs = [slice(bi * CHUNK, (bi + 1) * CHUNK) for bi in range(nb)]
    lg_c = [lg[(bi + 1) * CHUNK - 1:(bi + 1) * CHUNK, :] for bi in range(nb)]
    e_neg = jnp.exp(-lg)
    e_rel = jnp.concatenate([jnp.exp(lg_c[bi] - lg[rbs[bi]]) for bi in range(nb)], axis=0)
    g_c = [jnp.exp(x) for x in lg_c]
    at = a_in * jnp.exp(lg - lw)
    rt = r * jnp.exp(lg)
    kt = k2 * e_neg
    bt = b_in * e_neg
    kh = k2 * e_rel
    bh = b_in * e_rel

    c2 = 2 * CHUNK
    ri = lax.broadcasted_iota(jnp.int32, (c2, c2), 0)
    ci = lax.broadcasted_iota(jnp.int32, (c2, c2), 1)
    sh = int(math.log2(CHUNK))
    same = jnp.right_shift(ri, sh) == jnp.right_shift(ci, sh)
    strict = same & (ri > ci)
    incl = same & (ri >= ci)
    eye = jnp.where(ri == ci, 1.0, 0.0).astype(F32)
    lane = lax.broadcasted_iota(jnp.int32, (CHUNK, LANES), 1)
    lo_half = lane < RWKV_HD

    def bd(x):
        return jnp.concatenate([jnp.where(lo_half, x, 0.0), jnp.where(lo_half, 0.0, x)], axis=0)

    npair = RWKV_HEADS // 2
    pairs = range(nb * npair)
    sls = [(rbs[c // npair], slice((c % npair) * LANES, (c % npair + 1) * LANES)) for c in pairs]
    bdb = lambda x: [bd(x[sl]).astype(BF16) for sl in sls]
    at_b = [bd(at[sl]) for sl in sls]
    rt_b = [bd(rt[sl]) for sl in sls]
    kt_b, bt_b, kh_b, bh_b, v_b = bdb(kt), bdb(bt), bdb(kh), bdb(bh), bdb(v)
    hts = [ht_ref[p] for p in pairs]
    htb = [t.astype(BF16) for t in hts]
    sc = [_dot_nt(jnp.concatenate([at_b[p], rt_b[p]], axis=0).astype(BF16),
                  jnp.concatenate([kt_b[p], bt_b[p]], axis=0)) for p in pairs]
    a_ak = [jnp.where(strict, s[:c2, :c2], 0.0).astype(BF16) for s in sc]
    a_ab = [jnp.where(strict, s[:c2, c2:], 0.0) for s in sc]
    a_rk = [jnp.where(incl, s[c2:, :c2], 0.0).astype(BF16) for s in sc]
    a_rb = [jnp.where(incl, s[c2:, c2:], 0.0).astype(BF16) for s in sc]
    x2 = [_dot(a_ak[p], v_b[p]) for p in pairs]
    e1 = [_dot(a_rk[p], v_b[p]) for p in pairs]
    d1 = [_dot_tn(v_b[p], kh_b[p]) for p in pairs]
    pw = a_ab
    t_inv = [eye + a for a in a_ab]
    for _ in range(int(math.log2(CHUNK)) - 1):
        pwb = [x.astype(BF16) for x in pw]
        pw = [_dot(x, x) for x in pwb]
        t_inv = [t_inv[p] + _dot(t_inv[p].astype(BF16), pw[p].astype(BF16)) for p in pairs]
    txb = [_dot(t_inv[p].astype(BF16), jnp.concatenate([at_b[p], x2[p]], axis=1).astype(BF16)).astype(BF16)
           for p in pairs]
    qe = [_dot(a_rb[p], txb[p]) for p in pairs]
    gd = [_dot_tn(txb[p], bh_b[p]) for p in pairs]
    y_b = [_dot_nt((rt_b[p] + qe[p][:, :LANES]).astype(BF16), htb[p]) + e1[p] + qe[p][:, LANES:] for p in pairs]
    for p in pairs:
        upd = hts[p] * g_c[p // npair][:, sls[p][1]] + _dot(htb[p], gd[p][:LANES].astype(BF16)) + d1[p] + gd[p][LANES:]
        ht_ref[p] = jnp.where(same, upd, 0.0)
    y_p = [t[:CHUNK] + t[CHUNK:] for t in y_b]
    y = jnp.concatenate([jnp.concatenate(y_p[bi * npair:(bi + 1) * npair], axis=1) for bi in range(nb)],
                        axis=0)

    inv_n = 1.0 / RWKV_HD
    mean = seg(y) * inv_n
    dlt = y - mean
    var = seg(dlt * dlt) * inv_n
    yn = dlt * lax.rsqrt(var + RWKV_GN_EPS) * lnw_ref[...] + lnb_ref[...]
    bonus = seg(r * k2 * rk_ref[...]) * v
    o_ref[...] = ((yn + bonus) * g).astype(BF16).reshape(nb, CHUNK, RWKV_W)


def _rwkv(u, mu, w0, w2p, a0, a2p, g2, k_k, k_a, r_k, ln_w, ln_b, tri, seg):
    b, s, _ = u.shape
    nb = tri.shape[0] // CHUNK
    full = lambda bi, ci: (0, 0)
    wspec = lambda t: pl.BlockSpec(t.shape, full)
    params = [mu, w0, w2p, a0, a2p, g2, k_k, k_a, r_k, ln_w, ln_b, tri, seg]
    return pl.pallas_call(
        _rwkv_kernel,
        out_shape=jax.ShapeDtypeStruct((b, s, RWKV_W), BF16),
        grid=(b // nb, s // CHUNK),
        in_specs=[pl.BlockSpec((nb, CHUNK, RWKV_IN), lambda bi, ci: (bi, ci, 0))] + [wspec(t) for t in params],
        out_specs=pl.BlockSpec((nb, CHUNK, RWKV_W), lambda bi, ci: (bi, ci, 0)),
        scratch_shapes=[pltpu.VMEM((nb, 1, RWKV_IN), F32), pltpu.VMEM((nb * RWKV_HEADS // 2, LANES, LANES), F32)],
        compiler_params=_cparams(("parallel", "arbitrary")),
        name="rwkv7",
    )(u, *params)


def _hgrn_kernel(u_ref, lb_ref, on_ref, tri_ref, ones_ref, sel_ref, o_ref, st_ref):
    c = pl.program_id(1)

    @pl.when(c == 0)
    def _():
        st_ref[...] = jnp.zeros(st_ref.shape, F32)

    nb = u_ref.shape[0]
    u = u_ref[...].reshape(nb * CHUNK, HGRN_IN).astype(F32)
    lbp = lb_ref[...]
    mx = jnp.maximum(lbp[0:1], lbp[1:2])
    e0 = jnp.exp(lbp[0:1] - mx)
    e1 = jnp.exp(lbp[1:2] - mx)
    p0 = e0 / (e0 + e1)
    p1 = e1 / (e0 + e1)
    lb = (p0 + p1) - p0
    tri = tri_ref[...]
    ones = ones_ref[...]
    sel = sel_ref[...]
    d = HGRN_D
    w = HGRN_W
    heads = range(HGRN_HEADS)
    sls = [slice(h * d, (h + 1) * d) for h in heads]
    q, z, iv, gt = u[:, :w], u[:, w:2 * w], u[:, 2 * w:3 * w], u[:, 3 * w:]
    qs = q * _sigmoid(q)
    log_sig = jnp.minimum(z, 0.0) - jnp.log1p(jnp.exp(-jnp.abs(z)))
    x1 = jnp.log(lb)
    x2 = jnp.log1p(-lb) + log_sig
    log_f = jnp.maximum(x1, x2) + jnp.log1p(jnp.exp(-jnp.abs(x1 - x2)))
    key = (1.0 - lb) * _sigmoid(-z)
    bc = _split_dot(log_f, tri, left=True)
    seqs = range(nb)
    rbs = [slice(i * CHUNK, (i + 1) * CHUNK) for i in seqs]
    b_l = [bc[(i + 1) * CHUNK - 1:(i + 1) * CHUNK] for i in seqs]
    ivb = iv.astype(BF16)
    qe = (qs * jnp.exp(bc)).astype(BF16)
    kl = (key * jnp.concatenate([jnp.exp(b_l[i] - bc[rbs[i]]) for i in seqs], axis=0)).astype(BF16)
    nh = HGRN_HEADS
    sts = [st_ref[c] for c in range(nb * nh)]
    o_inter = [_dot_nt(qe[rbs[c // nh], sls[c % nh]], sts[c].astype(BF16)) for c in range(nb * nh)]
    for c in range(nb * nh):
        i, sl = c // nh, sls[c % nh]
        st_ref[c] = sts[c] * jnp.exp(b_l[i])[:, sl] + _dot_tn(ivb[rbs[i], sl], kl[rbs[i], sl])

    nblk = CHUNK // SUB
    hrow = lax.broadcasted_iota(jnp.int32, (HALF, w), 0)
    p_all, v_tile, sc_off, sc_half = [], [], [], []
    for blk in range(nb * nblk):
        base = (blk // nblk) * CHUNK
        r0 = base + (blk % nblk) * SUB
        b_i, q_i, k_i = bc[r0:r0 + SUB], qs[r0:r0 + SUB], key[r0:r0 + SUB]
        rows = []
        for t in range(SUB):
            lo = (t // HALF) * HALF
            rows.append(q_i[t:t + 1] * k_i[lo:lo + HALF]
                        * jnp.where(hrow <= t - lo, jnp.exp(b_i[t:t + 1] - b_i[lo:lo + HALF]), 0.0))
        p_all.append(jnp.concatenate(rows, axis=0).astype(BF16))
        v_tile.append(jnp.concatenate([iv[r0:r0 + HALF]] * HALF + [iv[r0 + HALF:r0 + SUB]] * HALF, axis=0))
        b_h = b_i[HALF - 1:HALF]
        qh = (q_i[HALF:] * jnp.exp(b_i[HALF:] - b_h)).astype(BF16)
        kh = (k_i[:HALF] * jnp.exp(b_h - b_i[:HALF])).astype(BF16)
        sc_half.append([_dot_nt(qh[:, sl], kh[:, sl]).astype(BF16) for sl in sls])
        if r0 > base:
            b_m = bc[r0 - 1:r0]
            qp = (q_i * jnp.exp(b_i - b_m)).astype(BF16)
            kp = (key[base:r0] * jnp.exp(b_m - bc[base:r0])).astype(BF16)
            sc_off.append([_dot_nt(qp[:, sl], kp[:, sl]).astype(BF16) for sl in sls])
        else:
            sc_off.append(None)
    rs = [jnp.concatenate([_dot(p[:, sl], ones) for sl in sls], axis=1) for p in p_all]
    rv = [(rs[blk] * v_tile[blk]).astype(BF16) for blk in range(nb * nblk)]
    for c in range(nb * nh):
        i, sl = c // nh, sls[c % nh]
        parts = []
        for blk in range(i * nblk, (i + 1) * nblk):
            r0 = i * CHUNK + (blk % nblk) * SUB
            lower = _dot(sc_half[blk][c % nh], iv[r0:r0 + HALF, sl].astype(BF16))
            o_i = _dot(sel, rv[blk][:, sl]) + jnp.concatenate([jnp.zeros_like(lower), lower], axis=0)
            if sc_off[blk] is not None:
                o_i = o_i + _dot(sc_off[blk][c % nh], ivb[i * CHUNK:i * CHUNK + (blk % nblk) * SUB, sl])
            parts.append(o_i)
        o = o_inter[c] + jnp.concatenate(parts, axis=0)
        on = _rms(o, on_ref[:, sl])
        g_h = gt[rbs[i], sl]
        o_ref[i, :, sl] = (on * (g_h * _sigmoid(g_h))).astype(BF16)


def _hgrn(u, lbp, o_norm, tri, ones, sel):
    b, s, _ = u.shape
    nb = tri.shape[0] // CHUNK
    full = lambda bi, ci: (0, 0)
    wspec = lambda t: pl.BlockSpec(t.shape, full)
    return pl.pallas_call(
        _hgrn_kernel,
        out_shape=jax.ShapeDtypeStruct((b, s, HGRN_W), BF16),
        grid=(b // nb, s // CHUNK),
        in_specs=[pl.BlockSpec((nb, CHUNK, HGRN_IN), lambda bi, ci: (bi, ci, 0)), wspec(lbp), wspec(o_norm),
                  wspec(tri), wspec(ones), wspec(sel)],
        out_specs=pl.BlockSpec((nb, CHUNK, HGRN_W), lambda bi, ci: (bi, ci, 0)),
        scratch_shapes=[pltpu.VMEM((nb * HGRN_HEADS, HGRN_D, HGRN_D), F32)],
        compiler_params=_cparams(("parallel", "arbitrary")),
        name="hgrn2",
    )(u, lbp, o_norm, tri, ones, sel)


def _rope_tables(seq):
    pos = jnp.arange(seq, dtype=F32)[:, None]

    def cs(half):
        inv = jnp.power(ROPE_THETA, -jnp.arange(half, dtype=F32) / half)
        ang = pos * inv[None, :]
        return jnp.cos(ang), jnp.sin(ang)

    cd, sd = cs(DIFF_HD // 2)
    cos_d = jnp.tile(jnp.concatenate([cd, cd], axis=1), (1, LANES // DIFF_HD))
    sin_d = jnp.tile(jnp.concatenate([-sd, sd], axis=1), (1, LANES // DIFF_HD))
    cm, sm = cs(MLA_ROPE // 2)
    one = jnp.ones((seq, MLA_NOPE), F32)
    tail = LANES - MLA_QK
    cos_m = jnp.concatenate([one, cm, cm, jnp.ones((seq, tail), F32)], axis=1)
    sin_m = jnp.concatenate([0 * one, -sm, sm, jnp.zeros((seq, tail), F32)], axis=1)
    return cos_d, sin_d, cos_m, sin_m


def _router_weights(w_group, b_group, w_expert, b_expert):
    d = w_group.shape[0]
    pad = LANES - MOE_GROUPS - MOE_EXPERTS
    w = jnp.concatenate([w_group, w_expert, jnp.zeros((d, pad), F32)], axis=1)
    bias = jnp.concatenate([b_group, b_expert, jnp.zeros((pad,), F32)])[None, :]
    hi = w.astype(BF16)
    lo = (w - hi.astype(F32)).astype(BF16)
    return hi, lo, bias


def kernel(x, norm_mix, norm_ffn, norm_final, ev_w_in, ev_w_out, mla_q_norm, mla_w_q_up, mla_kv_norm, mla_w_kv_up, diff_lambda, diff_subln, od_w_in, od_w_out, rwkv_mu, rwkv_w0, rwkv_w2, rwkv_a0, rwkv_a2, rwkv_g2, rwkv_k_k, rwkv_k_a, rwkv_r_k, rwkv_ln_w, rwkv_ln_b, hgrn_lb, hgrn_o_norm, moe_w_group, moe_b_group, moe_w_expert, moe_b_expert, moe_w_gate, moe_w_up, moe_w_down):
    b, s, d = x.shape
    n = b * s
    assert norm_mix.shape[0] == 2 and hgrn_lb.shape[0] == 2
    assert s % ATTN_BLOCK == 0 and s % ROW_TILE == 0 and ATTN_BLOCK % CHUNK == 0
    row2 = lambda t: t.reshape(1, -1)
    h = x.reshape(n, d)

    w_in = ev_w_in[0]
    o1, o2, o3 = MLA_LORA, 2 * MLA_LORA, 2 * MLA_LORA + MLA_ROPE
    kr_pad = jnp.zeros((d, LANES), F32).at[:, MLA_NOPE:MLA_QK].set(w_in[:, o2:o3])
    w0 = jnp.concatenate([w_in[:, :o2], kr_pad, w_in[:, o3:]], axis=1).astype(BF16)
    wq = mla_w_q_up[0].reshape(MLA_LORA, MLA_HEADS, MLA_QK)
    wq = jnp.pad(wq, ((0, 0), (0, 0), (0, LANES - MLA_QK))).reshape(MLA_LORA, MLA_HEADS * LANES).astype(BF16)
    wkv = mla_w_kv_up[0].reshape(MLA_LORA, MLA_HEADS, MLA_NOPE + MLA_V)
    wk = jnp.pad(wkv[:, :, :MLA_NOPE], ((0, 0), (0, 0), (0, LANES - MLA_NOPE)))
    wk = wk.reshape(MLA_LORA, MLA_HEADS * LANES).astype(BF16)
    wv = wkv[:, :, MLA_NOPE:].reshape(MLA_LORA, MLA_HEADS * MLA_V).astype(BF16)
    cos_d, sin_d, cos_m, sin_m = _rope_tables(s)
    qf, kf, vm, dq, dk, dv = _ev_in(h, row2(norm_mix[0]), w0, row2(mla_q_norm[0]), row2(mla_kv_norm[0]),
                                    wq, wk, wv, cos_d, sin_d, cos_m, sin_m, s)
    r3 = lambda t: t.reshape(b, s, t.shape[-1])
    o_mla = _mla_attn(r3(qf), r3(kf), vm)
    lam_init = 0.8 - 0.6 * math.exp(-0.3 * 0)
    o_diff = _diff_attn(r3(dq), r3(dk), dv, diff_lambda[0], row2(diff_subln[0]), lam_init)
    rwh, rwl, rb = _router_weights(moe_w_group[0], moe_b_group[0], moe_w_expert[0], moe_b_expert[0])
    ti = jnp.arange(ROW_TILE)
    tri_rows = (ti[None, :] < ti[:, None]).astype(BF16)
    h, hf, route, ridx, counts = _mix_out(h, o_mla.reshape(n, -1), o_diff.reshape(n, -1), ev_w_out[0].astype(BF16),
                                    row2(norm_ffn[0]), rwh, rwl, rb, tri_rows)
    y1, y2 = _moe(hf, ridx, counts, moe_w_gate, moe_w_up, moe_w_down, 0)

    h, ur, uh = _od_in(h, y1, y2, route, row2(norm_mix[1]), od_w_in[0].astype(BF16))
    zpad = jnp.zeros((RWKV_DECAY_LORA, RWKV_W), F32)
    w2p = jnp.concatenate([rwkv_w2[0], zpad], axis=0).astype(BF16)
    a2p = jnp.concatenate([zpad, rwkv_a2[0]], axis=0).astype(BF16)
    ci = jnp.arange(CHUNK)
    tri = (ci[None, :] <= ci[:, None]).astype(BF16)
    li = jnp.arange(LANES) // RWKV_HD
    seg = (li[:, None] == li[None, :]).astype(BF16)
    o_c = _rwkv(ur.reshape(b, s, RWKV_IN), row2(rwkv_mu[0]), row2(rwkv_w0[0]), w2p, row2(rwkv_a0[0]), a2p,
                rwkv_g2[0].astype(BF16), row2(rwkv_k_k[0]), row2(rwkv_k_a[0]), row2(rwkv_r_k[0]),
                row2(rwkv_ln_w[0]), row2(rwkv_ln_b[0]),
                jnp.kron(jnp.eye(math.gcd(b, RWKV_SEQS), dtype=BF16), tri), seg)
    pt = jnp.arange(SUB * HALF) // HALF
    ps = jnp.arange(SUB * HALF) % HALF
    sel = ((pt[None, :] == jnp.arange(SUB)[:, None]) & (ps <= pt % HALF)[None, :]).astype(BF16)
    o_d = _hgrn(uh.reshape(b, s, HGRN_IN), hgrn_lb, row2(hgrn_o_norm[0]),
                jnp.kron(jnp.eye(math.gcd(b, HGRN_SEQS), dtype=BF16), tri), jnp.ones((LANES, LANES), BF16), sel)
    rwh, rwl, rb = _router_weights(moe_w_group[1], moe_b_group[1], moe_w_expert[1], moe_b_expert[1])
    h, hf, route, ridx, counts = _mix_out(h, o_c.reshape(n, -1), o_d.reshape(n, -1), od_w_out[0].astype(BF16),
                                    row2(norm_ffn[1]), rwh, rwl, rb, tri_rows)
    y1, y2 = _moe(hf, ridx, counts, moe_w_gate, moe_w_up, moe_w_down, 1)
    out = _final(h, y1, y2, route, row2(norm_final))
    return out.reshape(b, s, d)
```

```python
import functools
import math

import jax
import jax.numpy as jnp
from jax import lax
from jax.experimental import pallas as pl
from jax.experimental.pallas import tpu as pltpu

F32 = jnp.float32
BF16 = jnp.bfloat16

CHUNK = 64
ROPE_THETA = 10000.0
NORM_EPS = 1e-6
MLA_HEADS = 8
MLA_LORA = 256
MLA_NOPE = 64
MLA_ROPE = 32
MLA_V = 64
MLA_QK = MLA_NOPE + MLA_ROPE
DIFF_HEADS = 4
DIFF_HD = 64
DIFF_V = 2 * DIFF_HD
DIFF_W = DIFF_HEADS * 2 * DIFF_HD
RWKV_HEADS = 8
RWKV_HD = 64
RWKV_W = RWKV_HEADS * RWKV_HD
RWKV_DECAY_LORA = 64
RWKV_A_LORA = 64
RWKV_GATE_LORA = 128
RWKV_IN = 3 * RWKV_W + RWKV_DECAY_LORA + RWKV_A_LORA + RWKV_GATE_LORA
RWKV_GN_EPS = 64e-5
HGRN_HEADS = 4
HGRN_D = 128
HGRN_W = HGRN_HEADS * HGRN_D
HGRN_IN = 4 * HGRN_W
MOE_GROUPS = 4
MOE_EPG = 8
MOE_EXPERTS = MOE_GROUPS * MOE_EPG

LANES = 128
ROW_TILE = 512
ATTN_BLOCK = 512
EXPERT_TILE = 512
SUB = 16
HALF = 8
RWKV_SEQS = 4
HGRN_SEQS = 4
V7X_VMEM_BYTES = 64 * 1024 * 1024
VMEM_LIMIT = V7X_VMEM_BYTES * 3 // 4
NEG = -1e30
LOG2E = math.log2(math.e)


def _cparams(sem):
    return pltpu.CompilerParams(dimension_semantics=sem, vmem_limit_bytes=VMEM_LIMIT)


def _dot(a, b):
    return jnp.dot(a, b, preferred_element_type=F32)


def _dot_nt(a, b):
    return lax.dot_general(a, b, (((1,), (1,)), ((), ())), preferred_element_type=F32)


def _dot_tn(a, b):
    return lax.dot_general(a, b, (((0,), (0,)), ((), ())), preferred_element_type=F32)


def _rms(x, g):
    return x * lax.rsqrt(jnp.mean(x * x, axis=-1, keepdims=True) + NORM_EPS) * g


def _sigmoid(x):
    return 1.0 / (1.0 + jnp.exp(-x))


def _split_dot(x, w_bf16, left=False):
    hi = x.astype(BF16)
    lo = (x - hi.astype(F32)).astype(BF16)
    if left:
        return _dot(w_bf16, hi) + _dot(w_bf16, lo)
    return _dot(hi, w_bf16) + _dot(lo, w_bf16)


def _tile_lanes(t, n):
    return jnp.concatenate([t] * n, axis=-1) if n > 1 else t


def _rope_rot(x, half):
    w = x.shape[-1]
    lane = lax.broadcasted_iota(jnp.int32, x.shape, x.ndim - 1)
    first = (lane & (2 * half - 1)) < half
    return jnp.where(first, pltpu.roll(x, w - half, x.ndim - 1), pltpu.roll(x, half, x.ndim - 1))


def _ev_in_kernel(x_ref, g_ref, w0_ref, qg_ref, kvg_ref, wq_ref, wk_ref, wv_ref,
                  cd_ref, sd_ref, cm_ref, sm_ref,
                  qf_ref, kf_ref, vm_ref, dq_ref, dk_ref, dv_ref):
    hn = _rms(x_ref[...], g_ref[...]).astype(BF16)
    u = _dot(hn, w0_ref[...])
    o1, o2 = 2 * MLA_LORA, 2 * MLA_LORA + LANES
    c_q = u[:, :MLA_LORA]
    c_kv = u[:, MLA_LORA:o1]
    kr = u[:, o1:o2]
    dq = u[:, o2:o2 + DIFF_W]
    dk = u[:, o2 + DIFF_W:o2 + 2 * DIFF_W]
    dv = u[:, o2 + 2 * DIFF_W:]
    cqn = _rms(c_q, qg_ref[...]).astype(BF16)
    ckn = _rms(c_kv, kvg_ref[...]).astype(BF16)
    q = _dot(cqn, wq_ref[...])
    k = _dot(ckn, wk_ref[...])
    v = _dot(ckn, wv_ref[...])
    cm = cm_ref[...]
    sm = sm_ref[...]
    m_half = MLA_ROPE // 2
    lane = lax.broadcasted_iota(jnp.int32, q.shape, 1) & (LANES - 1)
    first = (lane >= MLA_NOPE) & (lane < MLA_NOPE + m_half)
    rot_q = jnp.where(first, pltpu.roll(q, q.shape[1] - m_half, 1), pltpu.roll(q, m_half, 1))
    q = (q * _tile_lanes(cm, MLA_HEADS) + rot_q * _tile_lanes(sm, MLA_HEADS)) * (MLA_QK ** -0.5 * LOG2E)
    lane1 = lax.broadcasted_iota(jnp.int32, kr.shape, 1)
    first1 = (lane1 >= MLA_NOPE) & (lane1 < MLA_NOPE + m_half)
    rot_k = jnp.where(first1, pltpu.roll(kr, LANES - m_half, 1), pltpu.roll(kr, m_half, 1))
    kr = kr * cm + rot_k * sm
    k = k + _tile_lanes(kr, MLA_HEADS)
    qf_ref[...] = q.astype(BF16)
    kf_ref[...] = k.astype(BF16)
    vm_ref[0, :, 0] = v.T.reshape(vm_ref.shape[1], LANES, v.shape[0]).astype(BF16)
    cd = _tile_lanes(cd_ref[...], DIFF_W // LANES)
    sd = _tile_lanes(sd_ref[...], DIFF_W // LANES)
    dq = (dq * cd + _rope_rot(dq, DIFF_HD // 2) * sd) * (DIFF_HD ** -0.5 * LOG2E)
    dk = dk * cd + _rope_rot(dk, DIFF_HD // 2) * sd
    dq_ref[...] = dq.astype(BF16)
    dk_ref[...] = dk.astype(BF16)
    dv_ref[0, :, 0] = dv.T.reshape(dv_ref.shape[1], LANES, dv.shape[0]).astype(BF16)


def _ev_in(x2, g, w0, qg, kvg, wq, wk, wv, cd, sd, cm, sm, seq):
    n, d = x2.shape
    tm = ATTN_BLOCK
    nt = seq // tm
    row = lambda i: (i, 0)
    full = lambda i: (0, 0)
    tab = lambda i: (i % nt, 0)
    wspec = lambda a: pl.BlockSpec(a.shape, full)
    vt_shape = lambda w: jax.ShapeDtypeStruct((n // seq, w // LANES, nt, LANES, tm), BF16)
    vt_spec = lambda w: pl.BlockSpec((1, w // LANES, 1, LANES, tm), lambda i: (i // nt, 0, i % nt, 0, 0))
    rows = lambda w: jax.ShapeDtypeStruct((n, w), BF16)
    rspec = lambda w: pl.BlockSpec((tm, w), row)
    wide = MLA_HEADS * LANES
    return pl.pallas_call(
        _ev_in_kernel,
        out_shape=[rows(wide), rows(wide), vt_shape(MLA_HEADS * MLA_V), rows(DIFF_W), rows(DIFF_W), vt_shape(DIFF_W)],
        grid=(n // tm,),
        in_specs=[pl.BlockSpec((tm, d), row), wspec(g), wspec(w0), wspec(qg), wspec(kvg), wspec(wq), wspec(wk),
                  wspec(wv)] + [pl.BlockSpec((tm, LANES), tab)] * 4,
        out_specs=[rspec(wide), rspec(wide), vt_spec(MLA_HEADS * MLA_V), rspec(DIFF_W), rspec(DIFF_W),
                   vt_spec(DIFF_W)],
        compiler_params=_cparams(("parallel",)),
        name="ev_in",
    )(x2, g, w0, qg, kvg, wq, wk, wv, cd, sd, cm, sm)


ONES_ROWS = 16


def _chunk_mask_t(shape):
    sh = int(math.log2(CHUNK))
    key_chunk = jnp.right_shift(lax.broadcasted_iota(jnp.int32, shape, 0), sh)
    q_chunk = jnp.right_shift(lax.broadcasted_iota(jnp.int32, shape, 1), sh)
    return key_chunk <= q_chunk


def _attn_streams(i, qs, key_of, val_of, k_ref, vt_ref, m_ref, acc_ref):
    bk = ATTN_BLOCK
    n = len(qs)
    m_ref[...] = jnp.full(m_ref.shape, NEG, F32)
    acc_ref[...] = jnp.zeros(acc_ref.shape, F32)
    dv = acc_ref.shape[1] - ONES_ROWS
    ones = jnp.ones((ONES_ROWS, bk), BF16)

    def step(j, masked):
        kb = k_ref[0, pl.ds(pl.multiple_of(j * bk, bk), bk), :]
        vt = vt_ref[0, :, j]
        st = [_dot_nt(key_of(kb, s), qs[s]) for s in range(n)]
        if masked:
            mask = _chunk_mask_t(st[0].shape)
            st = [jnp.where(mask, x, NEG) for x in st]
        m_prev = [m_ref[s] for s in range(n)]
        m_new = [jnp.maximum(m_prev[s], jnp.max(st[s], axis=0, keepdims=True)) for s in range(n)]
        p = [jnp.exp2(st[s] - m_new[s]).astype(BF16) for s in range(n)]
        pv = [_dot(jnp.concatenate([val_of(vt, s), ones], axis=0), p[s]) for s in range(n)]
        for s in range(n):
            acc_ref[s] = jnp.exp2(m_prev[s] - m_new[s]) * acc_ref[s] + pv[s]
            m_ref[s] = m_new[s]

    def body(j, carry):
        step(j, False)
        return carry

    lax.fori_loop(0, i, body, 0)
    step(i, True)
    accs = [acc_ref[s] for s in range(n)]
    return [a[:dv] / a[dv:dv + 1] for a in accs]


def _mla_attn_kernel(q_ref, k_ref, vt_ref, o_ref, m_ref, acc_ref):
    q = q_ref[0]
    n = q.shape[1] // LANES
    qs = [q[:, s * LANES:(s + 1) * LANES] for s in range(n)]
    key_of = lambda kb, s: kb[:, s * LANES:(s + 1) * LANES]
    val_of = lambda vt, s: vt[s // 2, (s % 2) * MLA_V:(s % 2 + 1) * MLA_V]
    outs = _attn_streams(pl.program_id(2), qs, key_of, val_of, k_ref, vt_ref, m_ref, acc_ref)
    o_ref[0] = jnp.concatenate(outs, axis=0).T.astype(BF16)


def _diff_attn_kernel(q_ref, k_ref, vt_ref, lam_ref, sub_ref, o_ref, m_ref, acc_ref, *, lam_init):
    q = q_ref[0]
    nh = q.shape[1] // LANES
    lane = lax.broadcasted_iota(jnp.int32, (q.shape[0], LANES), 1)
    qs = []
    for h in range(nh):
        q_h = q[:, h * LANES:(h + 1) * LANES]
        zero = jnp.zeros_like(q_h)
        qs += [jnp.where(lane < DIFF_HD, q_h, zero), jnp.where(lane >= DIFF_HD, q_h, zero)]
    key_of = lambda kb, s: kb[:, (s // 2) * LANES:(s // 2 + 1) * LANES]
    val_of = lambda vt, s: vt[s // 2]
    outs = _attn_streams(pl.program_id(2), qs, key_of, val_of, k_ref, vt_ref, m_ref, acc_ref)
    lam = lam_ref[...]
    s1 = jnp.sum(lam[0:1] * lam[1:2], axis=-1, keepdims=True)
    s2 = jnp.sum(lam[2:3] * lam[3:4], axis=-1, keepdims=True)
    lam_full = jnp.exp(s1) - jnp.exp(s2) + lam_init
    for h in range(nh):
        o = (outs[2 * h] - lam_full * outs[2 * h + 1]).T
        o_ref[0, :, h * LANES:(h + 1) * LANES] = (_rms(o, sub_ref[...]) * (1.0 - lam_init)).astype(BF16)


ATTN_STREAMS = 4


def _attn_scratch(bq, dv):
    return [pltpu.VMEM((ATTN_STREAMS, 1, bq), F32), pltpu.VMEM((ATTN_STREAMS, dv + ONES_ROWS, bq), F32)]


def _vt_spec(nk, groups):
    return pl.BlockSpec((1, groups, nk, LANES, ATTN_BLOCK), lambda bi, h, i: (bi, h, 0, 0, 0))


def _mla_attn(qf, kf, vt):
    b, s, _ = qf.shape
    bq = ATTN_BLOCK
    w = ATTN_STREAMS * LANES
    return pl.pallas_call(
        _mla_attn_kernel,
        out_shape=jax.ShapeDtypeStruct((b, s, MLA_HEADS * MLA_V), BF16),
        grid=(b, MLA_HEADS // ATTN_STREAMS, s // bq),
        in_specs=[pl.BlockSpec((1, bq, w), lambda bi, h, i: (bi, i, h)),
                  pl.BlockSpec((1, s, w), lambda bi, h, i: (bi, 0, h)),
                  _vt_spec(s // bq, ATTN_STREAMS // 2)],
        out_specs=pl.BlockSpec((1, bq, ATTN_STREAMS * MLA_V), lambda bi, h, i: (bi, i, h)),
        scratch_shapes=_attn_scratch(bq, MLA_V),
        compiler_params=_cparams(("parallel", "parallel", "arbitrary")),
        name="mla_attn",
    )(qf, kf, vt)


def _diff_attn(dq, dk, dvt, lam, subln, lam_init):
    b, s, _ = dq.shape
    bq = ATTN_BLOCK
    nh = ATTN_STREAMS // 2
    blk = lambda rows, im: pl.BlockSpec((1, rows, nh * LANES), im)
    return pl.pallas_call(
        functools.partial(_diff_attn_kernel, lam_init=lam_init),
        out_shape=jax.ShapeDtypeStruct((b, s, DIFF_HEADS * DIFF_V), BF16),
        grid=(b, DIFF_HEADS // nh, s // bq),
        in_specs=[blk(bq, lambda bi, h, i: (bi, i, h)), blk(s, lambda bi, h, i: (bi, 0, h)),
                  _vt_spec(s // bq, nh),
                  pl.BlockSpec(lam.shape, lambda bi, h, i: (0, 0)),
                  pl.BlockSpec(subln.shape, lambda bi, h, i: (0, 0))],
        out_specs=blk(bq, lambda bi, h, i: (bi, i, h)),
        scratch_shapes=_attn_scratch(bq, DIFF_V),
        compiler_params=_cparams(("parallel", "parallel", "arbitrary")),
        name="diff_attn",
    )(dq, dk, dvt, lam, subln)


def _mix_out_kernel(h_ref, a_ref, b_ref, w_ref, g_ref, rwh_ref, rwl_ref, rb_ref, tri_ref,
                    hout_ref, hf_ref, route_ref, idx_ref, cnt_ref):
    mixed = jnp.concatenate([a_ref[...], b_ref[...]], axis=-1)
    h = h_ref[...] + _dot(mixed, w_ref[...])
    hout_ref[...] = h
    hf = _rms(h, g_ref[...])
    hi = hf.astype(BF16)
    hf_ref[...] = hi
    lo = (hf - hi.astype(F32)).astype(BF16)
    z = _dot(hi, rwh_ref[...]) + _dot(lo, rwh_ref[...]) + _dot(hi, rwl_ref[...]) + rb_ref[...]
    lane_i = lax.broadcasted_iota(jnp.int32, z.shape, 1)
    lane = lane_i.astype(F32)
    big = float(LANES)
    is_g = lane_i < MOE_GROUPS
    zg = jnp.where(is_g, z, NEG)
    mg = jnp.max(zg, axis=-1, keepdims=True)
    g_idx = jnp.min(jnp.where(zg == mg, lane, big), axis=-1, keepdims=True)
    pg_top = 1.0 / jnp.sum(jnp.where(is_g, jnp.exp(zg - mg), 0.0), axis=-1, keepdims=True)
    grp_of_lane = jnp.right_shift(lane_i - MOE_GROUPS, int(math.log2(MOE_EPG))).astype(F32)
    in_grp = (lane_i >= MOE_GROUPS) & (lane_i < MOE_GROUPS + MOE_EXPERTS) & (grp_of_lane == g_idx)
    ze = jnp.where(in_grp, z, NEG)
    m1 = jnp.max(ze, axis=-1, keepdims=True)
    i1 = jnp.min(jnp.where(ze == m1, lane, big), axis=-1, keepdims=True)
    ze2 = jnp.where(lane == i1, NEG, ze)
    m2 = jnp.max(ze2, axis=-1, keepdims=True)
    i2 = jnp.min(jnp.where(ze2 == m2, lane, big), axis=-1, keepdims=True)
    r = jnp.exp(m2 - m1)
    w1 = 1.0 / (1.0 + r)
    w2 = r / (1.0 + r)
    e1 = i1 - float(MOE_GROUPS)
    e2 = i2 - float(MOE_GROUPS)
    @pl.when(pl.program_id(0) == 0)
    def _():
        cnt_ref[...] = jnp.zeros(cnt_ref.shape, F32)

    both = jnp.where((lane == i1) | (lane == i2), 1.0, 0.0)
    before = _dot(tri_ref[...], both.astype(BF16)) + cnt_ref[0:1, :]
    rank1 = jnp.sum(jnp.where(lane == i1, before, 0.0), axis=-1, keepdims=True)
    rank2 = jnp.sum(jnp.where(lane == i2, before, 0.0), axis=-1, keepdims=True)
    cnt_ref[...] = cnt_ref[...] + jnp.sum(both, axis=0, keepdims=True)
    vals = (e1, e2, pg_top * w1, pg_top * w2, rank1, rank2)
    out = jnp.zeros(z.shape, F32)
    for col, val in enumerate(vals):
        out = jnp.where(lane_i == col, val, out)
    route_ref[...] = out
    idx_ref[0] = out.T[0:8].astype(jnp.int32)


def _mix_out(h, a, b, w, g, rwh, rwl, rb, tri):
    n, d = h.shape
    tm = ROW_TILE
    row = lambda i: (i, 0)
    full = lambda i: (0, 0)
    wspec = lambda t: pl.BlockSpec(t.shape, full)
    return pl.pallas_call(
        _mix_out_kernel,
        out_shape=[jax.ShapeDtypeStruct((n, d), F32), jax.ShapeDtypeStruct((n, d), BF16),
                   jax.ShapeDtypeStruct((n, LANES), F32), jax.ShapeDtypeStruct((n // tm, 8, tm), jnp.int32),
                   jax.ShapeDtypeStruct((8, LANES), F32)],
        grid=(n // tm,),
        in_specs=[pl.BlockSpec((tm, d), row), pl.BlockSpec((tm, a.shape[1]), row), pl.BlockSpec((tm, b.shape[1]), row),
                  wspec(w), wspec(g), wspec(rwh), wspec(rwl), wspec(rb), wspec(tri)],
        out_specs=[pl.BlockSpec((tm, d), row), pl.BlockSpec((tm, d), row), pl.BlockSpec((tm, LANES), row),
                   pl.BlockSpec((1, 8, tm), lambda i: (i, 0, 0)), pl.BlockSpec((8, LANES), full)],
        compiler_params=_cparams(("arbitrary",)),
        name="mix_out",
    )(h, a, b, w, g, rwh, rwl, rb, tri)


def _expert_kernel(te_ref, nv_ref, x_ref, wg_ref, wu_ref, wd_ref, y_ref, wgb_ref, wub_ref, wdb_ref):
    t = pl.program_id(0)

    @pl.when((t == 0) | (te_ref[t] != te_ref[jnp.maximum(t - 1, 0)]))
    def _():
        wgb_ref[...] = wg_ref[0, 0].astype(BF16)
        wub_ref[...] = wu_ref[0, 0].astype(BF16)
        wdb_ref[...] = wd_ref[0, 0].astype(BF16)

    @pl.when(t < nv_ref[0])
    def _():
        x = x_ref[...]
        a = _dot(x, wgb_ref[...])
        up = _dot(x, wub_ref[...])
        act = (a * _sigmoid(a) * up).astype(BF16)
        y_ref[...] = _dot(act, wdb_ref[...]).astype(BF16)

    @pl.when(t >= nv_ref[0])
    def _():
        y_ref[...] = jnp.zeros(y_ref.shape, BF16)


def _experts(xs, tile_expert, n_valid, wg, wu, wd, layer):
    p, d = xs.shape
    tm = EXPERT_TILE
    ff = wg.shape[3]
    grid_spec = pltpu.PrefetchScalarGridSpec(
        num_scalar_prefetch=2,
        grid=(p // tm,),
        in_specs=[pl.BlockSpec((tm, d), lambda t, te, nv: (t, 0)),
                  pl.BlockSpec((1, 1, d, ff), lambda t, te, nv: (layer, te[t], 0, 0)),
                  pl.BlockSpec((1, 1, d, ff), lambda t, te, nv: (layer, te[t], 0, 0)),
                  pl.BlockSpec((1, 1, ff, d), lambda t, te, nv: (layer, te[t], 0, 0))],
        out_specs=pl.BlockSpec((tm, d), lambda t, te, nv: (t, 0)),
        scratch_shapes=[pltpu.VMEM((d, ff), BF16), pltpu.VMEM((d, ff), BF16), pltpu.VMEM((ff, d), BF16)],
    )
    return pl.pallas_call(
        _expert_kernel,
        out_shape=jax.ShapeDtypeStruct((p, d), BF16),
        grid_spec=grid_spec,
        compiler_params=_cparams(("arbitrary",)),
        name="experts",
    )(tile_expert, n_valid, xs, wg, wu, wd)


def _add_experts(h_ref, y1_ref, y2_ref, route_ref):
    route = route_ref[...]
    return h_ref[...] + route[:, 2:3] * y1_ref[...].astype(F32) + route[:, 3:4] * y2_ref[...].astype(F32)


def _final_kernel(h_ref, y1_ref, y2_ref, route_ref, g_ref, o_ref):
    o_ref[...] = _rms(_add_experts(h_ref, y1_ref, y2_ref, route_ref), g_ref[...])


def _final(h, y, route, g):
    n, d = h.shape
    tm = ROW_TILE
    row = lambda i: (i, 0)
    second = lambda i: (i + n // tm, 0)
    return pl.pallas_call(
        _final_kernel,
        out_shape=jax.ShapeDtypeStruct((n, d), F32),
        grid=(n // tm,),
        in_specs=[pl.BlockSpec((tm, d), row), pl.BlockSpec((tm, d), row), pl.BlockSpec((tm, d), second),
                  pl.BlockSpec((tm, LANES), row), pl.BlockSpec(g.shape, lambda i: (0, 0))],
        out_specs=pl.BlockSpec((tm, d), row),
        compiler_params=_cparams(("parallel",)),
        name="moe_final",
    )(h, y, y, route, g)


def _moe(hf, idx, counts, wg, wu, wd, layer):
    n, d = hf.shape
    tm = EXPERT_TILE
    ids = jnp.arange(MOE_EXPERTS, dtype=jnp.int32)
    counts = counts[0, MOE_GROUPS:MOE_GROUPS + MOE_EXPERTS].astype(jnp.int32)
    padded = ((counts + tm - 1) // tm) * tm
    ends = jnp.cumsum(padded)
    starts = ends - padded
    start_of = lambda e: jnp.sum(jnp.where(e[:, None] == ids[None, :], starts[None, :], 0), axis=-1)
    e1, e2 = idx[:, 0, :].reshape(n), idx[:, 1, :].reshape(n)
    dest1 = start_of(e1) + idx[:, 4, :].reshape(n)
    dest2 = start_of(e2) + idx[:, 5, :].reshape(n)
    n_rows = 2 * n + MOE_EXPERTS * tm
    tok = jnp.arange(n, dtype=jnp.int32)
    tok_for_row = (jnp.arange(n_rows, dtype=jnp.int32) % n).at[jnp.concatenate([dest1, dest2])].set(
        jnp.concatenate([tok, tok]), mode="promise_in_bounds", unique_indices=True)
    take = functools.partial(jnp.take, axis=0, mode="clip")
    tile_start = jnp.arange(n_rows // tm, dtype=jnp.int32) * tm
    tile_expert = jnp.minimum(jnp.sum((ends[None, :] <= tile_start[:, None]).astype(jnp.int32), axis=1),
                              MOE_EXPERTS - 1)
    n_valid = (ends[-1] // tm).astype(jnp.int32).reshape(1)
    xs = take(hf, tok_for_row)
    ys = _experts(xs, tile_expert, n_valid, wg, wu, wd, layer)
    return take(ys, jnp.concatenate([dest1, dest2]))


def _od_in_kernel(h_ref, y1_ref, y2_ref, route_ref, g_ref, w_ref, hout_ref, ur_ref, uh_ref):
    h = _add_experts(h_ref, y1_ref, y2_ref, route_ref)
    hout_ref[...] = h
    hn = _rms(h, g_ref[...]).astype(BF16)
    u = _dot(hn, w_ref[...])
    ur_ref[...] = u[:, :RWKV_IN].astype(BF16)
    uh_ref[...] = u[:, RWKV_IN:].astype(BF16)


def _od_in(h, y, route, g, w):
    n, d = h.shape
    tm = ROW_TILE
    row = lambda i: (i, 0)
    rspec = lambda width: pl.BlockSpec((tm, width), row)
    return pl.pallas_call(
        _od_in_kernel,
        out_shape=[jax.ShapeDtypeStruct((n, d), F32), jax.ShapeDtypeStruct((n, RWKV_IN), BF16),
                   jax.ShapeDtypeStruct((n, HGRN_IN), BF16)],
        grid=(n // tm,),
        in_specs=[rspec(d), rspec(d), pl.BlockSpec((tm, d), lambda i: (i + n // tm, 0)), rspec(LANES),
                  pl.BlockSpec(g.shape, lambda i: (0, 0)), pl.BlockSpec(w.shape, lambda i: (0, 0))],
        out_specs=[rspec(d), rspec(RWKV_IN), rspec(HGRN_IN)],
        compiler_params=_cparams(("parallel",)),
        name="od_in",
    )(h, y, y, route, g, w)


def _rwkv_kernel(u_ref, mu_ref, w0_ref, w2_ref, a0_ref, a2_ref, g2_ref, kk_ref, ka_ref, rk_ref,
                 lnw_ref, lnb_ref, tri_ref, seg_ref, o_ref, prev_ref, ht_ref):
    c = pl.program_id(1)

    @pl.when(c == 0)
    def _():
        prev_ref[...] = jnp.zeros(prev_ref.shape, F32)
        ht_ref[...] = jnp.zeros(ht_ref.shape, F32)

    nb = u_ref.shape[0]
    u = u_ref[...].reshape(nb * CHUNK, RWKV_IN).astype(F32)
    rows = lax.broadcasted_iota(jnp.int32, u.shape, 0)
    u_prev = pltpu.roll(u, 1, 0)
    for bi in range(nb):
        u_prev = jnp.where(rows == bi * CHUNK, prev_ref[bi], u_prev)
        prev_ref[bi] = u[(bi + 1) * CHUNK - 1:(bi + 1) * CHUNK, :]
    us = u + (u_prev - u) * mu_ref[...]
    w = RWKV_W
    r = us[:, 0:w]
    k = us[:, w:2 * w]
    v = us[:, 2 * w:3 * w]
    x12 = us[:, 3 * w:3 * w + LANES]
    xg = us[:, 3 * w + LANES:]
    seg_pair = seg_ref[...]

    def seg(x, split=False):
        mm = _split_dot if split else (lambda v, m: _dot(v.astype(BF16), m))
        return jnp.concatenate([mm(x[:, p * LANES:(p + 1) * LANES], seg_pair)
                                for p in range(RWKV_HEADS // 2)], axis=1)

    tri = tri_ref[...]

    wl = w0_ref[...] + _dot(jnp.tanh(x12).astype(BF16), w2_ref[...])
    nwl = -wl
    softplus = jnp.maximum(nwl, 0.0) + jnp.log1p(jnp.exp(-jnp.abs(nwl)))
    lw = -jnp.exp(-softplus - 0.5)
    a = _sigmoid(a0_ref[...] + _dot(x12.astype(BF16), a2_ref[...]))
    g = _dot(_sigmoid(xg).astype(BF16), g2_ref[...])
    kk = k * kk_ref[...]
    kk = kk * lax.rsqrt(jnp.maximum(seg(kk * kk, split=True), 1e-24))
    k2 = k * (1.0 + (a - 1.0) * ka_ref[...])
    a_in = -kk
    b_in = kk * a

    lg = _split_dot(lw, tri, left=True)
    rbs = [slice(bi * CHUNK, (bi + 1) * CHUNK) for bi in range(nb)]
    lg_c = [lg[(bi + 1) * CHUNK - 1:(bi + 1) * CHUNK, :] for bi in range(nb)]
    e_neg = jnp.exp(-lg)
    e_rel = jnp.concatenate([jnp.exp(lg_c[bi] - lg[rbs[bi]]) for bi in range(nb)], axis=0)
    g_c = [jnp.exp(x) for x in lg_c]
    at = a_in * jnp.exp(lg - lw)
    rt = r * jnp.exp(lg)
    kt = k2 * e_neg
    bt = b_in * e_neg
    kh = k2 * e_rel
    bh = b_in * e_rel

    c2 = 2 * CHUNK
    ri = lax.broadcasted_iota(jnp.int32, (c2, c2), 0)
    ci = lax.broadcasted_iota(jnp.int32, (c2, c2), 1)
    sh = int(math.log2(CHUNK))
    same = jnp.right_shift(ri, sh) == jnp.right_shift(ci, sh)
    strict = same & (ri > ci)
    incl = same & (ri >= ci)
    eye = jnp.where(ri == ci, 1.0, 0.0).astype(F32)
    lane = lax.broadcasted_iota(jnp.int32, (CHUNK, LANES), 1)
    lo_half = lane < RWKV_HD

    def bd(x):
        return jnp.concatenate([jnp.where(lo_half, x, 0.0), jnp.where(lo_half, 0.0, x)], axis=0)

    npair = RWKV_HEADS // 2
    pairs = range(nb * npair)
    sls = [(rbs[c // npair], slice((c % npair) * LANES, (c % npair + 1) * LANES)) for c in pairs]
    bdb = lambda x: [bd(x[sl]).astype(BF16) for sl in sls]
    at_b = [bd(at[sl]) for sl in sls]
    rt_b = [bd(rt[sl]) for sl in sls]
    kt_b, bt_b, kh_b, bh_b, v_b = bdb(kt), bdb(bt), bdb(kh), bdb(bh), bdb(v)
    hts = [ht_ref[p] for p in pairs]
    htb = [t.astype(BF16) for t in hts]
    sc = [_dot_nt(jnp.concatenate([at_b[p], rt_b[p]], axis=0).astype(BF16),
                  jnp.concatenate([kt_b[p], bt_b[p]], axis=0)) for p in pairs]
    a_ak = [jnp.where(strict, s[:c2, :c2], 0.0).astype(BF16) for s in sc]
    a_ab = [jnp.where(strict, s[:c2, c2:], 0.0) for s in sc]
    a_rk = [jnp.where(incl, s[c2:, :c2], 0.0).astype(BF16) for s in sc]
    a_rb = [jnp.where(incl, s[c2:, c2:], 0.0).astype(BF16) for s in sc]
    x2 = [_dot(a_ak[p], v_b[p]) for p in pairs]
    e1 = [_dot(a_rk[p], v_b[p]) for p in pairs]
    d1 = [_dot_tn(v_b[p], kh_b[p]) for p in pairs]
    pw = a_ab
    t_inv = [eye + a for a in a_ab]
    for _ in range(int(math.log2(CHUNK)) - 1):
        pwb = [x.astype(BF16) for x in pw]
        pw = [_dot(x, x) for x in pwb]
        t_inv = [t_inv[p] + _dot(t_inv[p].astype(BF16), pw[p].astype(BF16)) for p in pairs]
    txb = [_dot(t_inv[p].astype(BF16), jnp.concatenate([at_b[p], x2[p]], axis=1).astype(BF16)).astype(BF16)
           for p in pairs]
    qe = [_dot(a_rb[p], txb[p]) for p in pairs]
    gd = [_dot_tn(txb[p], bh_b[p]) for p in pairs]
    y_b = [_dot_nt((rt_b[p] + qe[p][:, :LANES]).astype(BF16), htb[p]) + e1[p] + qe[p][:, LANES:] for p in pairs]
    for p in pairs:
        upd = hts[p] * g_c[p // npair][:, sls[p][1]] + _dot(htb[p], gd[p][:LANES].astype(BF16)) + d1[p] + gd[p][LANES:]
        ht_ref[p] = jnp.where(same, upd, 0.0)
    y_p = [t[:CHUNK] + t[CHUNK:] for t in y_b]
    y = jnp.concatenate([jnp.concatenate(y_p[bi * npair:(bi + 1) * npair], axis=1) for bi in range(nb)],
                        axis=0)

    inv_n = 1.0 / RWKV_HD
    mean = seg(y) * inv_n
    dlt = y - mean
    var = seg(dlt * dlt) * inv_n
    yn = dlt * lax.rsqrt(var + RWKV_GN_EPS) * lnw_ref[...] + lnb_ref[...]
    bonus = seg(r * k2 * rk_ref[...]) * v
    o_ref[...] = ((yn + bonus) * g).astype(BF16).reshape(nb, CHUNK, RWKV_W)


def _rwkv(u, mu, w0, w2p, a0, a2p, g2, k_k, k_a, r_k, ln_w, ln_b, tri, seg):
    b, s, _ = u.shape
    nb = tri.shape[0] // CHUNK
    full = lambda bi, ci: (0, 0)
    wspec = lambda t: pl.BlockSpec(t.shape, full)
    params = [mu, w0, w2p, a0, a2p, g2, k_k, k_a, r_k, ln_w, ln_b, tri, seg]
    return pl.pallas_call(
        _rwkv_kernel,
        out_shape=jax.ShapeDtypeStruct((b, s, RWKV_W), BF16),
        grid=(b // nb, s // CHUNK),
        in_specs=[pl.BlockSpec((nb, CHUNK, RWKV_IN), lambda bi, ci: (bi, ci, 0))] + [wspec(t) for t in params],
        out_specs=pl.BlockSpec((nb, CHUNK, RWKV_W), lambda bi, ci: (bi, ci, 0)),
        scratch_shapes=[pltpu.VMEM((nb, 1, RWKV_IN), F32), pltpu.VMEM((nb * RWKV_HEADS // 2, LANES, LANES), F32)],
        compiler_params=_cparams(("parallel", "arbitrary")),
        name="rwkv7",
    )(u, *params)


def _hgrn_kernel(u_ref, lb_ref, on_ref, tri_ref, ones_ref, sel_ref, o_ref, st_ref):
    c = pl.program_id(1)

    @pl.when(c == 0)
    def _():
        st_ref[...] = jnp.zeros(st_ref.shape, F32)

    nb = u_ref.shape[0]
    u = u_ref[...].reshape(nb * CHUNK, HGRN_IN).astype(F32)
    lbp = lb_ref[...]
    mx = jnp.maximum(lbp[0:1], lbp[1:2])
    e0 = jnp.exp(lbp[0:1] - mx)
    e1 = jnp.exp(lbp[1:2] - mx)
    p0 = e0 / (e0 + e1)
    p1 = e1 / (e0 + e1)
    lb = (p0 + p1) - p0
    tri = tri_ref[...]
    ones = ones_ref[...]
    sel = sel_ref[...]
    d = HGRN_D
    w = HGRN_W
    heads = range(HGRN_HEADS)
    sls = [slice(h * d, (h + 1) * d) for h in heads]
    q, z, iv, gt = u[:, :w], u[:, w:2 * w], u[:, 2 * w:3 * w], u[:, 3 * w:]
    qs = q * _sigmoid(q)
    log_sig = jnp.minimum(z, 0.0) - jnp.log1p(jnp.exp(-jnp.abs(z)))
    x1 = jnp.log(lb)
    x2 = jnp.log1p(-lb) + log_sig
    log_f = jnp.maximum(x1, x2) + jnp.log1p(jnp.exp(-jnp.abs(x1 - x2)))
    key = (1.0 - lb) * _sigmoid(-z)
    bc = _split_dot(log_f, tri, left=True)
    seqs = range(nb)
    rbs = [slice(i * CHUNK, (i + 1) * CHUNK) for i in seqs]
    b_l = [bc[(i + 1) * CHUNK - 1:(i + 1) * CHUNK] for i in seqs]
    ivb = iv.astype(BF16)
    qe = (qs * jnp.exp(bc)).astype(BF16)
    kl = (key * jnp.concatenate([jnp.exp(b_l[i] - bc[rbs[i]]) for i in seqs], axis=0)).astype(BF16)
    nh = HGRN_HEADS
    sts = [st_ref[c] for c in range(nb * nh)]
    o_inter = [_dot_nt(qe[rbs[c // nh], sls[c % nh]], sts[c].astype(BF16)) for c in range(nb * nh)]
    for c in range(nb * nh):
        i, sl = c // nh, sls[c % nh]
        st_ref[c] = sts[c] * jnp.exp(b_l[i])[:, sl] + _dot_tn(ivb[rbs[i], sl], kl[rbs[i], sl])

    nblk = CHUNK // SUB
    hrow = lax.broadcasted_iota(jnp.int32, (HALF, w), 0)
    p_all, v_tile, sc_off, sc_half = [], [], [], []
    for blk in range(nb * nblk):
        base = (blk // nblk) * CHUNK
        r0 = base + (blk % nblk) * SUB
        b_i, q_i, k_i = bc[r0:r0 + SUB], qs[r0:r0 + SUB], key[r0:r0 + SUB]
        rows = []
        for t in range(SUB):
            lo = (t // HALF) * HALF
            rows.append(q_i[t:t + 1] * k_i[lo:lo + HALF]
                        * jnp.where(hrow <= t - lo, jnp.exp(b_i[t:t + 1] - b_i[lo:lo + HALF]), 0.0))
        p_all.append(jnp.concatenate(rows, axis=0).astype(BF16))
        v_tile.append(jnp.concatenate([iv[r0:r0 + HALF]] * HALF + [iv[r0 + HALF:r0 + SUB]] * HALF, axis=0))
        b_h = b_i[HALF - 1:HALF]
        qh = (q_i[HALF:] * jnp.exp(b_i[HALF:] - b_h)).astype(BF16)
        kh = (k_i[:HALF] * jnp.exp(b_h - b_i[:HALF])).astype(BF16)
        sc_half.append([_dot_nt(qh[:, sl], kh[:, sl]).astype(BF16) for sl in sls])
        if r0 > base:
            b_m = bc[r0 - 1:r0]
            qp = (q_i * jnp.exp(b_i - b_m)).astype(BF16)
            kp = (key[base:r0] * jnp.exp(b_m - bc[base:r0])).astype(BF16)
            sc_off.append([_dot_nt(qp[:, sl], kp[:, sl]).astype(BF16) for sl in sls])
        else:
            sc_off.append(None)
    rs = [jnp.concatenate([_dot(p[:, sl], ones) for sl in sls], axis=1) for p in p_all]
    rv = [(rs[blk] * v_tile[blk]).astype(BF16) for blk in range(nb * nblk)]
    for c in range(nb * nh):
        i, sl = c // nh, sls[c % nh]
        parts = []
        for blk in range(i * nblk, (i + 1) * nblk):
            r0 = i * CHUNK + (blk % nblk) * SUB
            lower = _dot(sc_half[blk][c % nh], iv[r0:r0 + HALF, sl].astype(BF16))
            o_i = _dot(sel, rv[blk][:, sl]) + jnp.concatenate([jnp.zeros_like(lower), lower], axis=0)
            if sc_off[blk] is not None:
                o_i = o_i + _dot(sc_off[blk][c % nh], ivb[i * CHUNK:i * CHUNK + (blk % nblk) * SUB, sl])
            parts.append(o_i)
        o = o_inter[c] + jnp.concatenate(parts, axis=0)
        on = _rms(o, on_ref[:, sl])
        g_h = gt[rbs[i], sl]
        o_ref[i, :, sl] = (on * (g_h * _sigmoid(g_h))).astype(BF16)


def _hgrn(u, lbp, o_norm, tri, ones, sel):
    b, s, _ = u.shape
    nb = tri.shape[0] // CHUNK
    full = lambda bi, ci: (0, 0)
    wspec = lambda t: pl.BlockSpec(t.shape, full)
    return pl.pallas_call(
        _hgrn_kernel,
        out_shape=jax.ShapeDtypeStruct((b, s, HGRN_W), BF16),
        grid=(b // nb, s // CHUNK),
        in_specs=[pl.BlockSpec((nb, CHUNK, HGRN_IN), lambda bi, ci: (bi, ci, 0)), wspec(lbp), wspec(o_norm),
                  wspec(tri), wspec(ones), wspec(sel)],
        out_specs=pl.BlockSpec((nb, CHUNK, HGRN_W), lambda bi, ci: (bi, ci, 0)),
        scratch_shapes=[pltpu.VMEM((nb * HGRN_HEADS, HGRN_D, HGRN_D), F32)],
        compiler_params=_cparams(("parallel", "arbitrary")),
        name="hgrn2",
    )(u, lbp, o_norm, tri, ones, sel)


def _rope_tables(seq):
    pos = jnp.arange(seq, dtype=F32)[:, None]

    def cs(half):
        inv = jnp.power(ROPE_THETA, -jnp.arange(half, dtype=F32) / half)
        ang = pos * inv[None, :]
        return jnp.cos(ang), jnp.sin(ang)

    cd, sd = cs(DIFF_HD // 2)
    cos_d = jnp.tile(jnp.concatenate([cd, cd], axis=1), (1, LANES // DIFF_HD))
    sin_d = jnp.tile(jnp.concatenate([-sd, sd], axis=1), (1, LANES // DIFF_HD))
    cm, sm = cs(MLA_ROPE // 2)
    one = jnp.ones((seq, MLA_NOPE), F32)
    tail = LANES - MLA_QK
    cos_m = jnp.concatenate([one, cm, cm, jnp.ones((seq, tail), F32)], axis=1)
    sin_m = jnp.concatenate([0 * one, -sm, sm, jnp.zeros((seq, tail), F32)], axis=1)
    return cos_d, sin_d, cos_m, sin_m


def _router_weights(w_group, b_group, w_expert, b_expert):
    d = w_group.shape[0]
    pad = LANES - MOE_GROUPS - MOE_EXPERTS
    w = jnp.concatenate([w_group, w_expert, jnp.zeros((d, pad), F32)], axis=1)
    bias = jnp.concatenate([b_group, b_expert, jnp.zeros((pad,), F32)])[None, :]
    hi = w.astype(BF16)
    lo = (w - hi.astype(F32)).astype(BF16)
    return hi, lo, bias


def kernel(x, norm_mix, norm_ffn, norm_final, ev_w_in, ev_w_out, mla_q_norm, mla_w_q_up, mla_kv_norm, mla_w_kv_up, diff_lambda, diff_subln, od_w_in, od_w_out, rwkv_mu, rwkv_w0, rwkv_w2, rwkv_a0, rwkv_a2, rwkv_g2, rwkv_k_k, rwkv_k_a, rwkv_r_k, rwkv_ln_w, rwkv_ln_b, hgrn_lb, hgrn_o_norm, moe_w_group, moe_b_group, moe_w_expert, moe_b_expert, moe_w_gate, moe_w_up, moe_w_down):
    b, s, d = x.shape
    n = b * s
    assert norm_mix.shape[0] == 2 and hgrn_lb.shape[0] == 2
    assert s % ATTN_BLOCK == 0 and s % ROW_TILE == 0 and ATTN_BLOCK % CHUNK == 0
    row2 = lambda t: t.reshape(1, -1)
    h = x.reshape(n, d)

    w_in = ev_w_in[0]
    o1, o2, o3 = MLA_LORA, 2 * MLA_LORA, 2 * MLA_LORA + MLA_ROPE
    kr_pad = jnp.zeros((d, LANES), F32).at[:, MLA_NOPE:MLA_QK].set(w_in[:, o2:o3])
    w0 = jnp.concatenate([w_in[:, :o2], kr_pad, w_in[:, o3:]], axis=1).astype(BF16)
    wq = mla_w_q_up[0].reshape(MLA_LORA, MLA_HEADS, MLA_QK)
    wq = jnp.pad(wq, ((0, 0), (0, 0), (0, LANES - MLA_QK))).reshape(MLA_LORA, MLA_HEADS * LANES).astype(BF16)
    wkv = mla_w_kv_up[0].reshape(MLA_LORA, MLA_HEADS, MLA_NOPE + MLA_V)
    wk = jnp.pad(wkv[:, :, :MLA_NOPE], ((0, 0), (0, 0), (0, LANES - MLA_NOPE)))
    wk = wk.reshape(MLA_LORA, MLA_HEADS * LANES).astype(BF16)
    wv = wkv[:, :, MLA_NOPE:].reshape(MLA_LORA, MLA_HEADS * MLA_V).astype(BF16)
    cos_d, sin_d, cos_m, sin_m = _rope_tables(s)
    qf, kf, vm, dq, dk, dv = _ev_in(h, row2(norm_mix[0]), w0, row2(mla_q_norm[0]), row2(mla_kv_norm[0]),
                                    wq, wk, wv, cos_d, sin_d, cos_m, sin_m, s)
    r3 = lambda t: t.reshape(b, s, t.shape[-1])
    o_mla = _mla_attn(r3(qf), r3(kf), vm)
    lam_init = 0.8 - 0.6 * math.exp(-0.3 * 0)
    o_diff = _diff_attn(r3(dq), r3(dk), dv, diff_lambda[0], row2(diff_subln[0]), lam_init)
    rwh, rwl, rb = _router_weights(moe_w_group[0], moe_b_group[0], moe_w_expert[0], moe_b_expert[0])
    ti = jnp.arange(ROW_TILE)
    tri_rows = (ti[None, :] < ti[:, None]).astype(BF16)
    h, hf, route, ridx, counts = _mix_out(h, o_mla.reshape(n, -1), o_diff.reshape(n, -1), ev_w_out[0].astype(BF16),
                                    row2(norm_ffn[0]), rwh, rwl, rb, tri_rows)
    y = _moe(hf, ridx, counts, moe_w_gate, moe_w_up, moe_w_down, 0)

    h, ur, uh = _od_in(h, y, route, row2(norm_mix[1]), od_w_in[0].astype(BF16))
    zpad = jnp.zeros((RWKV_DECAY_LORA, RWKV_W), F32)
    w2p = jnp.concatenate([rwkv_w2[0], zpad], axis=0).astype(BF16)
    a2p = jnp.concatenate([zpad, rwkv_a2[0]], axis=0).astype(BF16)
    ci = jnp.arange(CHUNK)
    tri = (ci[None, :] <= ci[:, None]).astype(BF16)
    li = jnp.arange(LANES) // RWKV_HD
    seg = (li[:, None] == li[None, :]).astype(BF16)
    o_c = _rwkv(ur.reshape(b, s, RWKV_IN), row2(rwkv_mu[0]), row2(rwkv_w0[0]), w2p, row2(rwkv_a0[0]), a2p,
                rwkv_g2[0].astype(BF16), row2(rwkv_k_k[0]), row2(rwkv_k_a[0]), row2(rwkv_r_k[0]),
                row2(rwkv_ln_w[0]), row2(rwkv_ln_b[0]),
                jnp.kron(jnp.eye(math.gcd(b, RWKV_SEQS), dtype=BF16), tri), seg)
    pt = jnp.arange(SUB * HALF) // HALF
    ps = jnp.arange(SUB * HALF) % HALF
    sel = ((pt[None, :] == jnp.arange(SUB)[:, None]) & (ps <= pt % HALF)[None, :]).astype(BF16)
    o_d = _hgrn(uh.reshape(b, s, HGRN_IN), hgrn_lb, row2(hgrn_o_norm[0]),
                jnp.kron(jnp.eye(math.gcd(b, HGRN_SEQS), dtype=BF16), tri), jnp.ones((LANES, LANES), BF16), sel)
    rwh, rwl, rb = _router_weights(moe_w_group[1], moe_b_group[1], moe_w_expert[1], moe_b_expert[1])
    h, hf, route, ridx, counts = _mix_out(h, o_c.reshape(n, -1), o_d.reshape(n, -1), od_w_out[0].astype(BF16),
                                    row2(norm_ffn[1]), rwh, rwl, rb, tri_rows)
    y = _moe(hf, ridx, counts, moe_w_gate, moe_w_up, moe_w_down, 1)
    out = _final(h, y, route, row2(norm_final))
    return out.reshape(b, s, d)
```

```python
import functools
import math

import jax
import jax.numpy as jnp
from jax import lax
from jax.experimental import pallas as pl
from jax.experimental.pallas import tpu as pltpu

F32 = jnp.float32
BF16 = jnp.bfloat16

CHUNK = 64
ROPE_THETA = 10000.0
NORM_EPS = 1e-6
MLA_HEADS = 8
MLA_LORA = 256
MLA_NOPE = 64
MLA_ROPE = 32
MLA_V = 64
MLA_QK = MLA_NOPE + MLA_ROPE
DIFF_HEADS = 4
DIFF_HD = 64
DIFF_V = 2 * DIFF_HD
DIFF_W = DIFF_HEADS * 2 * DIFF_HD
RWKV_HEADS = 8
RWKV_HD = 64
RWKV_W = RWKV_HEADS * RWKV_HD
RWKV_DECAY_LORA = 64
RWKV_A_LORA = 64
RWKV_GATE_LORA = 128
RWKV_IN = 3 * RWKV_W + RWKV_DECAY_LORA + RWKV_A_LORA + RWKV_GATE_LORA
RWKV_GN_EPS = 64e-5
HGRN_HEADS = 4
HGRN_D = 128
HGRN_W = HGRN_HEADS * HGRN_D
HGRN_IN = 4 * HGRN_W
MOE_GROUPS = 4
MOE_EPG = 8
MOE_EXPERTS = MOE_GROUPS * MOE_EPG

LANES = 128
ROW_TILE = 512
ATTN_BLOCK = 512
EXPERT_TILE = 512
SUB = 16
HALF = 8
RWKV_SEQS = 4
HGRN_SEQS = 4
V7X_VMEM_BYTES = 64 * 1024 * 1024
VMEM_LIMIT = V7X_VMEM_BYTES * 3 // 4
NEG = -1e30
LOG2E = math.log2(math.e)


def _cparams(sem):
    return pltpu.CompilerParams(dimension_semantics=sem, vmem_limit_bytes=VMEM_LIMIT)


def _dot(a, b):
    return jnp.dot(a, b, preferred_element_type=F32)


def _dot_nt(a, b):
    return lax.dot_general(a, b, (((1,), (1,)), ((), ())), preferred_element_type=F32)


def _dot_tn(a, b):
    return lax.dot_general(a, b, (((0,), (0,)), ((), ())), preferred_element_type=F32)


def _rms(x, g):
    return x * lax.rsqrt(jnp.mean(x * x, axis=-1, keepdims=True) + NORM_EPS) * g


def _sigmoid(x):
    return 1.0 / (1.0 + jnp.exp(-x))


def _split_dot(x, w_bf16, left=False):
    hi = x.astype(BF16)
    lo = (x - hi.astype(F32)).astype(BF16)
    if left:
        return _dot(w_bf16, hi) + _dot(w_bf16, lo)
    return _dot(hi, w_bf16) + _dot(lo, w_bf16)


def _tile_lanes(t, n):
    return jnp.concatenate([t] * n, axis=-1) if n > 1 else t


def _rope_rot(x, half):
    w = x.shape[-1]
    lane = lax.broadcasted_iota(jnp.int32, x.shape, x.ndim - 1)
    first = (lane & (2 * half - 1)) < half
    return jnp.where(first, pltpu.roll(x, w - half, x.ndim - 1), pltpu.roll(x, half, x.ndim - 1))


def _ev_in_kernel(x_ref, g_ref, w0_ref, qg_ref, kvg_ref, wq_ref, wk_ref, wv_ref,
                  cd_ref, sd_ref, cm_ref, sm_ref,
                  qf_ref, kf_ref, vm_ref, dq_ref, dk_ref, dv_ref):
    hn = _rms(x_ref[...], g_ref[...]).astype(BF16)
    u = _dot(hn, w0_ref[...])
    o1, o2 = 2 * MLA_LORA, 2 * MLA_LORA + LANES
    c_q = u[:, :MLA_LORA]
    c_kv = u[:, MLA_LORA:o1]
    kr = u[:, o1:o2]
    dq = u[:, o2:o2 + DIFF_W]
    dk = u[:, o2 + DIFF_W:o2 + 2 * DIFF_W]
    dv = u[:, o2 + 2 * DIFF_W:]
    cqn = _rms(c_q, qg_ref[...]).astype(BF16)
    ckn = _rms(c_kv, kvg_ref[...]).astype(BF16)
    q = _dot(cqn, wq_ref[...])
    k = _dot(ckn, wk_ref[...])
    v = _dot(ckn, wv_ref[...])
    cm = cm_ref[...]
    sm = sm_ref[...]
    m_half = MLA_ROPE // 2
    lane = lax.broadcasted_iota(jnp.int32, q.shape, 1) & (LANES - 1)
    first = (lane >= MLA_NOPE) & (lane < MLA_NOPE + m_half)
    rot_q = jnp.where(first, pltpu.roll(q, q.shape[1] - m_half, 1), pltpu.roll(q, m_half, 1))
    q = (q * _tile_lanes(cm, MLA_HEADS) + rot_q * _tile_lanes(sm, MLA_HEADS)) * (MLA_QK ** -0.5 * LOG2E)
    lane1 = lax.broadcasted_iota(jnp.int32, kr.shape, 1)
    first1 = (lane1 >= MLA_NOPE) & (lane1 < MLA_NOPE + m_half)
    rot_k = jnp.where(first1, pltpu.roll(kr, LANES - m_half, 1), pltpu.roll(kr, m_half, 1))
    kr = kr * cm + rot_k * sm
    k = k + _tile_lanes(kr, MLA_HEADS)
    qf_ref[...] = q.astype(BF16)
    kf_ref[...] = k.astype(BF16)
    vm_ref[0, :, 0] = v.T.reshape(vm_ref.shape[1], LANES, v.shape[0]).astype(BF16)
    cd = _tile_lanes(cd_ref[...], DIFF_W // LANES)
    sd = _tile_lanes(sd_ref[...], DIFF_W // LANES)
    dq = (dq * cd + _rope_rot(dq, DIFF_HD // 2) * sd) * (DIFF_HD ** -0.5 * LOG2E)
    dk = dk * cd + _rope_rot(dk, DIFF_HD // 2) * sd
    dq_ref[...] = dq.astype(BF16)
    dk_ref[...] = dk.astype(BF16)
    dv_ref[0, :, 0] = dv.T.reshape(dv_ref.shape[1], LANES, dv.shape[0]).astype(BF16)


def _ev_in(x2, g, w0, qg, kvg, wq, wk, wv, cd, sd, cm, sm, seq):
    n, d = x2.shape
    tm = ATTN_BLOCK
    nt = seq // tm
    row = lambda i: (i, 0)
    full = lambda i: (0, 0)
    tab = lambda i: (i % nt, 0)
    wspec = lambda a: pl.BlockSpec(a.shape, full)
    vt_shape = lambda w: jax.ShapeDtypeStruct((n // seq, w // LANES, nt, LANES, tm), BF16)
    vt_spec = lambda w: pl.BlockSpec((1, w // LANES, 1, LANES, tm), lambda i: (i // nt, 0, i % nt, 0, 0))
    rows = lambda w: jax.ShapeDtypeStruct((n, w), BF16)
    rspec = lambda w: pl.BlockSpec((tm, w), row)
    wide = MLA_HEADS * LANES
    return pl.pallas_call(
        _ev_in_kernel,
        out_shape=[rows(wide), rows(wide), vt_shape(MLA_HEADS * MLA_V), rows(DIFF_W), rows(DIFF_W), vt_shape(DIFF_W)],
        grid=(n // tm,),
        in_specs=[pl.BlockSpec((tm, d), row), wspec(g), wspec(w0), wspec(qg), wspec(kvg), wspec(wq), wspec(wk),
                  wspec(wv)] + [pl.BlockSpec((tm, LANES), tab)] * 4,
        out_specs=[rspec(wide), rspec(wide), vt_spec(MLA_HEADS * MLA_V), rspec(DIFF_W), rspec(DIFF_W),
                   vt_spec(DIFF_W)],
        compiler_params=_cparams(("parallel",)),
        name="ev_in",
    )(x2, g, w0, qg, kvg, wq, wk, wv, cd, sd, cm, sm)


ONES_ROWS = 16


def _chunk_mask_t(shape):
    sh = int(math.log2(CHUNK))
    key_chunk = jnp.right_shift(lax.broadcasted_iota(jnp.int32, shape, 0), sh)
    q_chunk = jnp.right_shift(lax.broadcasted_iota(jnp.int32, shape, 1), sh)
    return key_chunk <= q_chunk


def _attn_streams(i, qs, key_of, val_of, k_ref, vt_ref, m_ref, acc_ref, st_ref=None):
    bk = ATTN_BLOCK
    n = len(qs)
    m_ref[...] = jnp.full(m_ref.shape, NEG, F32)
    acc_ref[...] = jnp.zeros(acc_ref.shape, F32)
    dv = acc_ref.shape[1] - ONES_ROWS
    ones = jnp.ones((ONES_ROWS, bk), BF16)

    def qk(j):
        kb = k_ref[0, pl.ds(pl.multiple_of(j * bk, bk), bk), :]
        return [_dot_nt(key_of(kb, s), qs[s]) for s in range(n)]

    def update(j, st, masked):
        vt = vt_ref[0, :, j]
        if masked:
            mask = _chunk_mask_t(st[0].shape)
            st = [jnp.where(mask, x, NEG) for x in st]
        m_prev = [m_ref[s] for s in range(n)]
        m_new = [jnp.maximum(m_prev[s], jnp.max(st[s], axis=0, keepdims=True)) for s in range(n)]
        p = [jnp.exp2(st[s] - m_new[s]).astype(BF16) for s in range(n)]
        pv = [_dot(jnp.concatenate([val_of(vt, s), ones], axis=0), p[s]) for s in range(n)]
        for s in range(n):
            acc_ref[s] = jnp.exp2(m_prev[s] - m_new[s]) * acc_ref[s] + pv[s]
            m_ref[s] = m_new[s]

    if st_ref is None:
        def body(j, carry):
            update(j, qk(j), False)
            return carry

        lax.fori_loop(0, i, body, 0)
        update(i, qk(i), True)
    else:
        for s, x in enumerate(qk(0)):
            st_ref[s] = x

        def body(j, carry):
            nxt = qk(j + 1)
            update(j, [st_ref[s] for s in range(n)], False)
            for s in range(n):
                st_ref[s] = nxt[s]
            return carry

        lax.fori_loop(0, i, body, 0)
        update(i, [st_ref[s] for s in range(n)], True)
    accs = [acc_ref[s] for s in range(n)]
    return [a[:dv] / a[dv:dv + 1] for a in accs]


def _mla_attn_kernel(q_ref, k_ref, vt_ref, o_ref, m_ref, acc_ref, st_ref):
    q = q_ref[0]
    n = q.shape[1] // LANES
    qs = [q[:, s * LANES:(s + 1) * LANES] for s in range(n)]
    key_of = lambda kb, s: kb[:, s * LANES:(s + 1) * LANES]
    val_of = lambda vt, s: vt[s // 2, (s % 2) * MLA_V:(s % 2 + 1) * MLA_V]
    outs = _attn_streams(pl.program_id(2), qs, key_of, val_of, k_ref, vt_ref, m_ref, acc_ref, st_ref)
    o_ref[0] = jnp.concatenate(outs, axis=0).T.astype(BF16)


def _diff_attn_kernel(q_ref, k_ref, vt_ref, lam_ref, sub_ref, o_ref, m_ref, acc_ref, *, lam_init):
    q = q_ref[0]
    nh = q.shape[1] // LANES
    lane = lax.broadcasted_iota(jnp.int32, (q.shape[0], LANES), 1)
    qs = []
    for h in range(nh):
        q_h = q[:, h * LANES:(h + 1) * LANES]
        zero = jnp.zeros_like(q_h)
        qs += [jnp.where(lane < DIFF_HD, q_h, zero), jnp.where(lane >= DIFF_HD, q_h, zero)]
    key_of = lambda kb, s: kb[:, (s // 2) * LANES:(s // 2 + 1) * LANES]
    val_of = lambda vt, s: vt[s // 2]
    outs = _attn_streams(pl.program_id(2), qs, key_of, val_of, k_ref, vt_ref, m_ref, acc_ref)
    lam = lam_ref[...]
    s1 = jnp.sum(lam[0:1] * lam[1:2], axis=-1, keepdims=True)
    s2 = jnp.sum(lam[2:3] * lam[3:4], axis=-1, keepdims=True)
    lam_full = jnp.exp(s1) - jnp.exp(s2) + lam_init
    for h in range(nh):
        o = (outs[2 * h] - lam_full * outs[2 * h + 1]).T
        o_ref[0, :, h * LANES:(h + 1) * LANES] = (_rms(o, sub_ref[...]) * (1.0 - lam_init)).astype(BF16)


ATTN_STREAMS = 4


def _attn_scratch(bq, dv):
    return [pltpu.VMEM((ATTN_STREAMS, 1, bq), F32), pltpu.VMEM((ATTN_STREAMS, dv + ONES_ROWS, bq), F32)]


def _vt_spec(nk, groups):
    return pl.BlockSpec((1, groups, nk, LANES, ATTN_BLOCK), lambda bi, h, i: (bi, h, 0, 0, 0))


def _mla_attn(qf, kf, vt):
    b, s, _ = qf.shape
    bq = ATTN_BLOCK
    w = ATTN_STREAMS * LANES
    return pl.pallas_call(
        _mla_attn_kernel,
        out_shape=jax.ShapeDtypeStruct((b, s, MLA_HEADS * MLA_V), BF16),
        grid=(b, MLA_HEADS // ATTN_STREAMS, s // bq),
        in_specs=[pl.BlockSpec((1, bq, w), lambda bi, h, i: (bi, i, h)),
                  pl.BlockSpec((1, s, w), lambda bi, h, i: (bi, 0, h)),
                  _vt_spec(s // bq, ATTN_STREAMS // 2)],
        out_specs=pl.BlockSpec((1, bq, ATTN_STREAMS * MLA_V), lambda bi, h, i: (bi, i, h)),
        scratch_shapes=_attn_scratch(bq, MLA_V) + [pltpu.VMEM((ATTN_STREAMS, ATTN_BLOCK, bq), F32)],
        compiler_params=_cparams(("parallel", "parallel", "arbitrary")),
        name="mla_attn",
    )(qf, kf, vt)


def _diff_attn(dq, dk, dvt, lam, subln, lam_init):
    b, s, _ = dq.shape
    bq = ATTN_BLOCK
    nh = ATTN_STREAMS // 2
    blk = lambda rows, im: pl.BlockSpec((1, rows, nh * LANES), im)
    return pl.pallas_call(
        functools.partial(_diff_attn_kernel, lam_init=lam_init),
        out_shape=jax.ShapeDtypeStruct((b, s, DIFF_HEADS * DIFF_V), BF16),
        grid=(b, DIFF_HEADS // nh, s // bq),
        in_specs=[blk(bq, lambda bi, h, i: (bi, i, h)), blk(s, lambda bi, h, i: (bi, 0, h)),
                  _vt_spec(s // bq, nh),
                  pl.BlockSpec(lam.shape, lambda bi, h, i: (0, 0)),
                  pl.BlockSpec(subln.shape, lambda bi, h, i: (0, 0))],
        out_specs=blk(bq, lambda bi, h, i: (bi, i, h)),
        scratch_shapes=_attn_scratch(bq, DIFF_V),
        compiler_params=_cparams(("parallel", "parallel", "arbitrary")),
        name="diff_attn",
    )(dq, dk, dvt, lam, subln)


def _mix_out_kernel(h_ref, a_ref, b_ref, w_ref, g_ref, rwh_ref, rwl_ref, rb_ref, tri_ref,
                    hout_ref, hf_ref, route_ref, idx_ref, cnt_ref):
    mixed = jnp.concatenate([a_ref[...], b_ref[...]], axis=-1)
    h = h_ref[...] + _dot(mixed, w_ref[...])
    hout_ref[...] = h
    hf = _rms(h, g_ref[...])
    hi = hf.astype(BF16)
    hf_ref[...] = hi
    lo = (hf - hi.astype(F32)).astype(BF16)
    z = _dot(hi, rwh_ref[...]) + _dot(lo, rwh_ref[...]) + _dot(hi, rwl_ref[...]) + rb_ref[...]
    lane_i = lax.broadcasted_iota(jnp.int32, z.shape, 1)
    lane = lane_i.astype(F32)
    big = float(LANES)
    is_g = lane_i < MOE_GROUPS
    zg = jnp.where(is_g, z, NEG)
    mg = jnp.max(zg, axis=-1, keepdims=True)
    g_idx = jnp.min(jnp.where(zg == mg, lane, big), axis=-1, keepdims=True)
    pg_top = 1.0 / jnp.sum(jnp.where(is_g, jnp.exp(zg - mg), 0.0), axis=-1, keepdims=True)
    grp_of_lane = jnp.right_shift(lane_i - MOE_GROUPS, int(math.log2(MOE_EPG))).astype(F32)
    in_grp = (lane_i >= MOE_GROUPS) & (lane_i < MOE_GROUPS + MOE_EXPERTS) & (grp_of_lane == g_idx)
    ze = jnp.where(in_grp, z, NEG)
    m1 = jnp.max(ze, axis=-1, keepdims=True)
    i1 = jnp.min(jnp.where(ze == m1, lane, big), axis=-1, keepdims=True)
    ze2 = jnp.where(lane == i1, NEG, ze)
    m2 = jnp.max(ze2, axis=-1, keepdims=True)
    i2 = jnp.min(jnp.where(ze2 == m2, lane, big), axis=-1, keepdims=True)
    r = jnp.exp(m2 - m1)
    w1 = 1.0 / (1.0 + r)
    w2 = r / (1.0 + r)
    e1 = i1 - float(MOE_GROUPS)
    e2 = i2 - float(MOE_GROUPS)
    @pl.when(pl.program_id(0) == 0)
    def _():
        cnt_ref[...] = jnp.zeros(cnt_ref.shape, F32)

    both = jnp.where((lane == i1) | (lane == i2), 1.0, 0.0)
    before = _dot(tri_ref[...], both.astype(BF16)) + cnt_ref[0:1, :]
    rank1 = jnp.sum(jnp.where(lane == i1, before, 0.0), axis=-1, keepdims=True)
    rank2 = jnp.sum(jnp.where(lane == i2, before, 0.0), axis=-1, keepdims=True)
    cnt_ref[...] = cnt_ref[...] + jnp.sum(both, axis=0, keepdims=True)
    vals = (e1, e2, pg_top * w1, pg_top * w2, rank1, rank2)
    out = jnp.zeros(z.shape, F32)
    for col, val in enumerate(vals):
        out = jnp.where(lane_i == col, val, out)
    route_ref[...] = out
    idx_ref[0] = out.T[0:8].astype(jnp.int32)


def _mix_out(h, a, b, w, g, rwh, rwl, rb, tri):
    n, d = h.shape
    tm = ROW_TILE
    row = lambda i: (i, 0)
    full = lambda i: (0, 0)
    wspec = lambda t: pl.BlockSpec(t.shape, full)
    return pl.pallas_call(
        _mix_out_kernel,
        out_shape=[jax.ShapeDtypeStruct((n, d), F32), jax.ShapeDtypeStruct((n, d), BF16),
                   jax.ShapeDtypeStruct((n, LANES), F32), jax.ShapeDtypeStruct((n // tm, 8, tm), jnp.int32),
                   jax.ShapeDtypeStruct((8, LANES), F32)],
        grid=(n // tm,),
        in_specs=[pl.BlockSpec((tm, d), row), pl.BlockSpec((tm, a.shape[1]), row), pl.BlockSpec((tm, b.shape[1]), row),
                  wspec(w), wspec(g), wspec(rwh), wspec(rwl), wspec(rb), wspec(tri)],
        out_specs=[pl.BlockSpec((tm, d), row), pl.BlockSpec((tm, d), row), pl.BlockSpec((tm, LANES), row),
                   pl.BlockSpec((1, 8, tm), lambda i: (i, 0, 0)), pl.BlockSpec((8, LANES), full)],
        compiler_params=_cparams(("arbitrary",)),
        name="mix_out",
    )(h, a, b, w, g, rwh, rwl, rb, tri)


def _expert_kernel(te_ref, nv_ref, x_ref, wg_ref, wu_ref, wd_ref, y_ref, wgb_ref, wub_ref, wdb_ref):
    t = pl.program_id(0)

    @pl.when((t == 0) | (te_ref[t] != te_ref[jnp.maximum(t - 1, 0)]))
    def _():
        wgb_ref[...] = wg_ref[0, 0].astype(BF16)
        wub_ref[...] = wu_ref[0, 0].astype(BF16)
        wdb_ref[...] = wd_ref[0, 0].astype(BF16)

    @pl.when(t < nv_ref[0])
    def _():
        x = x_ref[...]
        a = _dot(x, wgb_ref[...])
        up = _dot(x, wub_ref[...])
        act = (a * _sigmoid(a) * up).astype(BF16)
        y_ref[...] = _dot(act, wdb_ref[...]).astype(BF16)

    @pl.when(t >= nv_ref[0])
    def _():
        y_ref[...] = jnp.zeros(y_ref.shape, BF16)


def _experts(xs, tile_expert, n_valid, wg, wu, wd, layer):
    p, d = xs.shape
    tm = EXPERT_TILE
    ff = wg.shape[3]
    grid_spec = pltpu.PrefetchScalarGridSpec(
        num_scalar_prefetch=2,
        grid=(p // tm,),
        in_specs=[pl.BlockSpec((tm, d), lambda t, te, nv: (t, 0)),
                  pl.BlockSpec((1, 1, d, ff), lambda t, te, nv: (layer, te[t], 0, 0)),
                  pl.BlockSpec((1, 1, d, ff), lambda t, te, nv: (layer, te[t], 0, 0)),
                  pl.BlockSpec((1, 1, ff, d), lambda t, te, nv: (layer, te[t], 0, 0))],
        out_specs=pl.BlockSpec((tm, d), lambda t, te, nv: (t, 0)),
        scratch_shapes=[pltpu.VMEM((d, ff), BF16), pltpu.VMEM((d, ff), BF16), pltpu.VMEM((ff, d), BF16)],
    )
    return pl.pallas_call(
        _expert_kernel,
        out_shape=jax.ShapeDtypeStruct((p, d), BF16),
        grid_spec=grid_spec,
        compiler_params=_cparams(("arbitrary",)),
        name="experts",
    )(tile_expert, n_valid, xs, wg, wu, wd)


def _add_experts(h_ref, y1_ref, y2_ref, route_ref):
    route = route_ref[...]
    return h_ref[...] + route[:, 2:3] * y1_ref[...].astype(F32) + route[:, 3:4] * y2_ref[...].astype(F32)


def _final_kernel(h_ref, y1_ref, y2_ref, route_ref, g_ref, o_ref):
    o_ref[...] = _rms(_add_experts(h_ref, y1_ref, y2_ref, route_ref), g_ref[...])


def _final(h, y, route, g):
    n, d = h.shape
    tm = ROW_TILE
    row = lambda i: (i, 0)
    second = lambda i: (i + n // tm, 0)
    return pl.pallas_call(
        _final_kernel,
        out_shape=jax.ShapeDtypeStruct((n, d), F32),
        grid=(n // tm,),
        in_specs=[pl.BlockSpec((tm, d), row), pl.BlockSpec((tm, d), row), pl.BlockSpec((tm, d), second),
                  pl.BlockSpec((tm, LANES), row), pl.BlockSpec(g.shape, lambda i: (0, 0))],
        out_specs=pl.BlockSpec((tm, d), row),
        compiler_params=_cparams(("parallel",)),
        name="moe_final",
    )(h, y, y, route, g)


def _moe(hf, idx, counts, wg, wu, wd, layer):
    n, d = hf.shape
    tm = EXPERT_TILE
    ids = jnp.arange(MOE_EXPERTS, dtype=jnp.int32)
    counts = counts[0, MOE_GROUPS:MOE_GROUPS + MOE_EXPERTS].astype(jnp.int32)
    padded = ((counts + tm - 1) // tm) * tm
    ends = jnp.cumsum(padded)
    starts = ends - padded
    start_of = lambda e: jnp.sum(jnp.where(e[:, None] == ids[None, :], starts[None, :], 0), axis=-1)
    e1, e2 = idx[:, 0, :].reshape(n), idx[:, 1, :].reshape(n)
    dest1 = start_of(e1) + idx[:, 4, :].reshape(n)
    dest2 = start_of(e2) + idx[:, 5, :].reshape(n)
    n_rows = 2 * n + MOE_EXPERTS * tm
    tok = jnp.arange(n, dtype=jnp.int32)
    tok_for_row = (jnp.arange(n_rows, dtype=jnp.int32) % n).at[jnp.concatenate([dest1, dest2])].set(
        jnp.concatenate([tok, tok]), mode="promise_in_bounds", unique_indices=True)
    take = functools.partial(jnp.take, axis=0, mode="clip")
    tile_start = jnp.arange(n_rows // tm, dtype=jnp.int32) * tm
    tile_expert = jnp.minimum(jnp.sum((ends[None, :] <= tile_start[:, None]).astype(jnp.int32), axis=1),
                              MOE_EXPERTS - 1)
    n_valid = (ends[-1] // tm).astype(jnp.int32).reshape(1)
    xs = take(hf, tok_for_row)
    ys = _experts(xs, tile_expert, n_valid, wg, wu, wd, layer)
    return take(ys, jnp.concatenate([dest1, dest2]))


def _od_in_kernel(h_ref, y1_ref, y2_ref, route_ref, g_ref, w_ref, hout_ref, ur_ref, uh_ref):
    h = _add_experts(h_ref, y1_ref, y2_ref, route_ref)
    hout_ref[...] = h
    hn = _rms(h, g_ref[...]).astype(BF16)
    u = _dot(hn, w_ref[...])
    ur_ref[...] = u[:, :RWKV_IN].astype(BF16)
    uh_ref[...] = u[:, RWKV_IN:].astype(BF16)


def _od_in(h, y, route, g, w):
    n, d = h.shape
    tm = ROW_TILE
    row = lambda i: (i, 0)
    rspec = lambda width: pl.BlockSpec((tm, width), row)
    return pl.pallas_call(
        _od_in_kernel,
        out_shape=[jax.ShapeDtypeStruct((n, d), F32), jax.ShapeDtypeStruct((n, RWKV_IN), BF16),
                   jax.ShapeDtypeStruct((n, HGRN_IN), BF16)],
        grid=(n // tm,),
        in_specs=[rspec(d), rspec(d), pl.BlockSpec((tm, d), lambda i: (i + n // tm, 0)), rspec(LANES),
                  pl.BlockSpec(g.shape, lambda i: (0, 0)), pl.BlockSpec(w.shape, lambda i: (0, 0))],
        out_specs=[rspec(d), rspec(RWKV_IN), rspec(HGRN_IN)],
        compiler_params=_cparams(("parallel",)),
        name="od_in",
    )(h, y, y, route, g, w)


def _rwkv_kernel(u_ref, mu_ref, w0_ref, w2_ref, a0_ref, a2_ref, g2_ref, kk_ref, ka_ref, rk_ref,
                 lnw_ref, lnb_ref, tri_ref, seg_ref, o_ref, prev_ref, ht_ref):
    c = pl.program_id(1)

    @pl.when(c == 0)
    def _():
        prev_ref[...] = jnp.zeros(prev_ref.shape, F32)
        ht_ref[...] = jnp.zeros(ht_ref.shape, F32)

    nb = u_ref.shape[0]
    u = u_ref[...].reshape(nb * CHUNK, RWKV_IN).astype(F32)
    rows = lax.broadcasted_iota(jnp.int32, u.shape, 0)
    u_prev = pltpu.roll(u, 1, 0)
    for bi in range(nb):
        u_prev = jnp.where(rows == bi * CHUNK, prev_ref[bi], u_prev)
        prev_ref[bi] = u[(bi + 1) * CHUNK - 1:(bi + 1) * CHUNK, :]
    us = u + (u_prev - u) * mu_ref[...]
    w = RWKV_W
    r = us[:, 0:w]
    k = us[:, w:2 * w]
    v = us[:, 2 * w:3 * w]
    x12 = us[:, 3 * w:3 * w + LANES]
    xg = us[:, 3 * w + LANES:]
    seg_pair = seg_ref[...]

    def seg(x, split=False):
        mm = _split_dot if split else (lambda v, m: _dot(v.astype(BF16), m))
        return jnp.concatenate([mm(x[:, p * LANES:(p + 1) * LANES], seg_pair)
                                for p in range(RWKV_HEADS // 2)], axis=1)

    tri = tri_ref[...]

    wl = w0_ref[...] + _dot(jnp.tanh(x12).astype(BF16), w2_ref[...])
    nwl = -wl
    softplus = jnp.maximum(nwl, 0.0) + jnp.log1p(jnp.exp(-jnp.abs(nwl)))
    lw = -jnp.exp(-softplus - 0.5)
    a = _sigmoid(a0_ref[...] + _dot(x12.astype(BF16), a2_ref[...]))
    g = _dot(_sigmoid(xg).astype(BF16), g2_ref[...])
    kk = k * kk_ref[...]
    kk = kk * lax.rsqrt(jnp.maximum(seg(kk * kk, split=True), 1e-24))
    k2 = k * (1.0 + (a - 1.0) * ka_ref[...])
    a_in = -kk
    b_in = kk * a

    lg = _split_dot(lw, tri, left=True)
    rbs = [slice(bi * CHUNK, (bi + 1) * CHUNK) for bi in range(nb)]
    lg_c = [lg[(bi + 1) * CHUNK - 1:(bi + 1) * CHUNK, :] for bi in range(nb)]
    e_neg = jnp.exp(-lg)
    e_rel = jnp.concatenate([jnp.exp(lg_c[bi] - lg[rbs[bi]]) for bi in range(nb)], axis=0)
    g_c = [jnp.exp(x) for x in lg_c]
    at = a_in * jnp.exp(lg - lw)
    rt = r * jnp.exp(lg)
    kt = k2 * e_neg
    bt = b_in * e_neg
    kh = k2 * e_rel
    bh = b_in * e_rel

    c2 = 2 * CHUNK
    ri = lax.broadcasted_iota(jnp.int32, (c2, c2), 0)
    ci = lax.broadcasted_iota(jnp.int32, (c2, c2), 1)
    sh = int(math.log2(CHUNK))
    same = jnp.right_shift(ri, sh) == jnp.right_shift(ci, sh)
    strict = same & (ri > ci)
    incl = same & (ri >= ci)
    eye = jnp.where(ri == ci, 1.0, 0.0).astype(F32)
    lane = lax.broadcasted_iota(jnp.int32, (CHUNK, LANES), 1)
    lo_half = lane < RWKV_HD

    def bd(x):
        return jnp.concatenate([jnp.where(lo_half, x, 0.0), jnp.where(lo_half, 0.0, x)], axis=0)

    npair = RWKV_HEADS // 2
    pairs = range(nb * npair)
    sls = [(rbs[c // npair], slice((c % npair) * LANES, (c % npair + 1) * LANES)) for c in pairs]
    bdb = lambda x: [bd(x[sl]).astype(BF16) for sl in sls]
    at_b = [bd(at[sl]) for sl in sls]
    rt_b = [bd(rt[sl]) for sl in sls]
    kt_b, bt_b, kh_b, bh_b, v_b = bdb(kt), bdb(bt), bdb(kh), bdb(bh), bdb(v)
    hts = [ht_ref[p] for p in pairs]
    htb = [t.astype(BF16) for t in hts]
    sc = [_dot_nt(jnp.concatenate([at_b[p], rt_b[p]], axis=0).astype(BF16),
                  jnp.concatenate([kt_b[p], bt_b[p]], axis=0)) for p in pairs]
    a_ak = [jnp.where(strict, s[:c2, :c2], 0.0).astype(BF16) for s in sc]
    a_ab = [jnp.where(strict, s[:c2, c2:], 0.0) for s in sc]
    a_rk = [jnp.where(incl, s[c2:, :c2], 0.0).astype(BF16) for s in sc]
    a_rb = [jnp.where(incl, s[c2:, c2:], 0.0).astype(BF16) for s in sc]
    x2 = [_dot(a_ak[p], v_b[p]) for p in pairs]
    e1 = [_dot(a_rk[p], v_b[p]) for p in pairs]
    d1 = [_dot_tn(v_b[p], kh_b[p]) for p in pairs]
    pw = a_ab
    t_inv = [eye + a for a in a_ab]
    for _ in range(int(math.log2(CHUNK)) - 1):
        pwb = [x.astype(BF16) for x in pw]
        pw = [_dot(x, x) for x in pwb]
        t_inv = [t_inv[p] + _dot(t_inv[p].astype(BF16), pw[p].astype(BF16)) for p in pairs]
    txb = [_dot(t_inv[p].astype(BF16), jnp.concatenate([at_b[p], x2[p]], axis=1).astype(BF16)).astype(BF16)
           for p in pairs]
    qe = [_dot(a_rb[p], txb[p]) for p in pairs]
    gd = [_dot_tn(txb[p], bh_b[p]) for p in pairs]
    y_b = [_dot_nt((rt_b[p] + qe[p][:, :LANES]).astype(BF16), htb[p]) + e1[p] + qe[p][:, LANES:] for p in pairs]
    for p in pairs:
        upd = hts[p] * g_c[p // npair][:, sls[p][1]] + _dot(htb[p], gd[p][:LANES].astype(BF16)) + d1[p] + gd[p][LANES:]
        ht_ref[p] = jnp.where(same, upd, 0.0)
    y_p = [t[:CHUNK] + t[CHUNK:] for t in y_b]
    y = jnp.concatenate([jnp.concatenate(y_p[bi * npair:(bi + 1) * npair], axis=1) for bi in range(nb)],
                        axis=0)

    inv_n = 1.0 / RWKV_HD
    mean = seg(y) * inv_n
    dlt = y - mean
    var = seg(dlt * dlt) * inv_n
    yn = dlt * lax.rsqrt(var + RWKV_GN_EPS) * lnw_ref[...] + lnb_ref[...]
    bonus = seg(r * k2 * rk_ref[...]) * v
    o_ref[...] = ((yn + bonus) * g).astype(BF16).reshape(nb, CHUNK, RWKV_W)


def _rwkv(u, mu, w0, w2p, a0, a2p, g2, k_k, k_a, r_k, ln_w, ln_b, tri, seg):
    b, s, _ = u.shape
    nb = tri.shape[0] // CHUNK
    full = lambda bi, ci: (0, 0)
    wspec = lambda t: pl.BlockSpec(t.shape, full)
    params = [mu, w0, w2p, a0, a2p, g2, k_k, k_a, r_k, ln_w, ln_b, tri, seg]
    return pl.pallas_call(
        _rwkv_kernel,
        out_shape=jax.ShapeDtypeStruct((b, s, RWKV_W), BF16),
        grid=(b // nb, s // CHUNK),
        in_specs=[pl.BlockSpec((nb, CHUNK, RWKV_IN), lambda bi, ci: (bi, ci, 0))] + [wspec(t) for t in params],
        out_specs=pl.BlockSpec((nb, CHUNK, RWKV_W), lambda bi, ci: (bi, ci, 0)),
        scratch_shapes=[pltpu.VMEM((nb, 1, RWKV_IN), F32), pltpu.VMEM((nb * RWKV_HEADS // 2, LANES, LANES), F32)],
        compiler_params=_cparams(("parallel", "arbitrary")),
        name="rwkv7",
    )(u, *params)


def _hgrn_kernel(u_ref, lb_ref, on_ref, tri_ref, ones_ref, sel_ref, o_ref, st_ref):
    c = pl.program_id(1)

    @pl.when(c == 0)
    def _():
        st_ref[...] = jnp.zeros(st_ref.shape, F32)

    nb = u_ref.shape[0]
    u = u_ref[...].reshape(nb * CHUNK, HGRN_IN).astype(F32)
    lbp = lb_ref[...]
    mx = jnp.maximum(lbp[0:1], lbp[1:2])
    e0 = jnp.exp(lbp[0:1] - mx)
    e1 = jnp.exp(lbp[1:2] - mx)
    p0 = e0 / (e0 + e1)
    p1 = e1 / (e0 + e1)
    lb = (p0 + p1) - p0
    tri = tri_ref[...]
    ones = ones_ref[...]
    sel = sel_ref[...]
    d = HGRN_D
    w = HGRN_W
    heads = range(HGRN_HEADS)
    sls = [slice(h * d, (h + 1) * d) for h in heads]
    q, z, iv, gt = u[:, :w], u[:, w:2 * w], u[:, 2 * w:3 * w], u[:, 3 * w:]
    qs = q * _sigmoid(q)
    log_sig = jnp.minimum(z, 0.0) - jnp.log1p(jnp.exp(-jnp.abs(z)))
    x1 = jnp.log(lb)
    x2 = jnp.log1p(-lb) + log_sig
    log_f = jnp.maximum(x1, x2) + jnp.log1p(jnp.exp(-jnp.abs(x1 - x2)))
    key = (1.0 - lb) * _sigmoid(-z)
    bc = _split_dot(log_f, tri, left=True)
    seqs = range(nb)
    rbs = [slice(i * CHUNK, (i + 1) * CHUNK) for i in seqs]
    b_l = [bc[(i + 1) * CHUNK - 1:(i + 1) * CHUNK] for i in seqs]
    ivb = iv.astype(BF16)
    qe = (qs * jnp.exp(bc)).astype(BF16)
    kl = (key * jnp.concatenate([jnp.exp(b_l[i] - bc[rbs[i]]) for i in seqs], axis=0)).astype(BF16)
    nh = HGRN_HEADS
    sts = [st_ref[c] for c in range(nb * nh)]
    o_inter = [_dot_nt(qe[rbs[c // nh], sls[c % nh]], sts[c].astype(BF16)) for c in range(nb * nh)]
    for c in range(nb * nh):
        i, sl = c // nh, sls[c % nh]
        st_ref[c] = sts[c] * jnp.exp(b_l[i])[:, sl] + _dot_tn(ivb[rbs[i], sl], kl[rbs[i], sl])

    nblk = CHUNK // SUB
    hrow = lax.broadcasted_iota(jnp.int32, (HALF, w), 0)
    p_all, v_tile, sc_off, sc_half = [], [], [], []
    for blk in range(nb * nblk):
        base = (blk // nblk) * CHUNK
        r0 = base + (blk % nblk) * SUB
        b_i, q_i, k_i = bc[r0:r0 + SUB], qs[r0:r0 + SUB], key[r0:r0 + SUB]
        rows = []
        for t in range(SUB):
            lo = (t // HALF) * HALF
            rows.append(q_i[t:t + 1] * k_i[lo:lo + HALF]
                        * jnp.where(hrow <= t - lo, jnp.exp(b_i[t:t + 1] - b_i[lo:lo + HALF]), 0.0))
        p_all.append(jnp.concatenate(rows, axis=0).astype(BF16))
        v_tile.append(jnp.concatenate([iv[r0:r0 + HALF]] * HALF + [iv[r0 + HALF:r0 + SUB]] * HALF, axis=0))
        b_h = b_i[HALF - 1:HALF]
        qh = (q_i[HALF:] * jnp.exp(b_i[HALF:] - b_h)).astype(BF16)
        kh = (k_i[:HALF] * jnp.exp(b_h - b_i[:HALF])).astype(BF16)
        sc_half.append([_dot_nt(qh[:, sl], kh[:, sl]).astype(BF16) for sl in sls])
        if r0 > base:
            b_m = bc[r0 - 1:r0]
            qp = (q_i * jnp.exp(b_i - b_m)).astype(BF16)
            kp = (key[base:r0] * jnp.exp(b_m - bc[base:r0])).astype(BF16)
            sc_off.append([_dot_nt(qp[:, sl], kp[:, sl]).astype(BF16) for sl in sls])
        else:
            sc_off.append(None)
    rs = [jnp.concatenate([_dot(p[:, sl], ones) for sl in sls], axis=1) for p in p_all]
    rv = [(rs[blk] * v_tile[blk]).astype(BF16) for blk in range(nb * nblk)]
    for c in range(nb * nh):
        i, sl = c // nh, sls[c % nh]
        parts = []
        for blk in range(i * nblk, (i + 1) * nblk):
            r0 = i * CHUNK + (blk % nblk) * SUB
            lower = _dot(sc_half[blk][c % nh], iv[r0:r0 + HALF, sl].astype(BF16))
            o_i = _dot(sel, rv[blk][:, sl]) + jnp.concatenate([jnp.zeros_like(lower), lower], axis=0)
            if sc_off[blk] is not None:
                o_i = o_i + _dot(sc_off[blk][c % nh], ivb[i * CHUNK:i * CHUNK + (blk % nblk) * SUB, sl])
            parts.append(o_i)
        o = o_inter[c] + jnp.concatenate(parts, axis=0)
        on = _rms(o, on_ref[:, sl])
        g_h = gt[rbs[i], sl]
        o_ref[i, :, sl] = (on * (g_h * _sigmoid(g_h))).astype(BF16)


def _hgrn(u, lbp, o_norm, tri, ones, sel):
    b, s, _ = u.shape
    nb = tri.shape[0] // CHUNK
    full = lambda bi, ci: (0, 0)
    wspec = lambda t: pl.BlockSpec(t.shape, full)
    return pl.pallas_call(
        _hgrn_kernel,
        out_shape=jax.ShapeDtypeStruct((b, s, HGRN_W), BF16),
        grid=(b // nb, s // CHUNK),
        in_specs=[pl.BlockSpec((nb, CHUNK, HGRN_IN), lambda bi, ci: (bi, ci, 0)), wspec(lbp), wspec(o_norm),
                  wspec(tri), wspec(ones), wspec(sel)],
        out_specs=pl.BlockSpec((nb, CHUNK, HGRN_W), lambda bi, ci: (bi, ci, 0)),
        scratch_shapes=[pltpu.VMEM((nb * HGRN_HEADS, HGRN_D, HGRN_D), F32)],
        compiler_params=_cparams(("parallel", "arbitrary")),
        name="hgrn2",
    )(u, lbp, o_norm, tri, ones, sel)


def _rope_tables(seq):
    pos = jnp.arange(seq, dtype=F32)[:, None]

    def cs(half):
        inv = jnp.power(ROPE_THETA, -jnp.arange(half, dtype=F32) / half)
        ang = pos * inv[None, :]
        return jnp.cos(ang), jnp.sin(ang)

    cd, sd = cs(DIFF_HD // 2)
    cos_d = jnp.tile(jnp.concatenate([cd, cd], axis=1), (1, LANES // DIFF_HD))
    sin_d = jnp.tile(jnp.concatenate([-sd, sd], axis=1), (1, LANES // DIFF_HD))
    cm, sm = cs(MLA_ROPE // 2)
    one = jnp.ones((seq, MLA_NOPE), F32)
    tail = LANES - MLA_QK
    cos_m = jnp.concatenate([one, cm, cm, jnp.ones((seq, tail), F32)], axis=1)
    sin_m = jnp.concatenate([0 * one, -sm, sm, jnp.zeros((seq, tail), F32)], axis=1)
    return cos_d, sin_d, cos_m, sin_m


def _router_weights(w_group, b_group, w_expert, b_expert):
    d = w_group.shape[0]
    pad = LANES - MOE_GROUPS - MOE_EXPERTS
    w = jnp.concatenate([w_group, w_expert, jnp.zeros((d, pad), F32)], axis=1)
    bias = jnp.concatenate([b_group, b_expert, jnp.zeros((pad,), F32)])[None, :]
    hi = w.astype(BF16)
    lo = (w - hi.astype(F32)).astype(BF16)
    return hi, lo, bias


def kernel(x, norm_mix, norm_ffn, norm_final, ev_w_in, ev_w_out, mla_q_norm, mla_w_q_up, mla_kv_norm, mla_w_kv_up, diff_lambda, diff_subln, od_w_in, od_w_out, rwkv_mu, rwkv_w0, rwkv_w2, rwkv_a0, rwkv_a2, rwkv_g2, rwkv_k_k, rwkv_k_a, rwkv_r_k, rwkv_ln_w, rwkv_ln_b, hgrn_lb, hgrn_o_norm, moe_w_group, moe_b_group, moe_w_expert, moe_b_expert, moe_w_gate, moe_w_up, moe_w_down):
    b, s, d = x.shape
    n = b * s
    assert norm_mix.shape[0] == 2 and hgrn_lb.shape[0] == 2
    assert s % ATTN_BLOCK == 0 and s % ROW_TILE == 0 and ATTN_BLOCK % CHUNK == 0
    row2 = lambda t: t.reshape(1, -1)
    h = x.reshape(n, d)

    w_in = ev_w_in[0]
    o1, o2, o3 = MLA_LORA, 2 * MLA_LORA, 2 * MLA_LORA + MLA_ROPE
    kr_pad = jnp.zeros((d, LANES), F32).at[:, MLA_NOPE:MLA_QK].set(w_in[:, o2:o3])
    w0 = jnp.concatenate([w_in[:, :o2], kr_pad, w_in[:, o3:]], axis=1).astype(BF16)
    wq = mla_w_q_up[0].reshape(MLA_LORA, MLA_HEADS, MLA_QK)
    wq = jnp.pad(wq, ((0, 0), (0, 0), (0, LANES - MLA_QK))).reshape(MLA_LORA, MLA_HEADS * LANES).astype(BF16)
    wkv = mla_w_kv_up[0].reshape(MLA_LORA, MLA_HEADS, MLA_NOPE + MLA_V)
    wk = jnp.pad(wkv[:, :, :MLA_NOPE], ((0, 0), (0, 0), (0, LANES - MLA_NOPE)))
    wk = wk.reshape(MLA_LORA, MLA_HEADS * LANES).astype(BF16)
    wv = wkv[:, :, MLA_NOPE:].reshape(MLA_LORA, MLA_HEADS * MLA_V).astype(BF16)
    cos_d, sin_d, cos_m, sin_m = _rope_tables(s)
    qf, kf, vm, dq, dk, dv = _ev_in(h, row2(norm_mix[0]), w0, row2(mla_q_norm[0]), row2(mla_kv_norm[0]),
                                    wq, wk, wv, cos_d, sin_d, cos_m, sin_m, s)
    r3 = lambda t: t.reshape(b, s, t.shape[-1])
    o_mla = _mla_attn(r3(qf), r3(kf), vm)
    lam_init = 0.8 - 0.6 * math.exp(-0.3 * 0)
    o_diff = _diff_attn(r3(dq), r3(dk), dv, diff_lambda[0], row2(diff_subln[0]), lam_init)
    rwh, rwl, rb = _router_weights(moe_w_group[0], moe_b_group[0], moe_w_expert[0], moe_b_expert[0])
    ti = jnp.arange(ROW_TILE)
    tri_rows = (ti[None, :] < ti[:, None]).astype(BF16)
    h, hf, route, ridx, counts = _mix_out(h, o_mla.reshape(n, -1), o_diff.reshape(n, -1), ev_w_out[0].astype(BF16),
                                    row2(norm_ffn[0]), rwh, rwl, rb, tri_rows)
    y = _moe(hf, ridx, counts, moe_w_gate, moe_w_up, moe_w_down, 0)

    h, ur, uh = _od_in(h, y, route, row2(norm_mix[1]), od_w_in[0].astype(BF16))
    zpad = jnp.zeros((RWKV_DECAY_LORA, RWKV_W), F32)
    w2p = jnp.concatenate([rwkv_w2[0], zpad], axis=0).astype(BF16)
    a2p = jnp.concatenate([zpad, rwkv_a2[0]], axis=0).astype(BF16)
    ci = jnp.arange(CHUNK)
    tri = (ci[None, :] <= ci[:, None]).astype(BF16)
    li = jnp.arange(LANES) // RWKV_HD
    seg = (li[:, None] == li[None, :]).astype(BF16)
    o_c = _rwkv(ur.reshape(b, s, RWKV_IN), row2(rwkv_mu[0]), row2(rwkv_w0[0]), w2p, row2(rwkv_a0[0]), a2p,
                rwkv_g2[0].astype(BF16), row2(rwkv_k_k[0]), row2(rwkv_k_a[0]), row2(rwkv_r_k[0]),
                row2(rwkv_ln_w[0]), row2(rwkv_ln_b[0]),
                jnp.kron(jnp.eye(math.gcd(b, RWKV_SEQS), dtype=BF16), tri), seg)
    pt = jnp.arange(SUB * HALF) // HALF
    ps = jnp.arange(SUB * HALF) % HALF
    sel = ((pt[None, :] == jnp.arange(SUB)[:, None]) & (ps <= pt % HALF)[None, :]).astype(BF16)
    o_d = _hgrn(uh.reshape(b, s, HGRN_IN), hgrn_lb, row2(hgrn_o_norm[0]),
                jnp.kron(jnp.eye(math.gcd(b, HGRN_SEQS), dtype=BF16), tri), jnp.ones((LANES, LANES), BF16), sel)
    rwh, rwl, rb = _router_weights(moe_w_group[1], moe_b_group[1], moe_w_expert[1], moe_b_expert[1])
    h, hf, route, ridx, counts = _mix_out(h, o_c.reshape(n, -1), o_d.reshape(n, -1), od_w_out[0].astype(BF16),
                                    row2(norm_ffn[1]), rwh, rwl, rb, tri_rows)
    y = _moe(hf, ridx, counts, moe_w_gate, moe_w_up, moe_w_down, 1)
    out = _final(h, y, route, row2(norm_final))
    return out.reshape(b, s, d)
```

```python
import functools
import math

import jax
import jax.numpy as jnp
from jax import lax
from jax.experimental import pallas as pl
from jax.experimental.pallas import tpu as pltpu

F32 = jnp.float32
BF16 = jnp.bfloat16

CHUNK = 64
ROPE_THETA = 10000.0
NORM_EPS = 1e-6
MLA_HEADS = 8
MLA_LORA = 256
MLA_NOPE = 64
MLA_ROPE = 32
MLA_V = 64
MLA_QK = MLA_NOPE + MLA_ROPE
DIFF_HEADS = 4
DIFF_HD = 64
DIFF_V = 2 * DIFF_HD
DIFF_W = DIFF_HEADS * 2 * DIFF_HD
RWKV_HEADS = 8
RWKV_HD = 64
RWKV_W = RWKV_HEADS * RWKV_HD
RWKV_DECAY_LORA = 64
RWKV_A_LORA = 64
RWKV_GATE_LORA = 128
RWKV_IN = 3 * RWKV_W + RWKV_DECAY_LORA + RWKV_A_LORA + RWKV_GATE_LORA
RWKV_GN_EPS = 64e-5
HGRN_HEADS = 4
HGRN_D = 128
HGRN_W = HGRN_HEADS * HGRN_D
HGRN_IN = 4 * HGRN_W
MOE_GROUPS = 4
MOE_EPG = 8
MOE_EXPERTS = MOE_GROUPS * MOE_EPG

LANES = 128
ROW_TILE = 512
ATTN_BLOCK = 512
EXPERT_TILE = 512
SUB = 16
HALF = 8
RWKV_SEQS = 4
HGRN_SEQS = 4
V7X_VMEM_BYTES = 64 * 1024 * 1024
VMEM_LIMIT = V7X_VMEM_BYTES * 3 // 4
NEG = -1e30
LOG2E = math.log2(math.e)


def _cparams(sem):
    return pltpu.CompilerParams(dimension_semantics=sem, vmem_limit_bytes=VMEM_LIMIT)


def _dot(a, b):
    return jnp.dot(a, b, preferred_element_type=F32)


def _dot_nt(a, b):
    return lax.dot_general(a, b, (((1,), (1,)), ((), ())), preferred_element_type=F32)


def _dot_tn(a, b):
    return lax.dot_general(a, b, (((0,), (0,)), ((), ())), preferred_element_type=F32)


def _rms(x, g):
    return x * lax.rsqrt(jnp.mean(x * x, axis=-1, keepdims=True) + NORM_EPS) * g


def _sigmoid(x):
    return 1.0 / (1.0 + jnp.exp(-x))


def _split_dot(x, w_bf16, left=False):
    hi = x.astype(BF16)
    lo = (x - hi.astype(F32)).astype(BF16)
    if left:
        return _dot(w_bf16, hi) + _dot(w_bf16, lo)
    return _dot(hi, w_bf16) + _dot(lo, w_bf16)


def _tile_lanes(t, n):
    return jnp.concatenate([t] * n, axis=-1) if n > 1 else t


def _rope_rot(x, half):
    w = x.shape[-1]
    lane = lax.broadcasted_iota(jnp.int32, x.shape, x.ndim - 1)
    first = (lane & (2 * half - 1)) < half
    return jnp.where(first, pltpu.roll(x, w - half, x.ndim - 1), pltpu.roll(x, half, x.ndim - 1))


def _ev_in_kernel(x_ref, g_ref, w0_ref, qg_ref, kvg_ref, wq_ref, wk_ref, wv_ref,
                  cd_ref, sd_ref, cm_ref, sm_ref,
                  qf_ref, kf_ref, vm_ref, dq_ref, dk_ref, dv_ref):
    hn = _rms(x_ref[...], g_ref[...]).astype(BF16)
    u = _dot(hn, w0_ref[...])
    o1, o2 = 2 * MLA_LORA, 2 * MLA_LORA + LANES
    c_q = u[:, :MLA_LORA]
    c_kv = u[:, MLA_LORA:o1]
    kr = u[:, o1:o2]
    dq = u[:, o2:o2 + DIFF_W]
    dk = u[:, o2 + DIFF_W:o2 + 2 * DIFF_W]
    dv = u[:, o2 + 2 * DIFF_W:]
    cqn = _rms(c_q, qg_ref[...]).astype(BF16)
    ckn = _rms(c_kv, kvg_ref[...]).astype(BF16)
    q = _dot(cqn, wq_ref[...])
    k = _dot(ckn, wk_ref[...])
    v = _dot(ckn, wv_ref[...])
    cm = cm_ref[...]
    sm = sm_ref[...]
    m_half = MLA_ROPE // 2
    lane = lax.broadcasted_iota(jnp.int32, q.shape, 1) & (LANES - 1)
    first = (lane >= MLA_NOPE) & (lane < MLA_NOPE + m_half)
    rot_q = jnp.where(first, pltpu.roll(q, q.shape[1] - m_half, 1), pltpu.roll(q, m_half, 1))
    q = (q * _tile_lanes(cm, MLA_HEADS) + rot_q * _tile_lanes(sm, MLA_HEADS)) * (MLA_QK ** -0.5 * LOG2E)
    lane1 = lax.broadcasted_iota(jnp.int32, kr.shape, 1)
    first1 = (lane1 >= MLA_NOPE) & (lane1 < MLA_NOPE + m_half)
    rot_k = jnp.where(first1, pltpu.roll(kr, LANES - m_half, 1), pltpu.roll(kr, m_half, 1))
    kr = kr * cm + rot_k * sm
    k = k + _tile_lanes(kr, MLA_HEADS)
    qf_ref[...] = q.astype(BF16)
    kf_ref[...] = k.astype(BF16)
    vm_ref[0, :, 0] = v.T.reshape(vm_ref.shape[1], LANES, v.shape[0]).astype(BF16)
    cd = _tile_lanes(cd_ref[...], DIFF_W // LANES)
    sd = _tile_lanes(sd_ref[...], DIFF_W // LANES)
    dq = (dq * cd + _rope_rot(dq, DIFF_HD // 2) * sd) * (DIFF_HD ** -0.5 * LOG2E)
    dk = dk * cd + _rope_rot(dk, DIFF_HD // 2) * sd
    dq_ref[...] = dq.astype(BF16)
    dk_ref[...] = dk.astype(BF16)
    dv_ref[0, :, 0] = dv.T.reshape(dv_ref.shape[1], LANES, dv.shape[0]).astype(BF16)


def _ev_in(x2, g, w0, qg, kvg, wq, wk, wv, cd, sd, cm, sm, seq):
    n, d = x2.shape
    tm = ATTN_BLOCK
    nt = seq // tm
    row = lambda i: (i, 0)
    full = lambda i: (0, 0)
    tab = lambda i: (i % nt, 0)
    wspec = lambda a: pl.BlockSpec(a.shape, full)
    vt_shape = lambda w: jax.ShapeDtypeStruct((n // seq, w // LANES, nt, LANES, tm), BF16)
    vt_spec = lambda w: pl.BlockSpec((1, w // LANES, 1, LANES, tm), lambda i: (i // nt, 0, i % nt, 0, 0))
    rows = lambda w: jax.ShapeDtypeStruct((n, w), BF16)
    rspec = lambda w: pl.BlockSpec((tm, w), row)
    wide = MLA_HEADS * LANES
    return pl.pallas_call(
        _ev_in_kernel,
        out_shape=[rows(wide), rows(wide), vt_shape(MLA_HEADS * MLA_V), rows(DIFF_W), rows(DIFF_W), vt_shape(DIFF_W)],
        grid=(n // tm,),
        in_specs=[pl.BlockSpec((tm, d), row), wspec(g), wspec(w0), wspec(qg), wspec(kvg), wspec(wq), wspec(wk),
                  wspec(wv)] + [pl.BlockSpec((tm, LANES), tab)] * 4,
        out_specs=[rspec(wide), rspec(wide), vt_spec(MLA_HEADS * MLA_V), rspec(DIFF_W), rspec(DIFF_W),
                   vt_spec(DIFF_W)],
        compiler_params=_cparams(("parallel",)),
        name="ev_in",
    )(x2, g, w0, qg, kvg, wq, wk, wv, cd, sd, cm, sm)


ONES_ROWS = 16


def _chunk_mask_t(shape):
    sh = int(math.log2(CHUNK))
    key_chunk = jnp.right_shift(lax.broadcasted_iota(jnp.int32, shape, 0), sh)
    q_chunk = jnp.right_shift(lax.broadcasted_iota(jnp.int32, shape, 1), sh)
    return key_chunk <= q_chunk


def _attn_streams(i, qs, key_of, val_of, k_ref, vt_ref, m_ref, acc_ref, st_ref=None):
    bk = ATTN_BLOCK
    n = len(qs)
    m_ref[...] = jnp.full(m_ref.shape, NEG, F32)
    acc_ref[...] = jnp.zeros(acc_ref.shape, F32)
    dv = acc_ref.shape[1] - ONES_ROWS
    ones = jnp.ones((ONES_ROWS, bk), BF16)

    def qk(j):
        kb = k_ref[0, pl.ds(pl.multiple_of(j * bk, bk), bk), :]
        return [_dot_nt(key_of(kb, s), qs[s]) for s in range(n)]

    def update(j, st, masked):
        vt = vt_ref[0, :, j]
        if masked:
            mask = _chunk_mask_t(st[0].shape)
            st = [jnp.where(mask, x, NEG) for x in st]
        m_prev = [m_ref[s] for s in range(n)]
        m_new = [jnp.maximum(m_prev[s], jnp.max(st[s], axis=0, keepdims=True)) for s in range(n)]
        p = [jnp.exp2(st[s] - m_new[s]).astype(BF16) for s in range(n)]
        pv = [_dot(jnp.concatenate([val_of(vt, s), ones], axis=0), p[s]) for s in range(n)]
        for s in range(n):
            acc_ref[s] = jnp.exp2(m_prev[s] - m_new[s]) * acc_ref[s] + pv[s]
            m_ref[s] = m_new[s]

    if st_ref is None:
        def body(j, carry):
            update(j, qk(j), False)
            return carry

        lax.fori_loop(0, i, body, 0)
        update(i, qk(i), True)
    else:
        for s, x in enumerate(qk(0)):
            st_ref[s] = x

        def body(j, carry):
            nxt = qk(j + 1)
            update(j, [st_ref[s] for s in range(n)], False)
            for s in range(n):
                st_ref[s] = nxt[s]
            return carry

        lax.fori_loop(0, i, body, 0)
        update(i, [st_ref[s] for s in range(n)], True)
    accs = [acc_ref[s] for s in range(n)]
    return [a[:dv] / a[dv:dv + 1] for a in accs]


def _mla_attn_kernel(q_ref, k_ref, vt_ref, o_ref, m_ref, acc_ref, st_ref):
    q = q_ref[0]
    n = q.shape[1] // LANES
    qs = [q[:, s * LANES:(s + 1) * LANES] for s in range(n)]
    key_of = lambda kb, s: kb[:, s * LANES:(s + 1) * LANES]
    val_of = lambda vt, s: vt[s // 2, (s % 2) * MLA_V:(s % 2 + 1) * MLA_V]
    outs = _attn_streams(pl.program_id(2), qs, key_of, val_of, k_ref, vt_ref, m_ref, acc_ref, st_ref)
    o_ref[0] = jnp.concatenate(outs, axis=0).T.astype(BF16)


def _diff_attn_kernel(q_ref, k_ref, vt_ref, lam_ref, sub_ref, o_ref, m_ref, acc_ref, *, lam_init):
    q = q_ref[0]
    nh = q.shape[1] // LANES
    lane = lax.broadcasted_iota(jnp.int32, (q.shape[0], LANES), 1)
    qs = []
    for h in range(nh):
        q_h = q[:, h * LANES:(h + 1) * LANES]
        zero = jnp.zeros_like(q_h)
        qs += [jnp.where(lane < DIFF_HD, q_h, zero), jnp.where(lane >= DIFF_HD, q_h, zero)]
    key_of = lambda kb, s: kb[:, (s // 2) * LANES:(s // 2 + 1) * LANES]
    val_of = lambda vt, s: vt[s // 2]
    outs = _attn_streams(pl.program_id(2), qs, key_of, val_of, k_ref, vt_ref, m_ref, acc_ref)
    lam = lam_ref[...]
    s1 = jnp.sum(lam[0:1] * lam[1:2], axis=-1, keepdims=True)
    s2 = jnp.sum(lam[2:3] * lam[3:4], axis=-1, keepdims=True)
    lam_full = jnp.exp(s1) - jnp.exp(s2) + lam_init
    for h in range(nh):
        o = (outs[2 * h] - lam_full * outs[2 * h + 1]).T
        o_ref[0, :, h * LANES:(h + 1) * LANES] = (_rms(o, sub_ref[...]) * (1.0 - lam_init)).astype(BF16)


ATTN_STREAMS = 4


def _attn_scratch(bq, dv):
    return [pltpu.VMEM((ATTN_STREAMS, 1, bq), F32), pltpu.VMEM((ATTN_STREAMS, dv + ONES_ROWS, bq), F32)]


def _vt_spec(nk, groups):
    return pl.BlockSpec((1, groups, nk, LANES, ATTN_BLOCK), lambda bi, h, i: (bi, h, 0, 0, 0))


def _mla_attn(qf, kf, vt):
    b, s, _ = qf.shape
    bq = ATTN_BLOCK
    w = ATTN_STREAMS * LANES
    return pl.pallas_call(
        _mla_attn_kernel,
        out_shape=jax.ShapeDtypeStruct((b, s, MLA_HEADS * MLA_V), BF16),
        grid=(b, MLA_HEADS // ATTN_STREAMS, s // bq),
        in_specs=[pl.BlockSpec((1, bq, w), lambda bi, h, i: (bi, i, h)),
                  pl.BlockSpec((1, s, w), lambda bi, h, i: (bi, 0, h)),
                  _vt_spec(s // bq, ATTN_STREAMS // 2)],
        out_specs=pl.BlockSpec((1, bq, ATTN_STREAMS * MLA_V), lambda bi, h, i: (bi, i, h)),
        scratch_shapes=_attn_scratch(bq, MLA_V) + [pltpu.VMEM((ATTN_STREAMS, ATTN_BLOCK, bq), F32)],
        compiler_params=_cparams(("parallel", "parallel", "arbitrary")),
        name="mla_attn",
    )(qf, kf, vt)


def _diff_attn(dq, dk, dvt, lam, subln, lam_init):
    b, s, _ = dq.shape
    bq = ATTN_BLOCK
    nh = ATTN_STREAMS // 2
    blk = lambda rows, im: pl.BlockSpec((1, rows, nh * LANES), im)
    return pl.pallas_call(
        functools.partial(_diff_attn_kernel, lam_init=lam_init),
        out_shape=jax.ShapeDtypeStruct((b, s, DIFF_HEADS * DIFF_V), BF16),
        grid=(b, DIFF_HEADS // nh, s // bq),
        in_specs=[blk(bq, lambda bi, h, i: (bi, i, h)), blk(s, lambda bi, h, i: (bi, 0, h)),
                  _vt_spec(s // bq, nh),
                  pl.BlockSpec(lam.shape, lambda bi, h, i: (0, 0)),
                  pl.BlockSpec(subln.shape, lambda bi, h, i: (0, 0))],
        out_specs=blk(bq, lambda bi, h, i: (bi, i, h)),
        scratch_shapes=_attn_scratch(bq, DIFF_V),
        compiler_params=_cparams(("parallel", "parallel", "arbitrary")),
        name="diff_attn",
    )(dq, dk, dvt, lam, subln)


def _mix_out_kernel(h_ref, a_ref, b_ref, w_ref, g_ref, rwh_ref, rb_ref, tri_ref,
                    hout_ref, hf_ref, route_ref, idx_ref, cnt_ref):
    mixed = jnp.concatenate([a_ref[...], b_ref[...]], axis=-1)
    h = h_ref[...] + _dot(mixed, w_ref[...])
    hout_ref[...] = h
    hf = _rms(h, g_ref[...])
    hi = hf.astype(BF16)
    hf_ref[...] = hi
    lo = (hf - hi.astype(F32)).astype(BF16)
    z = _dot(hi, rwh_ref[...]) + _dot(lo, rwh_ref[...]) + rb_ref[...]
    lane_i = lax.broadcasted_iota(jnp.int32, z.shape, 1)
    lane = lane_i.astype(F32)
    big = float(LANES)
    is_g = lane_i < MOE_GROUPS
    zg = jnp.where(is_g, z, NEG)
    mg = jnp.max(zg, axis=-1, keepdims=True)
    g_idx = jnp.min(jnp.where(zg == mg, lane, big), axis=-1, keepdims=True)
    pg_top = 1.0 / jnp.sum(jnp.where(is_g, jnp.exp(zg - mg), 0.0), axis=-1, keepdims=True)
    grp_of_lane = jnp.right_shift(lane_i - MOE_GROUPS, int(math.log2(MOE_EPG))).astype(F32)
    in_grp = (lane_i >= MOE_GROUPS) & (lane_i < MOE_GROUPS + MOE_EXPERTS) & (grp_of_lane == g_idx)
    ze = jnp.where(in_grp, z, NEG)
    m1 = jnp.max(ze, axis=-1, keepdims=True)
    i1 = jnp.min(jnp.where(ze == m1, lane, big), axis=-1, keepdims=True)
    ze2 = jnp.where(lane == i1, NEG, ze)
    m2 = jnp.max(ze2, axis=-1, keepdims=True)
    i2 = jnp.min(jnp.where(ze2 == m2, lane, big), axis=-1, keepdims=True)
    r = jnp.exp(m2 - m1)
    w1 = 1.0 / (1.0 + r)
    w2 = r / (1.0 + r)
    e1 = i1 - float(MOE_GROUPS)
    e2 = i2 - float(MOE_GROUPS)
    @pl.when(pl.program_id(0) == 0)
    def _():
        cnt_ref[...] = jnp.zeros(cnt_ref.shape, F32)

    both = jnp.where((lane == i1) | (lane == i2), 1.0, 0.0)
    before = _dot(tri_ref[...], both.astype(BF16)) + cnt_ref[0:1, :]
    rank1 = jnp.sum(jnp.where(lane == i1, before, 0.0), axis=-1, keepdims=True)
    rank2 = jnp.sum(jnp.where(lane == i2, before, 0.0), axis=-1, keepdims=True)
    cnt_ref[...] = cnt_ref[...] + jnp.sum(both, axis=0, keepdims=True)
    vals = (e1, e2, pg_top * w1, pg_top * w2, rank1, rank2)
    out = jnp.zeros(z.shape, F32)
    for col, val in enumerate(vals):
        out = jnp.where(lane_i == col, val, out)
    route_ref[...] = out
    idx_ref[0] = out.T[0:8].astype(jnp.int32)


def _mix_out(h, a, b, w, g, rwh, rb, tri):
    n, d = h.shape
    tm = ROW_TILE
    row = lambda i: (i, 0)
    full = lambda i: (0, 0)
    wspec = lambda t: pl.BlockSpec(t.shape, full)
    return pl.pallas_call(
        _mix_out_kernel,
        out_shape=[jax.ShapeDtypeStruct((n, d), F32), jax.ShapeDtypeStruct((n, d), BF16),
                   jax.ShapeDtypeStruct((n, LANES), F32), jax.ShapeDtypeStruct((n // tm, 8, tm), jnp.int32),
                   jax.ShapeDtypeStruct((8, LANES), F32)],
        grid=(n // tm,),
        in_specs=[pl.BlockSpec((tm, d), row), pl.BlockSpec((tm, a.shape[1]), row), pl.BlockSpec((tm, b.shape[1]), row),
                  wspec(w), wspec(g), wspec(rwh), wspec(rb), wspec(tri)],
        out_specs=[pl.BlockSpec((tm, d), row), pl.BlockSpec((tm, d), row), pl.BlockSpec((tm, LANES), row),
                   pl.BlockSpec((1, 8, tm), lambda i: (i, 0, 0)), pl.BlockSpec((8, LANES), full)],
        compiler_params=_cparams(("arbitrary",)),
        name="mix_out",
    )(h, a, b, w, g, rwh, rb, tri)


def _expert_kernel(te_ref, nv_ref, x_ref, wg_ref, wu_ref, wd_ref, y_ref, wgb_ref, wub_ref, wdb_ref):
    t = pl.program_id(0)

    @pl.when((t == 0) | (te_ref[t] != te_ref[jnp.maximum(t - 1, 0)]))
    def _():
        wgb_ref[...] = wg_ref[0, 0].astype(BF16)
        wub_ref[...] = wu_ref[0, 0].astype(BF16)
        wdb_ref[...] = wd_ref[0, 0].astype(BF16)

    @pl.when(t < nv_ref[0])
    def _():
        x = x_ref[...]
        a = _dot(x, wgb_ref[...])
        up = _dot(x, wub_ref[...])
        act = (a * _sigmoid(a) * up).astype(BF16)
        y_ref[...] = _dot(act, wdb_ref[...]).astype(BF16)

    @pl.when(t >= nv_ref[0])
    def _():
        y_ref[...] = jnp.zeros(y_ref.shape, BF16)


def _experts(xs, tile_expert, n_valid, wg, wu, wd, layer):
    p, d = xs.shape
    tm = EXPERT_TILE
    ff = wg.shape[3]
    grid_spec = pltpu.PrefetchScalarGridSpec(
        num_scalar_prefetch=2,
        grid=(p // tm,),
        in_specs=[pl.BlockSpec((tm, d), lambda t, te, nv: (t, 0)),
                  pl.BlockSpec((1, 1, d, ff), lambda t, te, nv: (layer, te[t], 0, 0)),
                  pl.BlockSpec((1, 1, d, ff), lambda t, te, nv: (layer, te[t], 0, 0)),
                  pl.BlockSpec((1, 1, ff, d), lambda t, te, nv: (layer, te[t], 0, 0))],
        out_specs=pl.BlockSpec((tm, d), lambda t, te, nv: (t, 0)),
        scratch_shapes=[pltpu.VMEM((d, ff), BF16), pltpu.VMEM((d, ff), BF16), pltpu.VMEM((ff, d), BF16)],
    )
    return pl.pallas_call(
        _expert_kernel,
        out_shape=jax.ShapeDtypeStruct((p, d), BF16),
        grid_spec=grid_spec,
        compiler_params=_cparams(("arbitrary",)),
        name="experts",
    )(tile_expert, n_valid, xs, wg, wu, wd)


def _add_experts(h_ref, y1_ref, y2_ref, route_ref):
    route = route_ref[...]
    return h_ref[...] + route[:, 2:3] * y1_ref[...].astype(F32) + route[:, 3:4] * y2_ref[...].astype(F32)


def _final_kernel(h_ref, y1_ref, y2_ref, route_ref, g_ref, o_ref):
    o_ref[...] = _rms(_add_experts(h_ref, y1_ref, y2_ref, route_ref), g_ref[...])


def _final(h, y, route, g):
    n, d = h.shape
    tm = ROW_TILE
    row = lambda i: (i, 0)
    second = lambda i: (i + n // tm, 0)
    return pl.pallas_call(
        _final_kernel,
        out_shape=jax.ShapeDtypeStruct((n, d), F32),
        grid=(n // tm,),
        in_specs=[pl.BlockSpec((tm, d), row), pl.BlockSpec((tm, d), row), pl.BlockSpec((tm, d), second),
                  pl.BlockSpec((tm, LANES), row), pl.BlockSpec(g.shape, lambda i: (0, 0))],
        out_specs=pl.BlockSpec((tm, d), row),
        compiler_params=_cparams(("parallel",)),
        name="moe_final",
    )(h, y, y, route, g)


def _moe(hf, idx, counts, wg, wu, wd, layer):
    n, d = hf.shape
    tm = EXPERT_TILE
    ids = jnp.arange(MOE_EXPERTS, dtype=jnp.int32)
    counts = counts[0, MOE_GROUPS:MOE_GROUPS + MOE_EXPERTS].astype(jnp.int32)
    padded = ((counts + tm - 1) // tm) * tm
    ends = jnp.cumsum(padded)
    starts = ends - padded
    start_of = lambda e: jnp.sum(jnp.where(e[:, None] == ids[None, :], starts[None, :], 0), axis=-1)
    e1, e2 = idx[:, 0, :].reshape(n), idx[:, 1, :].reshape(n)
    dest1 = start_of(e1) + idx[:, 4, :].reshape(n)
    dest2 = start_of(e2) + idx[:, 5, :].reshape(n)
    n_rows = 2 * n + MOE_EXPERTS * tm
    tok = jnp.arange(n, dtype=jnp.int32)
    tok_for_row = (jnp.arange(n_rows, dtype=jnp.int32) % n).at[jnp.concatenate([dest1, dest2])].set(
        jnp.concatenate([tok, tok]), mode="promise_in_bounds", unique_indices=True)
    take = functools.partial(jnp.take, axis=0, mode="clip")
    tile_start = jnp.arange(n_rows // tm, dtype=jnp.int32) * tm
    tile_expert = jnp.minimum(jnp.sum((ends[None, :] <= tile_start[:, None]).astype(jnp.int32), axis=1),
                              MOE_EXPERTS - 1)
    n_valid = (ends[-1] // tm).astype(jnp.int32).reshape(1)
    xs = take(hf, tok_for_row)
    ys = _experts(xs, tile_expert, n_valid, wg, wu, wd, layer)
    return take(ys, jnp.concatenate([dest1, dest2]))


def _od_in_kernel(h_ref, y1_ref, y2_ref, route_ref, g_ref, w_ref, hout_ref, ur_ref, uh_ref):
    h = _add_experts(h_ref, y1_ref, y2_ref, route_ref)
    hout_ref[...] = h
    hn = _rms(h, g_ref[...]).astype(BF16)
    u = _dot(hn, w_ref[...])
    ur_ref[...] = u[:, :RWKV_IN].astype(BF16)
    uh_ref[...] = u[:, RWKV_IN:].astype(BF16)


def _od_in(h, y, route, g, w):
    n, d = h.shape
    tm = ROW_TILE
    row = lambda i: (i, 0)
    rspec = lambda width: pl.BlockSpec((tm, width), row)
    return pl.pallas_call(
        _od_in_kernel,
        out_shape=[jax.ShapeDtypeStruct((n, d), F32), jax.ShapeDtypeStruct((n, RWKV_IN), BF16),
                   jax.ShapeDtypeStruct((n, HGRN_IN), BF16)],
        grid=(n // tm,),
        in_specs=[rspec(d), rspec(d), pl.BlockSpec((tm, d), lambda i: (i + n // tm, 0)), rspec(LANES),
                  pl.BlockSpec(g.shape, lambda i: (0, 0)), pl.BlockSpec(w.shape, lambda i: (0, 0))],
        out_specs=[rspec(d), rspec(RWKV_IN), rspec(HGRN_IN)],
        compiler_params=_cparams(("parallel",)),
        name="od_in",
    )(h, y, y, route, g, w)


def _rwkv_kernel(u_ref, mu_ref, w0_ref, w2_ref, a0_ref, a2_ref, g2_ref, kk_ref, ka_ref, rk_ref,
                 lnw_ref, lnb_ref, tri_ref, seg_ref, o_ref, prev_ref, ht_ref):
    c = pl.program_id(1)

    @pl.when(c == 0)
    def _():
        prev_ref[...] = jnp.zeros(prev_ref.shape, F32)
        ht_ref[...] = jnp.zeros(ht_ref.shape, F32)

    nb = u_ref.shape[0]
    u = u_ref[...].reshape(nb * CHUNK, RWKV_IN).astype(F32)
    rows = lax.broadcasted_iota(jnp.int32, u.shape, 0)
    u_prev = pltpu.roll(u, 1, 0)
    for bi in range(nb):
        u_prev = jnp.where(rows == bi * CHUNK, prev_ref[bi], u_prev)
        prev_ref[bi] = u[(bi + 1) * CHUNK - 1:(bi + 1) * CHUNK, :]
    us = u + (u_prev - u) * mu_ref[...]
    w = RWKV_W
    r = us[:, 0:w]
    k = us[:, w:2 * w]
    v = us[:, 2 * w:3 * w]
    x12 = us[:, 3 * w:3 * w + LANES]
    xg = us[:, 3 * w + LANES:]
    seg_pair = seg_ref[...]

    def seg(x, split=False):
        mm = _split_dot if split else (lambda v, m: _dot(v.astype(BF16), m))
        return jnp.concatenate([mm(x[:, p * LANES:(p + 1) * LANES], seg_pair)
                                for p in range(RWKV_HEADS // 2)], axis=1)

    tri = tri_ref[...]

    wl = w0_ref[...] + _dot(jnp.tanh(x12).astype(BF16), w2_ref[...])
    nwl = -wl
    softplus = jnp.maximum(nwl, 0.0) + jnp.log1p(jnp.exp(-jnp.abs(nwl)))
    lw = -jnp.exp(-softplus - 0.5)
    a = _sigmoid(a0_ref[...] + _dot(x12.astype(BF16), a2_ref[...]))
    g = _dot(_sigmoid(xg).astype(BF16), g2_ref[...])
    kk = k * kk_ref[...]
    kk = kk * lax.rsqrt(jnp.maximum(seg(kk * kk, split=True), 1e-24))
    k2 = k * (1.0 + (a - 1.0) * ka_ref[...])
    a_in = -kk
    b_in = kk * a

    lg = _split_dot(lw, tri, left=True)
    rbs = [slice(bi * CHUNK, (bi + 1) * CHUNK) for bi in range(nb)]
    lg_c = [lg[(bi + 1) * CHUNK - 1:(bi + 1) * CHUNK, :] for bi in range(nb)]
    e_neg = jnp.exp(-lg)
    e_rel = jnp.concatenate([jnp.exp(lg_c[bi] - lg[rbs[bi]]) for bi in range(nb)], axis=0)
    g_c = [jnp.exp(x) for x in lg_c]
    at = a_in * jnp.exp(lg - lw)
    rt = r * jnp.exp(lg)
    kt = k2 * e_neg
    bt = b_in * e_neg
    kh = k2 * e_rel
    bh = b_in * e_rel

    c2 = 2 * CHUNK
    ri = lax.broadcasted_iota(jnp.int32, (c2, c2), 0)
    ci = lax.broadcasted_iota(jnp.int32, (c2, c2), 1)
    sh = int(math.log2(CHUNK))
    same = jnp.right_shift(ri, sh) == jnp.right_shift(ci, sh)
    strict = same & (ri > ci)
    incl = same & (ri >= ci)
    eye = jnp.where(ri == ci, 1.0, 0.0).astype(F32)
    lane = lax.broadcasted_iota(jnp.int32, (CHUNK, LANES), 1)
    lo_half = lane < RWKV_HD

    def bd(x):
        return jnp.concatenate([jnp.where(lo_half, x, 0.0), jnp.where(lo_half, 0.0, x)], axis=0)

    npair = RWKV_HEADS // 2
    pairs = range(nb * npair)
    sls = [(rbs[c // npair], slice((c % npair) * LANES, (c % npair + 1) * LANES)) for c in pairs]
    bdb = lambda x: [bd(x[sl]).astype(BF16) for sl in sls]
    at_b = [bd(at[sl]) for sl in sls]
    rt_b = [bd(rt[sl]) for sl in sls]
    kt_b, bt_b, kh_b, bh_b, v_b = bdb(kt), bdb(bt), bdb(kh), bdb(bh), bdb(v)
    hts = [ht_ref[p] for p in pairs]
    htb = [t.astype(BF16) for t in hts]
    sc = [_dot_nt(jnp.concatenate([at_b[p], rt_b[p]], axis=0).astype(BF16),
                  jnp.concatenate([kt_b[p], bt_b[p]], axis=0)) for p in pairs]
    a_ak = [jnp.where(strict, s[:c2, :c2], 0.0).astype(BF16) for s in sc]
    a_ab = [jnp.where(strict, s[:c2, c2:], 0.0) for s in sc]
    a_rk = [jnp.where(incl, s[c2:, :c2], 0.0).astype(BF16) for s in sc]
    a_rb = [jnp.where(incl, s[c2:, c2:], 0.0).astype(BF16) for s in sc]
    x2 = [_dot(a_ak[p], v_b[p]) for p in pairs]
    e1 = [_dot(a_rk[p], v_b[p]) for p in pairs]
    d1 = [_dot_tn(v_b[p], kh_b[p]) for p in pairs]
    pw = a_ab
    t_inv = [eye + a for a in a_ab]
    for _ in range(int(math.log2(CHUNK)) - 1):
        pwb = [x.astype(BF16) for x in pw]
        pw = [_dot(x, x) for x in pwb]
        t_inv = [t_inv[p] + _dot(t_inv[p].astype(BF16), pw[p].astype(BF16)) for p in pairs]
    txb = [_dot(t_inv[p].astype(BF16), jnp.concatenate([at_b[p], x2[p]], axis=1).astype(BF16)).astype(BF16)
           for p in pairs]
    qe = [_dot(a_rb[p], txb[p]) for p in pairs]
    gd = [_dot_tn(txb[p], bh_b[p]) for p in pairs]
    y_b = [_dot_nt((rt_b[p] + qe[p][:, :LANES]).astype(BF16), htb[p]) + e1[p] + qe[p][:, LANES:] for p in pairs]
    for p in pairs:
        upd = hts[p] * g_c[p // npair][:, sls[p][1]] + _dot(htb[p], gd[p][:LANES].astype(BF16)) + d1[p] + gd[p][LANES:]
        ht_ref[p] = jnp.where(same, upd, 0.0)
    y_p = [t[:CHUNK] + t[CHUNK:] for t in y_b]
    y = jnp.concatenate([jnp.concatenate(y_p[bi * npair:(bi + 1) * npair], axis=1) for bi in range(nb)],
                        axis=0)

    inv_n = 1.0 / RWKV_HD
    mean = seg(y) * inv_n
    dlt = y - mean
    var = seg(dlt * dlt) * inv_n
    yn = dlt * lax.rsqrt(var + RWKV_GN_EPS) * lnw_ref[...] + lnb_ref[...]
    bonus = seg(r * k2 * rk_ref[...]) * v
    o_ref[...] = ((yn + bonus) * g).astype(BF16).reshape(nb, CHUNK, RWKV_W)


def _rwkv(u, mu, w0, w2p, a0, a2p, g2, k_k, k_a, r_k, ln_w, ln_b, tri, seg):
    b, s, _ = u.shape
    nb = tri.shape[0] // CHUNK
    full = lambda bi, ci: (0, 0)
    wspec = lambda t: pl.BlockSpec(t.shape, full)
    params = [mu, w0, w2p, a0, a2p, g2, k_k, k_a, r_k, ln_w, ln_b, tri, seg]
    return pl.pallas_call(
        _rwkv_kernel,
        out_shape=jax.ShapeDtypeStruct((b, s, RWKV_W), BF16),
        grid=(b // nb, s // CHUNK),
        in_specs=[pl.BlockSpec((nb, CHUNK, RWKV_IN), lambda bi, ci: (bi, ci, 0))] + [wspec(t) for t in params],
        out_specs=pl.BlockSpec((nb, CHUNK, RWKV_W), lambda bi, ci: (bi, ci, 0)),
        scratch_shapes=[pltpu.VMEM((nb, 1, RWKV_IN), F32), pltpu.VMEM((nb * RWKV_HEADS // 2, LANES, LANES), F32)],
        compiler_params=_cparams(("parallel", "arbitrary")),
        name="rwkv7",
    )(u, *params)


def _hgrn_kernel(u_ref, lb_ref, on_ref, tri_ref, ones_ref, sel_ref, o_ref, st_ref):
    c = pl.program_id(1)

    @pl.when(c == 0)
    def _():
        st_ref[...] = jnp.zeros(st_ref.shape, F32)

    nb = u_ref.shape[0]
    u = u_ref[...].reshape(nb * CHUNK, HGRN_IN).astype(F32)
    lbp = lb_ref[...]
    mx = jnp.maximum(lbp[0:1], lbp[1:2])
    e0 = jnp.exp(lbp[0:1] - mx)
    e1 = jnp.exp(lbp[1:2] - mx)
    p0 = e0 / (e0 + e1)
    p1 = e1 / (e0 + e1)
    lb = (p0 + p1) - p0
    tri = tri_ref[...]
    ones = ones_ref[...]
    sel = sel_ref[...]
    d = HGRN_D
    w = HGRN_W
    heads = range(HGRN_HEADS)
    sls = [slice(h * d, (h + 1) * d) for h in heads]
    q, z, iv, gt = u[:, :w], u[:, w:2 * w], u[:, 2 * w:3 * w], u[:, 3 * w:]
    qs = q * _sigmoid(q)
    log_sig = jnp.minimum(z, 0.0) - jnp.log1p(jnp.exp(-jnp.abs(z)))
    x1 = jnp.log(lb)
    x2 = jnp.log1p(-lb) + log_sig
    log_f = jnp.maximum(x1, x2) + jnp.log1p(jnp.exp(-jnp.abs(x1 - x2)))
    key = (1.0 - lb) * _sigmoid(-z)
    bc = _split_dot(log_f, tri, left=True)
    seqs = range(nb)
    rbs = [slice(i * CHUNK, (i + 1) * CHUNK) for i in seqs]
    b_l = [bc[(i + 1) * CHUNK - 1:(i + 1) * CHUNK] for i in seqs]
    ivb = iv.astype(BF16)
    qe = (qs * jnp.exp(bc)).astype(BF16)
    kl = (key * jnp.concatenate([jnp.exp(b_l[i] - bc[rbs[i]]) for i in seqs], axis=0)).astype(BF16)
    nh = HGRN_HEADS
    sts = [st_ref[c] for c in range(nb * nh)]
    o_inter = [_dot_nt(qe[rbs[c // nh], sls[c % nh]], sts[c].astype(BF16)) for c in range(nb * nh)]
    for c in range(nb * nh):
        i, sl = c // nh, sls[c % nh]
        st_ref[c] = sts[c] * jnp.exp(b_l[i])[:, sl] + _dot_tn(ivb[rbs[i], sl], kl[rbs[i], sl])

    nblk = CHUNK // SUB
    hrow = lax.broadcasted_iota(jnp.int32, (HALF, w), 0)
    p_all, v_tile, sc_off, sc_half = [], [], [], []
    for blk in range(nb * nblk):
        base = (blk // nblk) * CHUNK
        r0 = base + (blk % nblk) * SUB
        b_i, q_i, k_i = bc[r0:r0 + SUB], qs[r0:r0 + SUB], key[r0:r0 + SUB]
        rows = []
        for t in range(SUB):
            lo = (t // HALF) * HALF
            rows.append(q_i[t:t + 1] * k_i[lo:lo + HALF]
                        * jnp.where(hrow <= t - lo, jnp.exp(b_i[t:t + 1] - b_i[lo:lo + HALF]), 0.0))
        p_all.append(jnp.concatenate(rows, axis=0).astype(BF16))
        v_tile.append(jnp.concatenate([iv[r0:r0 + HALF]] * HALF + [iv[r0 + HALF:r0 + SUB]] * HALF, axis=0))
        b_h = b_i[HALF - 1:HALF]
        qh = (q_i[HALF:] * jnp.exp(b_i[HALF:] - b_h)).astype(BF16)
        kh = (k_i[:HALF] * jnp.exp(b_h - b_i[:HALF])).astype(BF16)
        sc_half.append([_dot_nt(qh[:, sl], kh[:, sl]).astype(BF16) for sl in sls])
        if r0 > base:
            b_m = bc[r0 - 1:r0]
            qp = (q_i * jnp.exp(b_i - b_m)).astype(BF16)
            kp = (key[base:r0] * jnp.exp(b_m - bc[base:r0])).astype(BF16)
            sc_off.append([_dot_nt(qp[:, sl], kp[:, sl]).astype(BF16) for sl in sls])
        else:
            sc_off.append(None)
    rs = [jnp.concatenate([_dot(p[:, sl], ones) for sl in sls], axis=1) for p in p_all]
    rv = [(rs[blk] * v_tile[blk]).astype(BF16) for blk in range(nb * nblk)]
    for c in range(nb * nh):
        i, sl = c // nh, sls[c % nh]
        parts = []
        for blk in range(i * nblk, (i + 1) * nblk):
            r0 = i * CHUNK + (blk % nblk) * SUB
            lower = _dot(sc_half[blk][c % nh], iv[r0:r0 + HALF, sl].astype(BF16))
            o_i = _dot(sel, rv[blk][:, sl]) + jnp.concatenate([jnp.zeros_like(lower), lower], axis=0)
            if sc_off[blk] is not None:
                o_i = o_i + _dot(sc_off[blk][c % nh], ivb[i * CHUNK:i * CHUNK + (blk % nblk) * SUB, sl])
            parts.append(o_i)
        o = o_inter[c] + jnp.concatenate(parts, axis=0)
        on = _rms(o, on_ref[:, sl])
        g_h = gt[rbs[i], sl]
        o_ref[i, :, sl] = (on * (g_h * _sigmoid(g_h))).astype(BF16)


def _hgrn(u, lbp, o_norm, tri, ones, sel):
    b, s, _ = u.shape
    nb = tri.shape[0] // CHUNK
    full = lambda bi, ci: (0, 0)
    wspec = lambda t: pl.BlockSpec(t.shape, full)
    return pl.pallas_call(
        _hgrn_kernel,
        out_shape=jax.ShapeDtypeStruct((b, s, HGRN_W), BF16),
        grid=(b // nb, s // CHUNK),
        in_specs=[pl.BlockSpec((nb, CHUNK, HGRN_IN), lambda bi, ci: (bi, ci, 0)), wspec(lbp), wspec(o_norm),
                  wspec(tri), wspec(ones), wspec(sel)],
        out_specs=pl.BlockSpec((nb, CHUNK, HGRN_W), lambda bi, ci: (bi, ci, 0)),
        scratch_shapes=[pltpu.VMEM((nb * HGRN_HEADS, HGRN_D, HGRN_D), F32)],
        compiler_params=_cparams(("parallel", "arbitrary")),
        name="hgrn2",
    )(u, lbp, o_norm, tri, ones, sel)


def _rope_tables(seq):
    pos = jnp.arange(seq, dtype=F32)[:, None]

    def cs(half):
        inv = jnp.power(ROPE_THETA, -jnp.arange(half, dtype=F32) / half)
        ang = pos * inv[None, :]
        return jnp.cos(ang), jnp.sin(ang)

    cd, sd = cs(DIFF_HD // 2)
    cos_d = jnp.tile(jnp.concatenate([cd, cd], axis=1), (1, LANES // DIFF_HD))
    sin_d = jnp.tile(jnp.concatenate([-sd, sd], axis=1), (1, LANES // DIFF_HD))
    cm, sm = cs(MLA_ROPE // 2)
    one = jnp.ones((seq, MLA_NOPE), F32)
    tail = LANES - MLA_QK
    cos_m = jnp.concatenate([one, cm, cm, jnp.ones((seq, tail), F32)], axis=1)
    sin_m = jnp.concatenate([0 * one, -sm, sm, jnp.zeros((seq, tail), F32)], axis=1)
    return cos_d, sin_d, cos_m, sin_m


def _router_weights(w_group, b_group, w_expert, b_expert):
    d = w_group.shape[0]
    pad = LANES - MOE_GROUPS - MOE_EXPERTS
    w = jnp.concatenate([w_group, w_expert, jnp.zeros((d, pad), F32)], axis=1)
    bias = jnp.concatenate([b_group, b_expert, jnp.zeros((pad,), F32)])[None, :]
    return w.astype(BF16), bias


def kernel(x, norm_mix, norm_ffn, norm_final, ev_w_in, ev_w_out, mla_q_norm, mla_w_q_up, mla_kv_norm, mla_w_kv_up, diff_lambda, diff_subln, od_w_in, od_w_out, rwkv_mu, rwkv_w0, rwkv_w2, rwkv_a0, rwkv_a2, rwkv_g2, rwkv_k_k, rwkv_k_a, rwkv_r_k, rwkv_ln_w, rwkv_ln_b, hgrn_lb, hgrn_o_norm, moe_w_group, moe_b_group, moe_w_expert, moe_b_expert, moe_w_gate, moe_w_up, moe_w_down):
    b, s, d = x.shape
    n = b * s
    assert norm_mix.shape[0] == 2 and hgrn_lb.shape[0] == 2
    assert s % ATTN_BLOCK == 0 and s % ROW_TILE == 0 and ATTN_BLOCK % CHUNK == 0
    row2 = lambda t: t.reshape(1, -1)
    h = x.reshape(n, d)

    w_in = ev_w_in[0]
    o1, o2, o3 = MLA_LORA, 2 * MLA_LORA, 2 * MLA_LORA + MLA_ROPE
    kr_pad = jnp.zeros((d, LANES), F32).at[:, MLA_NOPE:MLA_QK].set(w_in[:, o2:o3])
    w0 = jnp.concatenate([w_in[:, :o2], kr_pad, w_in[:, o3:]], axis=1).astype(BF16)
    wq = mla_w_q_up[0].reshape(MLA_LORA, MLA_HEADS, MLA_QK)
    wq = jnp.pad(wq, ((0, 0), (0, 0), (0, LANES - MLA_QK))).reshape(MLA_LORA, MLA_HEADS * LANES).astype(BF16)
    wkv = mla_w_kv_up[0].reshape(MLA_LORA, MLA_HEADS, MLA_NOPE + MLA_V)
    wk = jnp.pad(wkv[:, :, :MLA_NOPE], ((0, 0), (0, 0), (0, LANES - MLA_NOPE)))
    wk = wk.reshape(MLA_LORA, MLA_HEADS * LANES).astype(BF16)
    wv = wkv[:, :, MLA_NOPE:].reshape(MLA_LORA, MLA_HEADS * MLA_V).astype(BF16)
    cos_d, sin_d, cos_m, sin_m = _rope_tables(s)
    qf, kf, vm, dq, dk, dv = _ev_in(h, row2(norm_mix[0]), w0, row2(mla_q_norm[0]), row2(mla_kv_norm[0]),
                                    wq, wk, wv, cos_d, sin_d, cos_m, sin_m, s)
    r3 = lambda t: t.reshape(b, s, t.shape[-1])
    o_mla = _mla_attn(r3(qf), r3(kf), vm)
    lam_init = 0.8 - 0.6 * math.exp(-0.3 * 0)
    o_diff = _diff_attn(r3(dq), r3(dk), dv, diff_lambda[0], row2(diff_subln[0]), lam_init)
    rwh, rb = _router_weights(moe_w_group[0], moe_b_group[0], moe_w_expert[0], moe_b_expert[0])
    ti = jnp.arange(ROW_TILE)
    tri_rows = (ti[None, :] < ti[:, None]).astype(BF16)
    h, hf, route, ridx, counts = _mix_out(h, o_mla.reshape(n, -1), o_diff.reshape(n, -1), ev_w_out[0].astype(BF16),
                                    row2(norm_ffn[0]), rwh, rb, tri_rows)
    y = _moe(hf, ridx, counts, moe_w_gate, moe_w_up, moe_w_down, 0)

    h, ur, uh = _od_in(h, y, route, row2(norm_mix[1]), od_w_in[0].astype(BF16))
    zpad = jnp.zeros((RWKV_DECAY_LORA, RWKV_W), F32)
    w2p = jnp.concatenate([rwkv_w2[0], zpad], axis=0).astype(BF16)
    a2p = jnp.concatenate([zpad, rwkv_a2[0]], axis=0).astype(BF16)
    ci = jnp.arange(CHUNK)
    tri = (ci[None, :] <= ci[:, None]).astype(BF16)
    li = jnp.arange(LANES) // RWKV_HD
    seg = (li[:, None] == li[None, :]).astype(BF16)
    o_c = _rwkv(ur.reshape(b, s, RWKV_IN), row2(rwkv_mu[0]), row2(rwkv_w0[0]), w2p, row2(rwkv_a0[0]), a2p,
                rwkv_g2[0].astype(BF16), row2(rwkv_k_k[0]), row2(rwkv_k_a[0]), row2(rwkv_r_k[0]),
                row2(rwkv_ln_w[0]), row2(rwkv_ln_b[0]),
                jnp.kron(jnp.eye(math.gcd(b, RWKV_SEQS), dtype=BF16), tri), seg)
    pt = jnp.arange(SUB * HALF) // HALF
    ps = jnp.arange(SUB * HALF) % HALF
    sel = ((pt[None, :] == jnp.arange(SUB)[:, None]) & (ps <= pt % HALF)[None, :]).astype(BF16)
    o_d = _hgrn(uh.reshape(b, s, HGRN_IN), hgrn_lb, row2(hgrn_o_norm[0]),
                jnp.kron(jnp.eye(math.gcd(b, HGRN_SEQS), dtype=BF16), tri), jnp.ones((LANES, LANES), BF16), sel)
    rwh, rb = _router_weights(moe_w_group[1], moe_b_group[1], moe_w_expert[1], moe_b_expert[1])
    h, hf, route, ridx, counts = _mix_out(h, o_c.reshape(n, -1), o_d.reshape(n, -1), od_w_out[0].astype(BF16),
                                    row2(norm_ffn[1]), rwh, rb, tri_rows)
    y = _moe(hf, ridx, counts, moe_w_gate, moe_w_up, moe_w_down, 1)
    out = _final(h, y, route, row2(norm_final))
    return out.reshape(b, s, d)
```
